```python
import jax, jax.numpy as jnp
from jax import lax
import numpy as np

D_MODEL = 2048
BATCH = 1
SEQ = 8192
DEPTH = 2
DEC_BATCH = 16
DEC_SEQ = 16
PAST_LEN = 1024

CHUNK = 64
N_MIXERS = 2
N_ATTN_LAYERS = (DEPTH + 1) // 2
N_CONV_LAYERS = DEPTH // 2
HEAD_DIM = 64
N_HEADS = D_MODEL // HEAD_DIM
N_KV_HEADS = 8
GROUP = N_HEADS // N_KV_HEADS
ATTN_WIDTH = N_HEADS * HEAD_DIM
KV_WIDTH = N_KV_HEADS * HEAD_DIM
WINDOW = 128
WIN_CHUNKS = WINDOW // CHUNK
ROT_DIM = HEAD_DIM // 4
ROPE_THETA = 500000.0
CONV_WIDTH = 31
CONV_CH = D_MODEL
LN_EPS = 1e-5
DEEPNORM_ALPHA = (2.0 * DEPTH) ** 0.25
DEEPNORM_BETA = (8.0 * DEPTH) ** -0.25

kernel_name = 'hybrid_swa_sink_conformer_conv_stream_step'


def layer_norm(x, g, b):
    xf = x.astype(jnp.float32)
    mu = jnp.mean(xf, axis=-1, keepdims=True)
    var = jnp.mean(jnp.square(xf - mu), axis=-1, keepdims=True)
    y = (xf - mu) * lax.rsqrt(var + LN_EPS) * g.astype(jnp.float32) + b.astype(jnp.float32)
    return y.astype(x.dtype)


def partial_rope(x, pos):
    half = ROT_DIM // 2
    inv = ROPE_THETA ** (-jnp.arange(half, dtype=jnp.float32) * (2.0 / ROT_DIM))
    ang = pos.astype(jnp.float32)[:, None] * inv[None, :]
    cos = jnp.cos(ang)[:, None, :]
    sin = jnp.sin(ang)[:, None, :]
    xf = x.astype(jnp.float32)
    x1 = xf[..., :half]
    x2 = xf[..., half:ROT_DIM]
    out = jnp.concatenate([x1 * cos - x2 * sin, x2 * cos + x1 * sin, xf[..., ROT_DIM:]], axis=-1)
    return out.astype(x.dtype)


def sink_probs(s, sink):
    sk = sink.astype(jnp.float32).reshape(N_KV_HEADS, GROUP, 1)
    m = jnp.maximum(jnp.max(s, axis=-1), sk)
    p = jnp.exp(s - m[..., None])
    denom = jnp.sum(p, axis=-1, keepdims=True) + jnp.exp(sk - m)[..., None]
    return p / denom


def attn_project(x, w_in, pos):
    B, T = x.shape[0], x.shape[1]
    h = x @ w_in
    q, k, v, g = jnp.split(h, [ATTN_WIDTH, ATTN_WIDTH + KV_WIDTH, ATTN_WIDTH + 2 * KV_WIDTH], axis=-1)
    q = partial_rope(q.reshape(B, T, N_HEADS, HEAD_DIM), pos)
    k = partial_rope(k.reshape(B, T, N_KV_HEADS, HEAD_DIM), pos)
    v = v.reshape(B, T, N_KV_HEADS, HEAD_DIM)
    return q, k, v, g


def attn_prompt(x, w_in, sink, w_out, wc):
    B, S = x.shape[0], x.shape[1]
    nc = S // CHUNK
    pos = jnp.arange(S, dtype=jnp.float32)
    q, k, v, g = attn_project(x, w_in, pos)
    scale = HEAD_DIM ** -0.5
    qb = q.reshape(B, nc, CHUNK, N_KV_HEADS, GROUP, HEAD_DIM)
    pad = ((0, 0), (WIN_CHUNKS * CHUNK, 0), (0, 0), (0, 0))
    kp = jnp.pad(k, pad).reshape(B, nc + WIN_CHUNKS, CHUNK, N_KV_HEADS, HEAD_DIM)
    vp = jnp.pad(v, pad).reshape(B, nc + WIN_CHUNKS, CHUNK, N_KV_HEADS, HEAD_DIM)
    kb = jnp.concatenate([kp[:, j:j + nc] for j in range(WIN_CHUNKS + 1)], axis=2)
    vb = jnp.concatenate([vp[:, j:j + nc] for j in range(WIN_CHUNKS + 1)], axis=2)
    key_chunk = (jnp.arange(nc)[:, None] - WIN_CHUNKS
                 + jnp.repeat(jnp.arange(WIN_CHUNKS + 1), CHUNK)[None, :])
    valid = key_chunk >= 0
    s = jnp.einsum('bcqkgd,bclkd->bckgql', qb, kb, preferred_element_type=jnp.float32) * scale
    s = jnp.where(valid[None, :, None, None, None, :], s, -jnp.inf)
    p = sink_probs(s, sink)
    o = jnp.einsum('bckgql,bclkd->bcqkgd', p.astype(v.dtype), vb).reshape(B, S, ATTN_WIDTH)
    y = (o * jax.nn.silu(g)) @ w_out
    return y, k[:, S - wc:], v[:, S - wc:]


def attn_sample(x, w_in, sink, w_out, cache_k, cache_v):
    B, T = x.shape[0], x.shape[1]
    wc = cache_k.shape[1]
    pos = PAST_LEN + jnp.arange(T, dtype=jnp.float32)
    q, k, v, g = attn_project(x, w_in, pos)
    scale = HEAD_DIM ** -0.5
    kk = jnp.concatenate([cache_k.astype(k.dtype), k], axis=1)
    vv = jnp.concatenate([cache_v.astype(v.dtype), v], axis=1)
    qh = q.reshape(B, T, N_KV_HEADS, GROUP, HEAD_DIM)
    s = jnp.einsum('btkgd,blkd->bkgtl', qh, kk, preferred_element_type=jnp.float32) * scale
    p = sink_probs(s, sink)
    o = jnp.einsum('bkgtl,blkd->btkgd', p.astype(vv.dtype), vv).reshape(B, T, ATTN_WIDTH)
    y = (o * jax.nn.silu(g)) @ w_out
    L = kk.shape[1]
    return y, kk[:, L - wc:], vv[:, L - wc:]


def conv_project(x, w_in):
    h = x @ w_in
    a, b, g = jnp.split(h, 3, axis=-1)
    return a * jax.nn.sigmoid(b), g


def conv_tail(u_padded, g, w_dw, b_dw, ln_g, ln_b, w_out):
    c = lax.conv_general_dilated(u_padded, w_dw[:, None, :].astype(u_padded.dtype),
                                 window_strides=(1,), padding='VALID',
                                 dimension_numbers=('NWC', 'WIO', 'NWC'),
                                 feature_group_count=CONV_CH)
    c = layer_norm(c + b_dw, ln_g, ln_b)
    return (jax.nn.silu(c) * jax.nn.silu(g)) @ w_out


def conv_prompt(x, w_in, w_dw, b_dw, ln_g, ln_b, w_out):
    u, g = conv_project(x, w_in)
    up = jnp.pad(u, ((0, 0), (CONV_WIDTH - 1, 0), (0, 0)))
    y = conv_tail(up, g, w_dw, b_dw, ln_g, ln_b, w_out)
    return y, up[:, up.shape[1] - (CONV_WIDTH - 1):]


def conv_sample(x, state, w_in, w_dw, b_dw, ln_g, ln_b, w_out):
    u, g = conv_project(x, w_in)
    up = jnp.concatenate([state.astype(u.dtype), u], axis=1)
    y = conv_tail(up, g, w_dw, b_dw, ln_g, ln_b, w_out)
    return y, up[:, up.shape[1] - (CONV_WIDTH - 1):]


def setup_inputs(seed: int = 0) -> dict:
    key = jax.random.key(seed)
    ks = jax.random.split(key, 20)
    f32 = jnp.float32
    wc = min(WINDOW, PAST_LEN)
    nrm = lambda k, shape, s: jax.random.normal(k, shape, f32) * s
    return {
        'x_prompt': nrm(ks[0], (BATCH, SEQ, D_MODEL), 1.0),
        'x_sample': nrm(ks[1], (DEC_BATCH, DEC_SEQ, D_MODEL), 1.0),
        'cache_k': nrm(ks[2], (N_ATTN_LAYERS, DEC_BATCH, wc, N_KV_HEADS, HEAD_DIM), 1.0),
        'cache_v': nrm(ks[3], (N_ATTN_LAYERS, DEC_BATCH, wc, N_KV_HEADS, HEAD_DIM), 1.0),
        'state_conv': nrm(ks[4], (N_CONV_LAYERS, DEC_BATCH, CONV_WIDTH - 1, CONV_CH), 0.5),
        'attn_w_in': nrm(ks[5], (N_ATTN_LAYERS, D_MODEL, 2 * ATTN_WIDTH + 2 * KV_WIDTH), D_MODEL ** -0.5),
        'attn_sink': nrm(ks[6], (N_ATTN_LAYERS, N_HEADS), 0.5),
        'attn_w_out': nrm(ks[7], (N_ATTN_LAYERS, ATTN_WIDTH, D_MODEL), DEEPNORM_BETA * ATTN_WIDTH ** -0.5),
        'conv_w_in': nrm(ks[8], (N_CONV_LAYERS, D_MODEL, 3 * CONV_CH), D_MODEL ** -0.5),
        'conv_w_dw': nrm(ks[9], (N_CONV_LAYERS, CONV_WIDTH, CONV_CH), CONV_WIDTH ** -0.5),
        'conv_b_dw': nrm(ks[10], (N_CONV_LAYERS, CONV_CH), 0.02),
        'conv_ln_g': 1.0 + nrm(ks[11], (N_CONV_LAYERS, CONV_CH), 0.02),
        'conv_ln_b': nrm(ks[12], (N_CONV_LAYERS, CONV_CH), 0.02),
        'conv_w_out': nrm(ks[13], (N_CONV_LAYERS, CONV_CH, D_MODEL), DEEPNORM_BETA * CONV_CH ** -0.5),
        'post_ln_g': 1.0 + nrm(ks[14], (DEPTH, D_MODEL), 0.02),
        'post_ln_b': nrm(ks[15], (DEPTH, D_MODEL), 0.02),
    }


def reference(x_prompt, x_sample, cache_k, cache_v, state_conv,
              attn_w_in, attn_sink, attn_w_out,
              conv_w_in, conv_w_dw, conv_b_dw, conv_ln_g, conv_ln_b, conv_w_out,
              post_ln_g, post_ln_b):
    wc = cache_k.shape[2]
    xp, xs = x_prompt, x_sample
    kp_list, vp_list, ks_list, vs_list, cp_list, cs_list = [], [], [], [], [], []
    for i in range(DEPTH):
        j = i // N_MIXERS
        if i % N_MIXERS == 0:
            yp, kp, vp = attn_prompt(xp, attn_w_in[j], attn_sink[j], attn_w_out[j], wc)
            ys, kn, vn = attn_sample(xs, attn_w_in[j], attn_sink[j], attn_w_out[j], cache_k[j], cache_v[j])
            kp_list.append(kp); vp_list.append(vp); ks_list.append(kn); vs_list.append(vn)
        else:
            yp, cp = conv_prompt(xp, conv_w_in[j], conv_w_dw[j], conv_b_dw[j], conv_ln_g[j], conv_ln_b[j], conv_w_out[j])
            ys, cn = conv_sample(xs, state_conv[j], conv_w_in[j], conv_w_dw[j], conv_b_dw[j], conv_ln_g[j], conv_ln_b[j], conv_w_out[j])
            cp_list.append(cp); cs_list.append(cn)
        xp = layer_norm(DEEPNORM_ALPHA * xp + yp, post_ln_g[i], post_ln_b[i])
        xs = layer_norm(DEEPNORM_ALPHA * xs + ys, post_ln_g[i], post_ln_b[i])
    new_k_prompt = jnp.stack(kp_list)
    new_v_prompt = jnp.stack(vp_list)
    new_k_sample = jnp.stack(ks_list)
    new_v_sample = jnp.stack(vs_list)
    new_conv_prompt = jnp.stack(cp_list)
    new_conv_sample = jnp.stack(cs_list)
    return (xp, xs, new_k_prompt, new_v_prompt, new_k_sample, new_v_sample, new_conv_prompt, new_conv_sample)
```

```python
import functools

import jax
import jax.numpy as jnp
from jax import lax
from jax.experimental import pallas as pl
from jax.experimental.pallas import tpu as pltpu

F32 = jnp.float32
BF16 = jnp.bfloat16

D_MODEL = 2048
DEPTH = 2
PAST_LEN = 1024
CHUNK = 64
HEAD_DIM = 64
N_HEADS = D_MODEL // HEAD_DIM
N_KV_HEADS = 8
GROUP = N_HEADS // N_KV_HEADS
ATTN_WIDTH = N_HEADS * HEAD_DIM
KV_WIDTH = N_KV_HEADS * HEAD_DIM
WIN_CHUNKS = 2
WIN_ROWS = WIN_CHUNKS * CHUNK
ROT_DIM = HEAD_DIM // 4
ROPE_THETA = 500000.0
CONV_WIDTH = 31
LN_EPS = 1e-5
DEEPNORM_ALPHA = (2.0 * DEPTH) ** 0.25
ATTN_SCALE = HEAD_DIM ** -0.5

LANES = 128
SUBLANES = 8
VMEM_LIMIT_BYTES = 58 * 1024 * 1024
ROW_BLOCK = 256
CONV_HIST = 32
CONV_ROW_TILE = 32
KEYS = WIN_ROWS + CHUNK


def _layer_norm(r, g, b):
    mu = jnp.mean(r, axis=-1, keepdims=True)
    d = r - mu
    var = jnp.mean(d * d, axis=-1, keepdims=True)
    return d * lax.rsqrt(var + LN_EPS) * g + b


def _silu(t):
    return t * jax.nn.sigmoid(t)


def _rope_tables(pos, inv_ref, sgn_ref):
    ang = pos * inv_ref[...]
    return jnp.cos(ang), jnp.sin(ang) * sgn_ref[...]


def _rope(t, cosv, sinv):
    lane = lax.broadcasted_iota(jnp.int32, (1, LANES), 1) % HEAD_DIM
    partner = jnp.where(lane < ROT_DIM // 2,
                        pltpu.roll(t, LANES - ROT_DIM // 2, 1),
                        pltpu.roll(t, ROT_DIM // 2, 1))
    return t * cosv + partner * sinv


def _lo_hi(col):
    low = lax.broadcasted_iota(jnp.int32, (1, LANES), 1) < HEAD_DIM
    swapped = pltpu.roll(col, HEAD_DIM, 1)
    zero = jnp.zeros_like(col)
    even_lo = jnp.where(low, col, zero).astype(BF16)
    even_hi = jnp.where(low, zero, swapped).astype(BF16)
    odd_lo = jnp.where(low, swapped, zero).astype(BF16)
    odd_hi = jnp.where(low, zero, col).astype(BF16)
    return even_lo, even_hi, odd_lo, odd_hi


def _attend(q2, k_lo, k_hi, v_lo, v_hi, sink_a, sink_b, valid):
    qst = jnp.concatenate([q2[:, :LANES], q2[:, LANES:]], axis=0)
    dims = (((1,), (1,)), ((), ()))
    outs = []
    for k_x, v_x, sk in ((k_lo, v_lo, sink_a), (k_hi, v_hi, sink_b)):
        s = lax.dot_general(qst, k_x, dims, preferred_element_type=F32)
        if valid is not None:
            s = jnp.where(valid, s, -jnp.inf)
        m = jnp.maximum(jnp.max(s, axis=-1, keepdims=True), sk)
        p = jnp.exp(s - m)
        denom = jnp.sum(p, axis=-1, keepdims=True) + jnp.exp(sk - m)
        o = jnp.dot(p.astype(BF16), v_x, preferred_element_type=F32)
        outs.append(o * (1.0 / denom))
    return outs[0] + outs[1]


def _sink_cols(sink_ref, kv, q_rows):
    row = lax.broadcasted_iota(jnp.int32, (2 * q_rows, 1), 0)
    first = row < q_rows
    h = kv * GROUP
    return (jnp.where(first, sink_ref[h + 0], sink_ref[h + 2]),
            jnp.where(first, sink_ref[h + 1], sink_ref[h + 3]))


def _attn_prompt_kernel(x_ref, wq_ref, wk_ref, wv_ref, wg_ref, wo_ref, inv_ref, sgn_ref,
                        sink_ref, lng_ref, lnb_ref,
                        y_ref, knew_ref, vnew_ref,
                        qs, klo, khi, vlo, vhi, os_):
    i = pl.program_id(0)
    last = pl.num_programs(0) - 1
    tm = x_ref.shape[0]

    @pl.when(i == 0)
    def _():
        for ref in (klo, khi, vlo, vhi):
            ref[0:WIN_ROWS, :] = jnp.zeros((WIN_ROWS, ref.shape[1]), BF16)

    x = x_ref[...]
    xb = x.astype(BF16)
    pos = (i * tm + lax.broadcasted_iota(jnp.int32, (tm, 1), 0)).astype(F32)
    cosv, sinv = _rope_tables(pos, inv_ref, sgn_ref)

    q = jnp.dot(xb, wq_ref[...], preferred_element_type=F32)
    for j in range(ATTN_WIDTH // LANES):
        sl = slice(j * LANES, (j + 1) * LANES)
        qs[:, sl] = (_rope(q[:, sl], cosv, sinv) * ATTN_SCALE).astype(BF16)

    k = jnp.dot(xb, wk_ref[...], preferred_element_type=F32)
    v = jnp.dot(xb, wv_ref[...], preferred_element_type=F32)
    new_rows = slice(WIN_ROWS, WIN_ROWS + tm)
    for c in range(KV_WIDTH // LANES):
        sl = slice(c * LANES, (c + 1) * LANES)
        kc = _rope(k[:, sl], cosv, sinv)
        vc = v[:, sl]

        @pl.when(i == last)
        def _():
            knew_ref[:, sl] = kc[tm - WIN_ROWS:, :]
            vnew_ref[:, sl] = vc[tm - WIN_ROWS:, :]

        ev = slice(2 * c * LANES, (2 * c + 1) * LANES)
        od = slice((2 * c + 1) * LANES, (2 * c + 2) * LANES)
        klo[new_rows, ev], khi[new_rows, ev], klo[new_rows, od], khi[new_rows, od] = _lo_hi(kc)
        vlo[new_rows, ev], vhi[new_rows, ev], vlo[new_rows, od], vhi[new_rows, od] = _lo_hi(vc)

    col = lax.broadcasted_iota(jnp.int32, (1, KEYS), 1)

    def chunk_body(c, carry):
        r0 = pl.multiple_of(c * CHUNK, CHUNK)
        first_key_chunk = i * (tm // CHUNK) + c - WIN_CHUNKS
        valid = col >= jnp.maximum(0, -first_key_chunk) * CHUNK
        for kv in range(N_KV_HEADS):
            ksl = slice(kv * LANES, (kv + 1) * LANES)
            q2 = qs[pl.ds(r0, CHUNK), kv * 2 * LANES:(kv + 1) * 2 * LANES]
            sink_a, sink_b = _sink_cols(sink_ref, kv, CHUNK)
            o = _attend(q2, klo[pl.ds(r0, KEYS), ksl], khi[pl.ds(r0, KEYS), ksl],
                        vlo[pl.ds(r0, KEYS), ksl], vhi[pl.ds(r0, KEYS), ksl],
                        sink_a, sink_b, valid)
            os_[pl.ds(r0, CHUNK), kv * 2 * LANES:kv * 2 * LANES + LANES] = o[:CHUNK]
            os_[pl.ds(r0, CHUNK), kv * 2 * LANES + LANES:(kv + 1) * 2 * LANES] = o[CHUNK:]
        return carry

    lax.fori_loop(0, tm // CHUNK, chunk_body, 0)

    for ref in (klo, khi, vlo, vhi):
        ref[0:WIN_ROWS, :] = ref[tm:tm + WIN_ROWS, :]

    g = jnp.dot(xb, wg_ref[...], preferred_element_type=F32)
    z = (os_[...] * _silu(g)).astype(BF16)
    y = jnp.dot(z, wo_ref[...], preferred_element_type=F32)
    y_ref[...] = _layer_norm(DEEPNORM_ALPHA * x + y, lng_ref[...], lnb_ref[...])


def _attn_sample_kernel(x_ref, ck_ref, cv_ref, wq_ref, wk_ref, wv_ref, wg_ref,
                        inv_ref, sgn_ref, sink_ref,
                        z_ref, knew_ref, vnew_ref,
                        qs, ks, vs, os_, *, n_batch, n_new, n_cache):
    rows = n_batch * n_new
    x = x_ref[...]
    xb = x.astype(BF16)
    pos = (PAST_LEN + lax.broadcasted_iota(jnp.int32, (rows, 1), 0) % n_new).astype(F32)
    cosv, sinv = _rope_tables(pos, inv_ref, sgn_ref)

    q = jnp.dot(xb, wq_ref[...], preferred_element_type=F32)
    for j in range(ATTN_WIDTH // LANES):
        sl = slice(j * LANES, (j + 1) * LANES)
        qs[:, sl] = (_rope(q[:, sl], cosv, sinv) * ATTN_SCALE).astype(BF16)
    k = jnp.dot(xb, wk_ref[...], preferred_element_type=F32)
    v = jnp.dot(xb, wv_ref[...], preferred_element_type=F32)
    for c in range(KV_WIDTH // LANES):
        sl = slice(c * LANES, (c + 1) * LANES)
        ks[:, sl] = _rope(k[:, sl], cosv, sinv)
        vs[:, sl] = v[:, sl]

    def batch_body(b, carry):
        r0 = pl.multiple_of(b * n_new, n_new)
        c0 = pl.multiple_of(b * n_cache, n_cache)
        kk = jnp.concatenate([ck_ref[pl.ds(c0, n_cache), :], ks[pl.ds(r0, n_new), :]], axis=0)
        vv = jnp.concatenate([cv_ref[pl.ds(c0, n_cache), :], vs[pl.ds(r0, n_new), :]], axis=0)
        knew_ref[pl.ds(c0, n_cache), :] = kk[n_new:, :]
        vnew_ref[pl.ds(c0, n_cache), :] = vv[n_new:, :]
        for c in range(KV_WIDTH // LANES):
            sl = slice(c * LANES, (c + 1) * LANES)
            k_parts = _lo_hi(kk[:, sl])
            v_parts = _lo_hi(vv[:, sl])
            for par in range(2):
                kv = 2 * c + par
                q2 = qs[pl.ds(r0, n_new), kv * 2 * LANES:(kv + 1) * 2 * LANES]
                sink_a, sink_b = _sink_cols(sink_ref, kv, n_new)
                o = _attend(q2, k_parts[2 * par], k_parts[2 * par + 1],
                            v_parts[2 * par], v_parts[2 * par + 1], sink_a, sink_b, None)
                os_[pl.ds(r0, n_new), kv * 2 * LANES:kv * 2 * LANES + LANES] = o[:n_new]
                os_[pl.ds(r0, n_new), kv * 2 * LANES + LANES:(kv + 1) * 2 * LANES] = o[n_new:]
        return carry

    lax.fori_loop(0, n_batch, batch_body, 0)

    g = jnp.dot(xb, wg_ref[...], preferred_element_type=F32)
    z_ref[...] = (os_[...] * _silu(g)).astype(BF16)


def _out_proj_kernel(z_ref, x_ref, wo_ref, lng_ref, lnb_ref, y_ref):
    y = jnp.dot(z_ref[...], wo_ref[...], preferred_element_type=F32)
    y_ref[...] = _layer_norm(DEEPNORM_ALPHA * x_ref[...] + y, lng_ref[...], lnb_ref[...])


def _depthwise(u_scr, c_scr, wdw_ref, bdw_ref, base, out_base, n_rows):
    off = CONV_HIST - (CONV_WIDTH - 1)
    for l in range(D_MODEL // LANES):
        sl = slice(l * LANES, (l + 1) * LANES)
        acc = jnp.zeros((n_rows, LANES), F32)
        for j in range(CONV_WIDTH):
            acc = acc + wdw_ref[j:j + 1, sl] * u_scr[l, pl.ds(base + j + off, n_rows, stride=1), :]
        c_scr[pl.ds(out_base, n_rows), sl] = acc + bdw_ref[:, sl]


def _conv_gate(c, g, cg_ref, cb_ref):
    cn = _layer_norm(c, cg_ref[...], cb_ref[...])
    return (_silu(cn) * _silu(g)).astype(BF16)


def _conv_prompt_kernel(x_ref, wa_ref, wb_ref, wg_ref, wo_ref, wdw_ref, bdw_ref, cg_ref, cb_ref,
                        lng_ref, lnb_ref,
                        y_ref, state_ref,
                        u_scr, c_scr):
    i = pl.program_id(0)
    last = pl.num_programs(0) - 1
    tm = x_ref.shape[0]

    n_tiles = D_MODEL // LANES

    @pl.when(i == 0)
    def _():
        u_scr[:, 0:CONV_HIST, :] = jnp.zeros((n_tiles, CONV_HIST, LANES), F32)

    x = x_ref[...]
    xb = x.astype(BF16)
    a = jnp.dot(xb, wa_ref[...], preferred_element_type=F32)
    b = jnp.dot(xb, wb_ref[...], preferred_element_type=F32)
    u = a * jax.nn.sigmoid(b)
    for l in range(n_tiles):
        u_scr[l, CONV_HIST:CONV_HIST + tm, :] = u[:, l * LANES:(l + 1) * LANES]

    def row_body(rt, carry):
        base = pl.multiple_of(rt * CONV_ROW_TILE, CONV_ROW_TILE)
        _depthwise(u_scr, c_scr, wdw_ref, bdw_ref, base, base, CONV_ROW_TILE)
        return carry

    lax.fori_loop(0, tm // CONV_ROW_TILE, row_body, 0)

    @pl.when(i == last)
    def _():
        for l in range(n_tiles):
            state_ref[:, l * LANES:(l + 1) * LANES] = (
                u_scr[l, CONV_HIST + tm - (CONV_WIDTH - 1):CONV_HIST + tm, :])

    u_scr[:, 0:CONV_HIST, :] = u_scr[:, tm:tm + CONV_HIST, :]

    g = jnp.dot(xb, wg_ref[...], preferred_element_type=F32)
    z = _conv_gate(c_scr[...], g, cg_ref, cb_ref)
    y = jnp.dot(z, wo_ref[...], preferred_element_type=F32)
    y_ref[...] = _layer_norm(DEEPNORM_ALPHA * x + y, lng_ref[...], lnb_ref[...])


def _conv_sample_kernel(x_ref, st_ref, wa_ref, wb_ref, wg_ref, wdw_ref, bdw_ref, cg_ref, cb_ref,
                        z_ref, state_ref,
                        u_scr, c_scr, *, n_batch, n_new):
    slot = CONV_HIST + n_new
    hist = CONV_WIDTH - 1
    x = x_ref[...]
    xb = x.astype(BF16)
    a = jnp.dot(xb, wa_ref[...], preferred_element_type=F32)
    b = jnp.dot(xb, wb_ref[...], preferred_element_type=F32)
    u = a * jax.nn.sigmoid(b)
    pad = CONV_HIST - hist
    for bi in range(n_batch):
        base = bi * slot
        for l in range(D_MODEL // LANES):
            sl = slice(l * LANES, (l + 1) * LANES)
            u_scr[l, base:base + pad, :] = jnp.zeros((pad, LANES), F32)
            u_scr[l, base + pad:base + CONV_HIST, :] = st_ref[bi, :, sl]
            u_scr[l, base + CONV_HIST:base + slot, :] = u[bi * n_new:(bi + 1) * n_new, sl]
            state_ref[bi, :, sl] = u_scr[l, base + slot - hist:base + slot, :]

    def batch_body(bi, carry):
        _depthwise(u_scr, c_scr, wdw_ref, bdw_ref, pl.multiple_of(bi * slot, SUBLANES),
                   pl.multiple_of(bi * n_new, n_new), n_new)
        return carry

    lax.fori_loop(0, n_batch, batch_body, 0)

    g = jnp.dot(xb, wg_ref[...], preferred_element_type=F32)
    z_ref[...] = _conv_gate(c_scr[...], g, cg_ref, cb_ref)


def _const_spec(shape):
    return pl.BlockSpec(shape, lambda i: (0,) * len(shape), pipeline_mode=pl.Buffered(1))


def _vmem_spec():
    return pl.BlockSpec(memory_space=pltpu.VMEM)


def _smem_spec():
    return pl.BlockSpec(memory_space=pltpu.SMEM)


def _out_proj(z, x, wo, lng, lnb):
    return pl.pallas_call(
        _out_proj_kernel,
        in_specs=[_vmem_spec()] * 5,
        out_specs=_vmem_spec(),
        out_shape=jax.ShapeDtypeStruct(x.shape, F32),
        compiler_params=pltpu.CompilerParams(vmem_limit_bytes=VMEM_LIMIT_BYTES),
        name="out_proj_sample",
    )(z, x, wo, lng, lnb)


def _rope_lane_consts():
    half = ROT_DIM // 2
    inv = ROPE_THETA ** (-jnp.arange(half, dtype=F32) * (2.0 / ROT_DIM))
    zeros = jnp.zeros((HEAD_DIM - ROT_DIM,), F32)
    inv_head = jnp.concatenate([inv, inv, zeros])
    sgn_head = jnp.concatenate([-jnp.ones((half,), F32), jnp.ones((half,), F32), zeros])
    reps = LANES // HEAD_DIM
    return jnp.tile(inv_head, reps)[None, :], jnp.tile(sgn_head, reps)[None, :]


def _attn_layer(xp, xs, ck, cv, w_in, sink, w_out, ln_g, ln_b):
    seq = xp.shape[0]
    n_batch, n_cache = ck.shape[0], ck.shape[1]
    n_new = xs.shape[0] // n_batch
    wq = w_in[:, :ATTN_WIDTH].astype(BF16)
    wk = w_in[:, ATTN_WIDTH:ATTN_WIDTH + KV_WIDTH].astype(BF16)
    wv = w_in[:, ATTN_WIDTH + KV_WIDTH:ATTN_WIDTH + 2 * KV_WIDTH].astype(BF16)
    wg = w_in[:, ATTN_WIDTH + 2 * KV_WIDTH:].astype(BF16)
    wo = w_out.astype(BF16)
    inv, sgn = _rope_lane_consts()
    lng, lnb = ln_g[None, :], ln_b[None, :]
    params = pltpu.CompilerParams(dimension_semantics=("arbitrary",), vmem_limit_bytes=VMEM_LIMIT_BYTES)

    tm = ROW_BLOCK
    kv_scr = pltpu.VMEM((WIN_ROWS + tm, N_KV_HEADS * LANES), BF16)
    yp, kp, vp = pl.pallas_call(
        _attn_prompt_kernel,
        grid=(seq // tm,),
        in_specs=[pl.BlockSpec((tm, D_MODEL), lambda i: (i, 0)),
                  _const_spec(wq.shape), _const_spec(wk.shape), _const_spec(wv.shape),
                  _const_spec(wg.shape), _const_spec(wo.shape),
                  _const_spec(inv.shape), _const_spec(sgn.shape), _smem_spec(),
                  _const_spec(lng.shape), _const_spec(lnb.shape)],
        out_specs=[pl.BlockSpec((tm, D_MODEL), lambda i: (i, 0)),
                   pl.BlockSpec((WIN_ROWS, KV_WIDTH), lambda i: (0, 0)),
                   pl.BlockSpec((WIN_ROWS, KV_WIDTH), lambda i: (0, 0))],
        out_shape=[jax.ShapeDtypeStruct((seq, D_MODEL), F32),
                   jax.ShapeDtypeStruct((WIN_ROWS, KV_WIDTH), F32),
                   jax.ShapeDtypeStruct((WIN_ROWS, KV_WIDTH), F32)],
        scratch_shapes=[pltpu.VMEM((tm, ATTN_WIDTH), BF16), kv_scr, kv_scr, kv_scr, kv_scr,
                        pltpu.VMEM((tm, ATTN_WIDTH), F32)],
        compiler_params=params,
        name="attn_prompt",
    )(xp, wq, wk, wv, wg, wo, inv, sgn, sink, lng, lnb)

    rows = xs.shape[0]
    ck2 = ck.reshape(n_batch * n_cache, KV_WIDTH)
    cv2 = cv.reshape(n_batch * n_cache, KV_WIDTH)
    zs, ksn, vsn = pl.pallas_call(
        functools.partial(_attn_sample_kernel, n_batch=n_batch, n_new=n_new, n_cache=n_cache),
        in_specs=[_vmem_spec()] * 9 + [_smem_spec()],
        out_specs=[_vmem_spec()] * 3,
        out_shape=[jax.ShapeDtypeStruct((rows, ATTN_WIDTH), BF16),
                   jax.ShapeDtypeStruct(ck2.shape, F32),
                   jax.ShapeDtypeStruct(cv2.shape, F32)],
        scratch_shapes=[pltpu.VMEM((rows, ATTN_WIDTH), BF16),
                        pltpu.VMEM((rows, KV_WIDTH), F32), pltpu.VMEM((rows, KV_WIDTH), F32),
                        pltpu.VMEM((rows, ATTN_WIDTH), F32)],
        compiler_params=pltpu.CompilerParams(vmem_limit_bytes=VMEM_LIMIT_BYTES),
        name="attn_sample",
    )(xs, ck2, cv2, wq, wk, wv, wg, inv, sgn, sink)
    ys = _out_proj(zs, xs, wo, lng, lnb)
    return yp, ys, kp, vp, ksn, vsn


def _conv_layer(xp, xs, state, w_in, w_dw, b_dw, cg, cb, w_out, ln_g, ln_b):
    seq = xp.shape[0]
    n_batch = state.shape[0]
    n_new = xs.shape[0] // n_batch
    hist = CONV_WIDTH - 1
    wa = w_in[:, :D_MODEL].astype(BF16)
    wb = w_in[:, D_MODEL:2 * D_MODEL].astype(BF16)
    wg = w_in[:, 2 * D_MODEL:].astype(BF16)
    wo = w_out.astype(BF16)
    vecs = [b_dw[None, :], cg[None, :], cb[None, :], ln_g[None, :], ln_b[None, :]]

    tm = ROW_BLOCK
    yp, cp = pl.pallas_call(
        _conv_prompt_kernel,
        grid=(seq // tm,),
        in_specs=[pl.BlockSpec((tm, D_MODEL), lambda i: (i, 0)),
                  _const_spec(wa.shape), _const_spec(wb.shape), _const_spec(wg.shape),
                  _const_spec(wo.shape), _const_spec(w_dw.shape)]
                 + [_const_spec(t.shape) for t in vecs],
        out_specs=[pl.BlockSpec((tm, D_MODEL), lambda i: (i, 0)),
                   pl.BlockSpec((hist, D_MODEL), lambda i: (0, 0))],
        out_shape=[jax.ShapeDtypeStruct((seq, D_MODEL), F32),
                   jax.ShapeDtypeStruct((hist, D_MODEL), F32)],
        scratch_shapes=[pltpu.VMEM((D_MODEL // LANES, CONV_HIST + tm, LANES), F32),
                        pltpu.VMEM((tm, D_MODEL), F32)],
        compiler_params=pltpu.CompilerParams(dimension_semantics=("arbitrary",),
                                             vmem_limit_bytes=VMEM_LIMIT_BYTES),
        name="conv_prompt",
    )(xp, wa, wb, wg, wo, w_dw, *vecs)

    rows = xs.shape[0]
    zs, cs = pl.pallas_call(
        functools.partial(_conv_sample_kernel, n_batch=n_batch, n_new=n_new),
        in_specs=[_vmem_spec()] * 9,
        out_specs=[_vmem_spec()] * 2,
        out_shape=[jax.ShapeDtypeStruct((rows, D_MODEL), BF16),
                   jax.ShapeDtypeStruct((n_batch, hist, D_MODEL), F32)],
        scratch_shapes=[pltpu.VMEM((D_MODEL // LANES, n_batch * (CONV_HIST + n_new), LANES), F32),
                        pltpu.VMEM((rows, D_MODEL), F32)],
        compiler_params=pltpu.CompilerParams(vmem_limit_bytes=VMEM_LIMIT_BYTES),
        name="conv_sample",
    )(xs, state, wa, wb, wg, w_dw, *vecs[:3])
    ys = _out_proj(zs, xs, wo, vecs[3], vecs[4])
    return yp, ys, cp, cs


def kernel(x_prompt, x_sample, cache_k, cache_v, state_conv, attn_w_in, attn_sink, attn_w_out,
           conv_w_in, conv_w_dw, conv_b_dw, conv_ln_g, conv_ln_b, conv_w_out, post_ln_g, post_ln_b):
    batch, seq, _ = x_prompt.shape
    dec_batch, dec_seq, _ = x_sample.shape
    wc = cache_k.shape[2]
    assert batch == 1 and wc == WIN_ROWS and seq % ROW_BLOCK == 0
    xp = x_prompt.reshape(seq, D_MODEL)
    xs = x_sample.reshape(dec_batch * dec_seq, D_MODEL)

    xp, xs, kp, vp, ksn, vsn = _attn_layer(
        xp, xs, cache_k[0].reshape(dec_batch, wc, KV_WIDTH), cache_v[0].reshape(dec_batch, wc, KV_WIDTH),
        attn_w_in[0], attn_sink[0], attn_w_out[0], post_ln_g[0], post_ln_b[0])
    xp, xs, cp, cs = _conv_layer(
        xp, xs, state_conv[0], conv_w_in[0], conv_w_dw[0], conv_b_dw[0], conv_ln_g[0], conv_ln_b[0],
        conv_w_out[0], post_ln_g[1], post_ln_b[1])

    kv_shape = (1, 1, wc, N_KV_HEADS, HEAD_DIM)
    kvs_shape = (1, dec_batch, wc, N_KV_HEADS, HEAD_DIM)
    return (xp.reshape(batch, seq, D_MODEL), xs.reshape(dec_batch, dec_seq, D_MODEL),
            kp.reshape(kv_shape), vp.reshape(kv_shape),
            ksn.reshape(kvs_shape), vsn.reshape(kvs_shape),
            cp[None, None], cs[None])
```

```python
import functools

import jax
import jax.numpy as jnp
from jax import lax
from jax.experimental import pallas as pl
from jax.experimental.pallas import tpu as pltpu

F32 = jnp.float32
BF16 = jnp.bfloat16

D_MODEL = 2048
DEPTH = 2
PAST_LEN = 1024
CHUNK = 64
HEAD_DIM = 64
N_HEADS = D_MODEL // HEAD_DIM
N_KV_HEADS = 8
GROUP = N_HEADS // N_KV_HEADS
ATTN_WIDTH = N_HEADS * HEAD_DIM
KV_WIDTH = N_KV_HEADS * HEAD_DIM
WIN_CHUNKS = 2
WIN_ROWS = WIN_CHUNKS * CHUNK
ROT_DIM = HEAD_DIM // 4
ROPE_THETA = 500000.0
CONV_WIDTH = 31
LN_EPS = 1e-5
DEEPNORM_ALPHA = (2.0 * DEPTH) ** 0.25
ATTN_SCALE = HEAD_DIM ** -0.5

LANES = 128
SUBLANES = 8
VMEM_LIMIT_BYTES = 58 * 1024 * 1024
ROW_BLOCK = 256
CONV_HIST = 32
CONV_ROW_TILE = 32
KEYS = WIN_ROWS + CHUNK


def _layer_norm(r, g, b):
    mu = jnp.mean(r, axis=-1, keepdims=True)
    d = r - mu
    var = jnp.mean(d * d, axis=-1, keepdims=True)
    return d * lax.rsqrt(var + LN_EPS) * g + b


def _silu(t):
    return t * jax.nn.sigmoid(t)


def _rope_tables(pos, inv_ref, sgn_ref):
    ang = pos * inv_ref[...]
    return jnp.cos(ang), jnp.sin(ang) * sgn_ref[...]


def _rope(t, cosv, sinv):
    lane = lax.broadcasted_iota(jnp.int32, (1, LANES), 1) % HEAD_DIM
    partner = jnp.where(lane < ROT_DIM // 2,
                        pltpu.roll(t, LANES - ROT_DIM // 2, 1),
                        pltpu.roll(t, ROT_DIM // 2, 1))
    return t * cosv + partner * sinv


def _lo_hi(col):
    low = lax.broadcasted_iota(jnp.int32, (1, LANES), 1) < HEAD_DIM
    swapped = pltpu.roll(col, HEAD_DIM, 1)
    zero = jnp.zeros_like(col)
    even_lo = jnp.where(low, col, zero).astype(BF16)
    even_hi = jnp.where(low, zero, swapped).astype(BF16)
    odd_lo = jnp.where(low, swapped, zero).astype(BF16)
    odd_hi = jnp.where(low, zero, col).astype(BF16)
    return even_lo, even_hi, odd_lo, odd_hi


def _scores(q2, k_lo, k_hi):
    qst = jnp.concatenate([q2[:, :LANES], q2[:, LANES:]], axis=0)
    dims = (((1,), (1,)), ((), ()))
    return (lax.dot_general(qst, k_lo, dims, preferred_element_type=F32),
            lax.dot_general(qst, k_hi, dims, preferred_element_type=F32))


def _sink_softmax(s, sk, valid):
    if valid is not None:
        s = jnp.where(valid, s, -jnp.inf)
    m = jnp.maximum(jnp.max(s, axis=-1, keepdims=True), sk)
    p = jnp.exp(s - m)
    denom = jnp.sum(p, axis=-1, keepdims=True) + jnp.exp(sk - m)
    return p.astype(BF16), 1.0 / denom


def _weighted_values(p_a, inv_a, p_b, inv_b, v_lo, v_hi):
    o_a = jnp.dot(p_a, v_lo, preferred_element_type=F32)
    o_b = jnp.dot(p_b, v_hi, preferred_element_type=F32)
    return o_a * inv_a + o_b * inv_b


def _attend_heads(q_of, k_of, v_of, sink_ref, q_rows, valid):
    scores = [_scores(q_of(kv), *k_of(kv)) for kv in range(N_KV_HEADS)]
    probs = []
    for kv, (s_a, s_b) in enumerate(scores):
        sink_a, sink_b = _sink_cols(sink_ref, kv, q_rows)
        probs.append(_sink_softmax(s_a, sink_a, valid) + _sink_softmax(s_b, sink_b, valid))
    return [_weighted_values(*probs[kv], *v_of(kv)) for kv in range(N_KV_HEADS)]


def _sink_cols(sink_ref, kv, q_rows):
    row = lax.broadcasted_iota(jnp.int32, (2 * q_rows, 1), 0)
    first = row < q_rows
    h = kv * GROUP
    return (jnp.where(first, sink_ref[h + 0], sink_ref[h + 2]),
            jnp.where(first, sink_ref[h + 1], sink_ref[h + 3]))


def _attn_prompt_kernel(x_ref, wq_ref, wk_ref, wv_ref, wg_ref, wo_ref, inv_ref, sgn_ref,
                        sink_ref, lng_ref, lnb_ref,
                        y_ref, knew_ref, vnew_ref,
                        qs, klo, khi, vlo, vhi, os_):
    i = pl.program_id(0)
    last = pl.num_programs(0) - 1
    tm = x_ref.shape[0]

    @pl.when(i == 0)
    def _():
        for ref in (klo, khi, vlo, vhi):
            ref[0:WIN_ROWS, :] = jnp.zeros((WIN_ROWS, ref.shape[1]), BF16)

    x = x_ref[...]
    xb = x.astype(BF16)
    pos = (i * tm + lax.broadcasted_iota(jnp.int32, (tm, 1), 0)).astype(F32)
    cosv, sinv = _rope_tables(pos, inv_ref, sgn_ref)

    q = jnp.dot(xb, wq_ref[...], preferred_element_type=F32)
    for j in range(ATTN_WIDTH // LANES):
        sl = slice(j * LANES, (j + 1) * LANES)
        qs[:, sl] = (_rope(q[:, sl], cosv, sinv) * ATTN_SCALE).astype(BF16)

    k = jnp.dot(xb, wk_ref[...], preferred_element_type=F32)
    v = jnp.dot(xb, wv_ref[...], preferred_element_type=F32)
    new_rows = slice(WIN_ROWS, WIN_ROWS + tm)
    for c in range(KV_WIDTH // LANES):
        sl = slice(c * LANES, (c + 1) * LANES)
        kc = _rope(k[:, sl], cosv, sinv)
        vc = v[:, sl]

        @pl.when(i == last)
        def _():
            knew_ref[:, sl] = kc[tm - WIN_ROWS:, :]
            vnew_ref[:, sl] = vc[tm - WIN_ROWS:, :]

        ev = slice(2 * c * LANES, (2 * c + 1) * LANES)
        od = slice((2 * c + 1) * LANES, (2 * c + 2) * LANES)
        klo[new_rows, ev], khi[new_rows, ev], klo[new_rows, od], khi[new_rows, od] = _lo_hi(kc)
        vlo[new_rows, ev], vhi[new_rows, ev], vlo[new_rows, od], vhi[new_rows, od] = _lo_hi(vc)

    col = lax.broadcasted_iota(jnp.int32, (1, KEYS), 1)

    def chunk_body(c, carry):
        r0 = pl.multiple_of(c * CHUNK, CHUNK)
        first_key_chunk = i * (tm // CHUNK) + c - WIN_CHUNKS
        valid = col >= jnp.maximum(0, -first_key_chunk) * CHUNK
        keys = pl.ds(r0, KEYS)

        def head_lanes(kv):
            return slice(kv * LANES, (kv + 1) * LANES)

        outs = _attend_heads(
            lambda kv: qs[pl.ds(r0, CHUNK), kv * 2 * LANES:(kv + 1) * 2 * LANES],
            lambda kv: (klo[keys, head_lanes(kv)], khi[keys, head_lanes(kv)]),
            lambda kv: (vlo[keys, head_lanes(kv)], vhi[keys, head_lanes(kv)]),
            sink_ref, CHUNK, valid)
        for kv, o in enumerate(outs):
            os_[pl.ds(r0, CHUNK), kv * 2 * LANES:kv * 2 * LANES + LANES] = o[:CHUNK]
            os_[pl.ds(r0, CHUNK), kv * 2 * LANES + LANES:(kv + 1) * 2 * LANES] = o[CHUNK:]
        return carry

    lax.fori_loop(0, tm // CHUNK, chunk_body, 0)

    for ref in (klo, khi, vlo, vhi):
        ref[0:WIN_ROWS, :] = ref[tm:tm + WIN_ROWS, :]

    g = jnp.dot(xb, wg_ref[...], preferred_element_type=F32)
    z = (os_[...] * _silu(g)).astype(BF16)
    y = jnp.dot(z, wo_ref[...], preferred_element_type=F32)
    y_ref[...] = _layer_norm(DEEPNORM_ALPHA * x + y, lng_ref[...], lnb_ref[...])


def _attn_sample_kernel(x_ref, ck_ref, cv_ref, wq_ref, wk_ref, wv_ref, wg_ref,
                        inv_ref, sgn_ref, sink_ref,
                        z_ref, knew_ref, vnew_ref,
                        qs, ks, vs, os_, *, n_batch, n_new, n_cache):
    rows = n_batch * n_new
    x = x_ref[...]
    xb = x.astype(BF16)
    pos = (PAST_LEN + lax.broadcasted_iota(jnp.int32, (rows, 1), 0) % n_new).astype(F32)
    cosv, sinv = _rope_tables(pos, inv_ref, sgn_ref)

    q = jnp.dot(xb, wq_ref[...], preferred_element_type=F32)
    for j in range(ATTN_WIDTH // LANES):
        sl = slice(j * LANES, (j + 1) * LANES)
        qs[:, sl] = (_rope(q[:, sl], cosv, sinv) * ATTN_SCALE).astype(BF16)
    k = jnp.dot(xb, wk_ref[...], preferred_element_type=F32)
    v = jnp.dot(xb, wv_ref[...], preferred_element_type=F32)
    for c in range(KV_WIDTH // LANES):
        sl = slice(c * LANES, (c + 1) * LANES)
        ks[:, sl] = _rope(k[:, sl], cosv, sinv)
        vs[:, sl] = v[:, sl]

    def batch_body(b, carry):
        r0 = pl.multiple_of(b * n_new, n_new)
        c0 = pl.multiple_of(b * n_cache, n_cache)
        kk = jnp.concatenate([ck_ref[pl.ds(c0, n_cache), :], ks[pl.ds(r0, n_new), :]], axis=0)
        vv = jnp.concatenate([cv_ref[pl.ds(c0, n_cache), :], vs[pl.ds(r0, n_new), :]], axis=0)
        knew_ref[pl.ds(c0, n_cache), :] = kk[n_new:, :]
        vnew_ref[pl.ds(c0, n_cache), :] = vv[n_new:, :]
        k_parts, v_parts = [], []
        for c in range(KV_WIDTH // LANES):
            sl = slice(c * LANES, (c + 1) * LANES)
            k_parts.extend(_lo_hi(kk[:, sl]))
            v_parts.extend(_lo_hi(vv[:, sl]))
        outs = _attend_heads(
            lambda kv: qs[pl.ds(r0, n_new), kv * 2 * LANES:(kv + 1) * 2 * LANES],
            lambda kv: (k_parts[2 * kv], k_parts[2 * kv + 1]),
            lambda kv: (v_parts[2 * kv], v_parts[2 * kv + 1]),
            sink_ref, n_new, None)
        for kv, o in enumerate(outs):
            os_[pl.ds(r0, n_new), kv * 2 * LANES:kv * 2 * LANES + LANES] = o[:n_new]
            os_[pl.ds(r0, n_new), kv * 2 * LANES + LANES:(kv + 1) * 2 * LANES] = o[n_new:]
        return carry

    lax.fori_loop(0, n_batch, batch_body, 0)

    g = jnp.dot(xb, wg_ref[...], preferred_element_type=F32)
    z_ref[...] = (os_[...] * _silu(g)).astype(BF16)


def _out_proj_kernel(z_ref, x_ref, wo_ref, lng_ref, lnb_ref, y_ref):
    y = jnp.dot(z_ref[...], wo_ref[...], preferred_element_type=F32)
    y_ref[...] = _layer_norm(DEEPNORM_ALPHA * x_ref[...] + y, lng_ref[...], lnb_ref[...])


def _depthwise(u_scr, c_scr, wdw_ref, bdw_ref, base, out_base, n_rows):
    off = CONV_HIST - (CONV_WIDTH - 1)
    for l in range(D_MODEL // LANES):
        sl = slice(l * LANES, (l + 1) * LANES)
        acc = jnp.zeros((n_rows, LANES), F32)
        for j in range(CONV_WIDTH):
            acc = acc + wdw_ref[j:j + 1, sl] * u_scr[l, pl.ds(base + j + off, n_rows, stride=1), :]
        c_scr[pl.ds(out_base, n_rows), sl] = acc + bdw_ref[:, sl]


def _conv_gate(c, g, cg_ref, cb_ref):
    cn = _layer_norm(c, cg_ref[...], cb_ref[...])
    return (_silu(cn) * _silu(g)).astype(BF16)


def _conv_prompt_kernel(x_ref, wa_ref, wb_ref, wg_ref, wo_ref, wdw_ref, bdw_ref, cg_ref, cb_ref,
                        lng_ref, lnb_ref,
                        y_ref, state_ref,
                        u_scr, c_scr):
    i = pl.program_id(0)
    last = pl.num_programs(0) - 1
    tm = x_ref.shape[0]

    n_tiles = D_MODEL // LANES

    @pl.when(i == 0)
    def _():
        u_scr[:, 0:CONV_HIST, :] = jnp.zeros((n_tiles, CONV_HIST, LANES), F32)

    x = x_ref[...]
    xb = x.astype(BF16)
    a = jnp.dot(xb, wa_ref[...], preferred_element_type=F32)
    b = jnp.dot(xb, wb_ref[...], preferred_element_type=F32)
    u = a * jax.nn.sigmoid(b)
    for l in range(n_tiles):
        u_scr[l, CONV_HIST:CONV_HIST + tm, :] = u[:, l * LANES:(l + 1) * LANES]

    def row_body(rt, carry):
        base = pl.multiple_of(rt * CONV_ROW_TILE, CONV_ROW_TILE)
        _depthwise(u_scr, c_scr, wdw_ref, bdw_ref, base, base, CONV_ROW_TILE)
        return carry

    lax.fori_loop(0, tm // CONV_ROW_TILE, row_body, 0)

    @pl.when(i == last)
    def _():
        for l in range(n_tiles):
            state_ref[:, l * LANES:(l + 1) * LANES] = (
                u_scr[l, CONV_HIST + tm - (CONV_WIDTH - 1):CONV_HIST + tm, :])

    u_scr[:, 0:CONV_HIST, :] = u_scr[:, tm:tm + CONV_HIST, :]

    g = jnp.dot(xb, wg_ref[...], preferred_element_type=F32)
    z = _conv_gate(c_scr[...], g, cg_ref, cb_ref)
    y = jnp.dot(z, wo_ref[...], preferred_element_type=F32)
    y_ref[...] = _layer_norm(DEEPNORM_ALPHA * x + y, lng_ref[...], lnb_ref[...])


def _conv_sample_kernel(x_ref, st_ref, wa_ref, wb_ref, wg_ref, wdw_ref, bdw_ref, cg_ref, cb_ref,
                        z_ref, state_ref,
                        u_scr, c_scr, *, n_batch, n_new):
    slot = CONV_HIST + n_new
    hist = CONV_WIDTH - 1
    x = x_ref[...]
    xb = x.astype(BF16)
    a = jnp.dot(xb, wa_ref[...], preferred_element_type=F32)
    b = jnp.dot(xb, wb_ref[...], preferred_element_type=F32)
    u = a * jax.nn.sigmoid(b)
    pad = CONV_HIST - hist
    for bi in range(n_batch):
        base = bi * slot
        for l in range(D_MODEL // LANES):
            sl = slice(l * LANES, (l + 1) * LANES)
            u_scr[l, base:base + pad, :] = jnp.zeros((pad, LANES), F32)
            u_scr[l, base + pad:base + CONV_HIST, :] = st_ref[bi, :, sl]
            u_scr[l, base + CONV_HIST:base + slot, :] = u[bi * n_new:(bi + 1) * n_new, sl]
            state_ref[bi, :, sl] = u_scr[l, base + slot - hist:base + slot, :]

    def batch_body(bi, carry):
        _depthwise(u_scr, c_scr, wdw_ref, bdw_ref, pl.multiple_of(bi * slot, SUBLANES),
                   pl.multiple_of(bi * n_new, n_new), n_new)
        return carry

    lax.fori_loop(0, n_batch, batch_body, 0)

    g = jnp.dot(xb, wg_ref[...], preferred_element_type=F32)
    z_ref[...] = _conv_gate(c_scr[...], g, cg_ref, cb_ref)


def _const_spec(shape):
    return pl.BlockSpec(shape, lambda i: (0,) * len(shape), pipeline_mode=pl.Buffered(1))


def _vmem_spec():
    return pl.BlockSpec(memory_space=pltpu.VMEM)


def _smem_spec():
    return pl.BlockSpec(memory_space=pltpu.SMEM)


def _out_proj(z, x, wo, lng, lnb):
    return pl.pallas_call(
        _out_proj_kernel,
        in_specs=[_vmem_spec()] * 5,
        out_specs=_vmem_spec(),
        out_shape=jax.ShapeDtypeStruct(x.shape, F32),
        compiler_params=pltpu.CompilerParams(vmem_limit_bytes=VMEM_LIMIT_BYTES),
        name="out_proj_sample",
    )(z, x, wo, lng, lnb)


def _rope_lane_consts():
    half = ROT_DIM // 2
    inv = ROPE_THETA ** (-jnp.arange(half, dtype=F32) * (2.0 / ROT_DIM))
    zeros = jnp.zeros((HEAD_DIM - ROT_DIM,), F32)
    inv_head = jnp.concatenate([inv, inv, zeros])
    sgn_head = jnp.concatenate([-jnp.ones((half,), F32), jnp.ones((half,), F32), zeros])
    reps = LANES // HEAD_DIM
    return jnp.tile(inv_head, reps)[None, :], jnp.tile(sgn_head, reps)[None, :]


def _attn_layer(xp, xs, ck, cv, w_in, sink, w_out, ln_g, ln_b):
    seq = xp.shape[0]
    n_batch, n_cache = ck.shape[0], ck.shape[1]
    n_new = xs.shape[0] // n_batch
    wq = w_in[:, :ATTN_WIDTH].astype(BF16)
    wk = w_in[:, ATTN_WIDTH:ATTN_WIDTH + KV_WIDTH].astype(BF16)
    wv = w_in[:, ATTN_WIDTH + KV_WIDTH:ATTN_WIDTH + 2 * KV_WIDTH].astype(BF16)
    wg = w_in[:, ATTN_WIDTH + 2 * KV_WIDTH:].astype(BF16)
    wo = w_out.astype(BF16)
    inv, sgn = _rope_lane_consts()
    lng, lnb = ln_g[None, :], ln_b[None, :]
    params = pltpu.CompilerParams(dimension_semantics=("arbitrary",), vmem_limit_bytes=VMEM_LIMIT_BYTES)

    tm = ROW_BLOCK
    kv_scr = pltpu.VMEM((WIN_ROWS + tm, N_KV_HEADS * LANES), BF16)
    yp, kp, vp = pl.pallas_call(
        _attn_prompt_kernel,
        grid=(seq // tm,),
        in_specs=[pl.BlockSpec((tm, D_MODEL), lambda i: (i, 0)),
                  _const_spec(wq.shape), _const_spec(wk.shape), _const_spec(wv.shape),
                  _const_spec(wg.shape), _const_spec(wo.shape),
                  _const_spec(inv.shape), _const_spec(sgn.shape), _smem_spec(),
                  _const_spec(lng.shape), _const_spec(lnb.shape)],
        out_specs=[pl.BlockSpec((tm, D_MODEL), lambda i: (i, 0)),
                   pl.BlockSpec((WIN_ROWS, KV_WIDTH), lambda i: (0, 0)),
                   pl.BlockSpec((WIN_ROWS, KV_WIDTH), lambda i: (0, 0))],
        out_shape=[jax.ShapeDtypeStruct((seq, D_MODEL), F32),
                   jax.ShapeDtypeStruct((WIN_ROWS, KV_WIDTH), F32),
                   jax.ShapeDtypeStruct((WIN_ROWS, KV_WIDTH), F32)],
        scratch_shapes=[pltpu.VMEM((tm, ATTN_WIDTH), BF16), kv_scr, kv_scr, kv_scr, kv_scr,
                        pltpu.VMEM((tm, ATTN_WIDTH), F32)],
        compiler_params=params,
        name="attn_prompt",
    )(xp, wq, wk, wv, wg, wo, inv, sgn, sink, lng, lnb)

    rows = xs.shape[0]
    ck2 = ck.reshape(n_batch * n_cache, KV_WIDTH)
    cv2 = cv.reshape(n_batch * n_cache, KV_WIDTH)
    zs, ksn, vsn = pl.pallas_call(
        functools.partial(_attn_sample_kernel, n_batch=n_batch, n_new=n_new, n_cache=n_cache),
        in_specs=[_vmem_spec()] * 9 + [_smem_spec()],
        out_specs=[_vmem_spec()] * 3,
        out_shape=[jax.ShapeDtypeStruct((rows, ATTN_WIDTH), BF16),
                   jax.ShapeDtypeStruct(ck2.shape, F32),
                   jax.ShapeDtypeStruct(cv2.shape, F32)],
        scratch_shapes=[pltpu.VMEM((rows, ATTN_WIDTH), BF16),
                        pltpu.VMEM((rows, KV_WIDTH), F32), pltpu.VMEM((rows, KV_WIDTH), F32),
                        pltpu.VMEM((rows, ATTN_WIDTH), F32)],
        compiler_params=pltpu.CompilerParams(vmem_limit_bytes=VMEM_LIMIT_BYTES),
        name="attn_sample",
    )(xs, ck2, cv2, wq, wk, wv, wg, inv, sgn, sink)
    ys = _out_proj(zs, xs, wo, lng, lnb)
    return yp, ys, kp, vp, ksn, vsn


def _conv_layer(xp, xs, state, w_in, w_dw, b_dw, cg, cb, w_out, ln_g, ln_b):
    seq = xp.shape[0]
    n_batch = state.shape[0]
    n_new = xs.shape[0] // n_batch
    hist = CONV_WIDTH - 1
    wa = w_in[:, :D_MODEL].astype(BF16)
    wb = w_in[:, D_MODEL:2 * D_MODEL].astype(BF16)
    wg = w_in[:, 2 * D_MODEL:].astype(BF16)
    wo = w_out.astype(BF16)
    vecs = [b_dw[None, :], cg[None, :], cb[None, :], ln_g[None, :], ln_b[None, :]]

    tm = ROW_BLOCK
    yp, cp = pl.pallas_call(
        _conv_prompt_kernel,
        grid=(seq // tm,),
        in_specs=[pl.BlockSpec((tm, D_MODEL), lambda i: (i, 0)),
                  _const_spec(wa.shape), _const_spec(wb.shape), _const_spec(wg.shape),
                  _const_spec(wo.shape), _const_spec(w_dw.shape)]
                 + [_const_spec(t.shape) for t in vecs],
        out_specs=[pl.BlockSpec((tm, D_MODEL), lambda i: (i, 0)),
                   pl.BlockSpec((hist, D_MODEL), lambda i: (0, 0))],
        out_shape=[jax.ShapeDtypeStruct((seq, D_MODEL), F32),
                   jax.ShapeDtypeStruct((hist, D_MODEL), F32)],
        scratch_shapes=[pltpu.VMEM((D_MODEL // LANES, CONV_HIST + tm, LANES), F32),
                        pltpu.VMEM((tm, D_MODEL), F32)],
        compiler_params=pltpu.CompilerParams(dimension_semantics=("arbitrary",),
                                             vmem_limit_bytes=VMEM_LIMIT_BYTES),
        name="conv_prompt",
    )(xp, wa, wb, wg, wo, w_dw, *vecs)

    rows = xs.shape[0]
    zs, cs = pl.pallas_call(
        functools.partial(_conv_sample_kernel, n_batch=n_batch, n_new=n_new),
        in_specs=[_vmem_spec()] * 9,
        out_specs=[_vmem_spec()] * 2,
        out_shape=[jax.ShapeDtypeStruct((rows, D_MODEL), BF16),
                   jax.ShapeDtypeStruct((n_batch, hist, D_MODEL), F32)],
        scratch_shapes=[pltpu.VMEM((D_MODEL // LANES, n_batch * (CONV_HIST + n_new), LANES), F32),
                        pltpu.VMEM((rows, D_MODEL), F32)],
        compiler_params=pltpu.CompilerParams(vmem_limit_bytes=VMEM_LIMIT_BYTES),
        name="conv_sample",
    )(xs, state, wa, wb, wg, w_dw, *vecs[:3])
    ys = _out_proj(zs, xs, wo, vecs[3], vecs[4])
    return yp, ys, cp, cs


def kernel(x_prompt, x_sample, cache_k, cache_v, state_conv, attn_w_in, attn_sink, attn_w_out,
           conv_w_in, conv_w_dw, conv_b_dw, conv_ln_g, conv_ln_b, conv_w_out, post_ln_g, post_ln_b):
    batch, seq, _ = x_prompt.shape
    dec_batch, dec_seq, _ = x_sample.shape
    wc = cache_k.shape[2]
    assert batch == 1 and wc == WIN_ROWS and seq % ROW_BLOCK == 0
    xp = x_prompt.reshape(seq, D_MODEL)
    xs = x_sample.reshape(dec_batch * dec_seq, D_MODEL)

    xp, xs, kp, vp, ksn, vsn = _attn_layer(
        xp, xs, cache_k[0].reshape(dec_batch, wc, KV_WIDTH), cache_v[0].reshape(dec_batch, wc, KV_WIDTH),
        attn_w_in[0], attn_sink[0], attn_w_out[0], post_ln_g[0], post_ln_b[0])
    xp, xs, cp, cs = _conv_layer(
        xp, xs, state_conv[0], conv_w_in[0], conv_w_dw[0], conv_b_dw[0], conv_ln_g[0], conv_ln_b[0],
        conv_w_out[0], post_ln_g[1], post_ln_b[1])

    kv_shape = (1, 1, wc, N_KV_HEADS, HEAD_DIM)
    kvs_shape = (1, dec_batch, wc, N_KV_HEADS, HEAD_DIM)
    return (xp.reshape(batch, seq, D_MODEL), xs.reshape(dec_batch, dec_seq, D_MODEL),
            kp.reshape(kv_shape), vp.reshape(kv_shape),
            ksn.reshape(kvs_shape), vsn.reshape(kvs_shape),
            cp[None, None], cs[None])
```

```python
import functools

import jax
import jax.numpy as jnp
from jax import lax
from jax.experimental import pallas as pl
from jax.experimental.pallas import tpu as pltpu

F32 = jnp.float32
BF16 = jnp.bfloat16

D_MODEL = 2048
DEPTH = 2
PAST_LEN = 1024
CHUNK = 64
HEAD_DIM = 64
N_HEADS = D_MODEL // HEAD_DIM
N_KV_HEADS = 8
GROUP = N_HEADS // N_KV_HEADS
ATTN_WIDTH = N_HEADS * HEAD_DIM
KV_WIDTH = N_KV_HEADS * HEAD_DIM
WIN_CHUNKS = 2
WIN_ROWS = WIN_CHUNKS * CHUNK
ROT_DIM = HEAD_DIM // 4
ROPE_THETA = 500000.0
CONV_WIDTH = 31
LN_EPS = 1e-5
DEEPNORM_ALPHA = (2.0 * DEPTH) ** 0.25
ATTN_SCALE = HEAD_DIM ** -0.5
LOG2E = 1.4426950408889634
Q_SCALE = ATTN_SCALE * LOG2E

LANES = 128
SUBLANES = 8
VMEM_LIMIT_BYTES = 58 * 1024 * 1024
ROW_BLOCK = 256
CONV_HIST = 32
CONV_ROW_TILE = 32
PAIR_ROWS = 2 * CHUNK
PAIR_KEYS = WIN_ROWS + PAIR_ROWS


def _layer_norm(r, g, b):
    mu = jnp.mean(r, axis=-1, keepdims=True)
    d = r - mu
    var = jnp.mean(d * d, axis=-1, keepdims=True)
    return d * lax.rsqrt(var + LN_EPS) * g + b


def _silu(t):
    return t * jax.nn.sigmoid(t)


def _rope_tables(pos, inv_ref, sgn_ref):
    ang = pos * inv_ref[...]
    sin = jnp.sin(ang)
    return jnp.cos(ang), sin if sgn_ref is None else sin * sgn_ref[...]


def _rope(t, cosv, sinv):
    lane = lax.broadcasted_iota(jnp.int32, (1, LANES), 1) % HEAD_DIM
    partner = jnp.where(lane < ROT_DIM // 2,
                        pltpu.roll(t, LANES - ROT_DIM // 2, 1),
                        pltpu.roll(t, ROT_DIM // 2, 1))
    return t * cosv + partner * sinv


def _lo_hi(col):
    low = lax.broadcasted_iota(jnp.int32, (1, LANES), 1) < HEAD_DIM
    swapped = pltpu.roll(col, HEAD_DIM, 1)
    zero = jnp.zeros_like(col)
    even_lo = jnp.where(low, col, zero).astype(BF16)
    even_hi = jnp.where(low, zero, swapped).astype(BF16)
    odd_lo = jnp.where(low, swapped, zero).astype(BF16)
    odd_hi = jnp.where(low, zero, col).astype(BF16)
    return even_lo, even_hi, odd_lo, odd_hi


def _scores(q2, k_lo, k_hi):
    qst = jnp.concatenate([q2[:, :LANES], q2[:, LANES:]], axis=0)
    dims = (((1,), (1,)), ((), ()))
    return (lax.dot_general(qst, k_lo, dims, preferred_element_type=F32),
            lax.dot_general(qst, k_hi, dims, preferred_element_type=F32))


def _sink_softmax(s, sk, bias):
    if bias is not None:
        s = s + bias
    m = jnp.maximum(jnp.max(s, axis=-1, keepdims=True), sk)
    p = jnp.exp2(s - m)
    denom = jnp.sum(p, axis=-1, keepdims=True) + jnp.exp2(sk - m)
    return p.astype(BF16), 1.0 / denom


def _weighted_values(p_a, inv_a, p_b, inv_b, v_lo, v_hi):
    o_a = jnp.dot(p_a, v_lo, preferred_element_type=F32)
    o_b = jnp.dot(p_b, v_hi, preferred_element_type=F32)
    return o_a * inv_a + o_b * inv_b


def _attend_heads(q_of, k_of, v_of, sink_ref, q_rows, bias):
    scores = [_scores(q_of(kv), *k_of(kv)) for kv in range(N_KV_HEADS)]
    probs = []
    for kv, (s_a, s_b) in enumerate(scores):
        sink_a, sink_b = _sink_cols(sink_ref, kv, q_rows)
        probs.append(_sink_softmax(s_a, sink_a, bias) + _sink_softmax(s_b, sink_b, bias))
    return [_weighted_values(*probs[kv], *v_of(kv)) for kv in range(N_KV_HEADS)]


def _sink_cols(sink_ref, kv, q_rows):
    row = lax.broadcasted_iota(jnp.int32, (2 * q_rows, 1), 0)
    first = row < q_rows
    h = kv * GROUP
    return (jnp.where(first, sink_ref[h + 0] * LOG2E, sink_ref[h + 2] * LOG2E),
            jnp.where(first, sink_ref[h + 1] * LOG2E, sink_ref[h + 3] * LOG2E))


def _split_attn_w_in(w_ref):
    k0, v0, g0 = ATTN_WIDTH, ATTN_WIDTH + KV_WIDTH, ATTN_WIDTH + 2 * KV_WIDTH
    return w_ref.at[:, 0:k0], w_ref.at[:, k0:v0], w_ref.at[:, v0:g0], w_ref.at[:, g0:g0 + ATTN_WIDTH]


def _attn_prompt_kernel(x_ref, w_ref, wo_ref, inv_ref, sgn_ref,
                        sink_ref, lng_ref, lnb_ref,
                        y_ref, knew_ref, vnew_ref,
                        qs, klo, khi, vlo, vhi, os_, rope_tab):
    i = pl.program_id(0)
    last = pl.num_programs(0) - 1
    tm = x_ref.shape[0]
    wq_ref, wk_ref, wv_ref, wg_ref = _split_attn_w_in(w_ref)

    @pl.when(i == 0)
    def _():
        for ref in (klo, khi, vlo, vhi):
            ref[0:WIN_ROWS, :] = jnp.zeros((WIN_ROWS, ref.shape[1]), BF16)
        off = lax.broadcasted_iota(jnp.int32, (tm, 1), 0).astype(F32)
        cos_r, sin_r = _rope_tables(off, inv_ref, None)
        rope_tab[0] = cos_r
        rope_tab[1] = sin_r
        rope_tab[2] = cos_r * sgn_ref[...]
        rope_tab[3] = sin_r * sgn_ref[...]

    x = x_ref[...]
    xb = x.astype(BF16)
    start = jnp.full((1, 1), i * tm, jnp.int32).astype(F32)
    cos_s, sin_s = _rope_tables(start, inv_ref, None)
    cosv = cos_s * rope_tab[0] - sin_s * rope_tab[1]
    sinv = sin_s * rope_tab[2] + cos_s * rope_tab[3]

    q = jnp.dot(xb, wq_ref[...], preferred_element_type=F32)
    for j in range(ATTN_WIDTH // LANES):
        sl = slice(j * LANES, (j + 1) * LANES)
        qs[:, sl] = (_rope(q[:, sl], cosv, sinv) * Q_SCALE).astype(BF16)

    k = jnp.dot(xb, wk_ref[...], preferred_element_type=F32)
    v = jnp.dot(xb, wv_ref[...], preferred_element_type=F32)
    new_rows = slice(WIN_ROWS, WIN_ROWS + tm)
    for c in range(KV_WIDTH // LANES):
        sl = slice(c * LANES, (c + 1) * LANES)
        kc = _rope(k[:, sl], cosv, sinv)
        vc = v[:, sl]

        @pl.when(i == last)
        def _():
            knew_ref[:, sl] = kc[tm - WIN_ROWS:, :]
            vnew_ref[:, sl] = vc[tm - WIN_ROWS:, :]

        ev = slice(2 * c * LANES, (2 * c + 1) * LANES)
        od = slice((2 * c + 1) * LANES, (2 * c + 2) * LANES)
        klo[new_rows, ev], khi[new_rows, ev], klo[new_rows, od], khi[new_rows, od] = _lo_hi(kc)
        vlo[new_rows, ev], vhi[new_rows, ev], vlo[new_rows, od], vhi[new_rows, od] = _lo_hi(vc)

    row = lax.broadcasted_iota(jnp.int32, (2 * PAIR_ROWS, 1), 0)
    col = lax.broadcasted_iota(jnp.int32, (1, PAIR_KEYS), 1)
    first_col = jnp.where(row % PAIR_ROWS >= CHUNK, CHUNK, 0)
    in_window = (col >= first_col) & (col < first_col + (PAIR_KEYS - CHUNK))
    window_bias = jnp.where(in_window, 0.0, -jnp.inf)

    def pair_body(p, carry):
        r0 = pl.multiple_of(p * PAIR_ROWS, PAIR_ROWS)
        first_key_chunk = i * (tm // CHUNK) + 2 * p - WIN_CHUNKS
        started = col >= jnp.maximum(0, -first_key_chunk) * CHUNK
        bias = window_bias + jnp.where(started, 0.0, -jnp.inf)
        keys = pl.ds(r0, PAIR_KEYS)

        def head_lanes(kv):
            return slice(kv * LANES, (kv + 1) * LANES)

        outs = _attend_heads(
            lambda kv: qs[pl.ds(r0, PAIR_ROWS), kv * 2 * LANES:(kv + 1) * 2 * LANES],
            lambda kv: (klo[keys, head_lanes(kv)], khi[keys, head_lanes(kv)]),
            lambda kv: (vlo[keys, head_lanes(kv)], vhi[keys, head_lanes(kv)]),
            sink_ref, PAIR_ROWS, bias)
        for kv, o in enumerate(outs):
            os_[pl.ds(r0, PAIR_ROWS), kv * 2 * LANES:kv * 2 * LANES + LANES] = o[:PAIR_ROWS]
            os_[pl.ds(r0, PAIR_ROWS), kv * 2 * LANES + LANES:(kv + 1) * 2 * LANES] = o[PAIR_ROWS:]
        return carry

    lax.fori_loop(0, tm // PAIR_ROWS, pair_body, 0)

    for ref in (klo, khi, vlo, vhi):
        ref[0:WIN_ROWS, :] = ref[tm:tm + WIN_ROWS, :]

    g = jnp.dot(xb, wg_ref[...], preferred_element_type=F32)
    z = (os_[...] * _silu(g)).astype(BF16)
    y = jnp.dot(z, wo_ref[...], preferred_element_type=F32)
    y_ref[...] = _layer_norm(DEEPNORM_ALPHA * x + y, lng_ref[...], lnb_ref[...])


def _attn_sample_kernel(x_ref, ck_ref, cv_ref, w_ref,
                        inv_ref, sgn_ref, sink_ref,
                        z_ref, knew_ref, vnew_ref,
                        qs, ks, vs, os_, *, n_batch, n_new, n_cache):
    rows = n_batch * n_new
    wq_ref, wk_ref, wv_ref, wg_ref = _split_attn_w_in(w_ref)
    x = x_ref[...]
    xb = x.astype(BF16)
    pos = (PAST_LEN + lax.broadcasted_iota(jnp.int32, (rows, 1), 0) % n_new).astype(F32)
    cosv, sinv = _rope_tables(pos, inv_ref, sgn_ref)

    q = jnp.dot(xb, wq_ref[...], preferred_element_type=F32)
    for j in range(ATTN_WIDTH // LANES):
        sl = slice(j * LANES, (j + 1) * LANES)
        qs[:, sl] = (_rope(q[:, sl], cosv, sinv) * Q_SCALE).astype(BF16)
    k = jnp.dot(xb, wk_ref[...], preferred_element_type=F32)
    v = jnp.dot(xb, wv_ref[...], preferred_element_type=F32)
    for c in range(KV_WIDTH // LANES):
        sl = slice(c * LANES, (c + 1) * LANES)
        ks[:, sl] = _rope(k[:, sl], cosv, sinv)
        vs[:, sl] = v[:, sl]

    def batch_body(b, carry):
        r0 = pl.multiple_of(b * n_new, n_new)
        c0 = pl.multiple_of(b * n_cache, n_cache)
        kk = jnp.concatenate([ck_ref[pl.ds(c0, n_cache), :], ks[pl.ds(r0, n_new), :]], axis=0)
        vv = jnp.concatenate([cv_ref[pl.ds(c0, n_cache), :], vs[pl.ds(r0, n_new), :]], axis=0)
        knew_ref[pl.ds(c0, n_cache), :] = kk[n_new:, :]
        vnew_ref[pl.ds(c0, n_cache), :] = vv[n_new:, :]
        k_parts, v_parts = [], []
        for c in range(KV_WIDTH // LANES):
            sl = slice(c * LANES, (c + 1) * LANES)
            k_parts.extend(_lo_hi(kk[:, sl]))
            v_parts.extend(_lo_hi(vv[:, sl]))
        outs = _attend_heads(
            lambda kv: qs[pl.ds(r0, n_new), kv * 2 * LANES:(kv + 1) * 2 * LANES],
            lambda kv: (k_parts[2 * kv], k_parts[2 * kv + 1]),
            lambda kv: (v_parts[2 * kv], v_parts[2 * kv + 1]),
            sink_ref, n_new, None)
        for kv, o in enumerate(outs):
            os_[pl.ds(r0, n_new), kv * 2 * LANES:kv * 2 * LANES + LANES] = o[:n_new]
            os_[pl.ds(r0, n_new), kv * 2 * LANES + LANES:(kv + 1) * 2 * LANES] = o[n_new:]
        return carry

    lax.fori_loop(0, n_batch, batch_body, 0)

    g = jnp.dot(xb, wg_ref[...], preferred_element_type=F32)
    z_ref[...] = (os_[...] * _silu(g)).astype(BF16)


def _out_proj_kernel(z_ref, x_ref, wo_ref, lng_ref, lnb_ref, y_ref):
    y = jnp.dot(z_ref[...], wo_ref[...], preferred_element_type=F32)
    y_ref[...] = _layer_norm(DEEPNORM_ALPHA * x_ref[...] + y, lng_ref[...], lnb_ref[...])


def _depthwise(u_scr, c_scr, wdw_ref, bdw_ref, base, out_base, n_rows):
    off = CONV_HIST - (CONV_WIDTH - 1)
    for l in range(D_MODEL // LANES):
        sl = slice(l * LANES, (l + 1) * LANES)
        acc = jnp.zeros((n_rows, LANES), F32)
        for j in range(CONV_WIDTH):
            acc = acc + wdw_ref[j:j + 1, sl] * u_scr[l, pl.ds(base + j + off, n_rows, stride=1), :]
        c_scr[pl.ds(out_base, n_rows), sl] = acc + bdw_ref[:, sl]


def _conv_gate(c, g, cg_ref, cb_ref):
    cn = _layer_norm(c, cg_ref[...], cb_ref[...])
    return (_silu(cn) * _silu(g)).astype(BF16)


def _split_conv_w_in(w_ref):
    return tuple(w_ref.at[:, n * D_MODEL:(n + 1) * D_MODEL] for n in range(3))


def _conv_prompt_kernel(x_ref, w_ref, wo_ref, wdw_ref, bdw_ref, cg_ref, cb_ref,
                        lng_ref, lnb_ref,
                        y_ref, state_ref,
                        u_scr, c_scr):
    i = pl.program_id(0)
    last = pl.num_programs(0) - 1
    tm = x_ref.shape[0]
    wa_ref, wb_ref, wg_ref = _split_conv_w_in(w_ref)
    n_tiles = D_MODEL // LANES

    @pl.when(i == 0)
    def _():
        u_scr[:, 0:CONV_HIST, :] = jnp.zeros((n_tiles, CONV_HIST, LANES), F32)

    x = x_ref[...]
    xb = x.astype(BF16)
    a = jnp.dot(xb, wa_ref[...], preferred_element_type=F32)
    b = jnp.dot(xb, wb_ref[...], preferred_element_type=F32)
    u = a * jax.nn.sigmoid(b)
    for l in range(n_tiles):
        u_scr[l, CONV_HIST:CONV_HIST + tm, :] = u[:, l * LANES:(l + 1) * LANES]

    def row_body(rt, carry):
        base = pl.multiple_of(rt * CONV_ROW_TILE, CONV_ROW_TILE)
        _depthwise(u_scr, c_scr, wdw_ref, bdw_ref, base, base, CONV_ROW_TILE)
        return carry

    lax.fori_loop(0, tm // CONV_ROW_TILE, row_body, 0)

    @pl.when(i == last)
    def _():
        for l in range(n_tiles):
            state_ref[:, l * LANES:(l + 1) * LANES] = (
                u_scr[l, CONV_HIST + tm - (CONV_WIDTH - 1):CONV_HIST + tm, :])

    u_scr[:, 0:CONV_HIST, :] = u_scr[:, tm:tm + CONV_HIST, :]

    g = jnp.dot(xb, wg_ref[...], preferred_element_type=F32)
    z = _conv_gate(c_scr[...], g, cg_ref, cb_ref)
    y = jnp.dot(z, wo_ref[...], preferred_element_type=F32)
    y_ref[...] = _layer_norm(DEEPNORM_ALPHA * x + y, lng_ref[...], lnb_ref[...])


def _conv_sample_kernel(x_ref, st_ref, w_ref, wdw_ref, bdw_ref, cg_ref, cb_ref,
                        z_ref, state_ref,
                        u_scr, c_scr, *, n_batch, n_new):
    slot = CONV_HIST + n_new
    hist = CONV_WIDTH - 1
    wa_ref, wb_ref, wg_ref = _split_conv_w_in(w_ref)
    x = x_ref[...]
    xb = x.astype(BF16)
    a = jnp.dot(xb, wa_ref[...], preferred_element_type=F32)
    b = jnp.dot(xb, wb_ref[...], preferred_element_type=F32)
    u = a * jax.nn.sigmoid(b)
    pad = CONV_HIST - hist
    for bi in range(n_batch):
        base = bi * slot
        for l in range(D_MODEL // LANES):
            sl = slice(l * LANES, (l + 1) * LANES)
            u_scr[l, base:base + pad, :] = jnp.zeros((pad, LANES), F32)
            u_scr[l, base + pad:base + CONV_HIST, :] = st_ref[bi, :, sl]
            u_scr[l, base + CONV_HIST:base + slot, :] = u[bi * n_new:(bi + 1) * n_new, sl]
            state_ref[bi, :, sl] = u_scr[l, base + slot - hist:base + slot, :]

    def batch_body(bi, carry):
        _depthwise(u_scr, c_scr, wdw_ref, bdw_ref, pl.multiple_of(bi * slot, SUBLANES),
                   pl.multiple_of(bi * n_new, n_new), n_new)
        return carry

    lax.fori_loop(0, n_batch, batch_body, 0)

    g = jnp.dot(xb, wg_ref[...], preferred_element_type=F32)
    z_ref[...] = _conv_gate(c_scr[...], g, cg_ref, cb_ref)


def _const_spec(shape):
    return pl.BlockSpec(shape, lambda i: (0,) * len(shape), pipeline_mode=pl.Buffered(1))


def _vmem_spec():
    return pl.BlockSpec(memory_space=pltpu.VMEM)


def _smem_spec():
    return pl.BlockSpec(memory_space=pltpu.SMEM)


def _out_proj(z, x, wo, lng, lnb):
    return pl.pallas_call(
        _out_proj_kernel,
        in_specs=[_vmem_spec()] * 5,
        out_specs=_vmem_spec(),
        out_shape=jax.ShapeDtypeStruct(x.shape, F32),
        compiler_params=pltpu.CompilerParams(vmem_limit_bytes=VMEM_LIMIT_BYTES),
        name="out_proj_sample",
    )(z, x, wo, lng, lnb)


def _rope_lane_consts():
    half = ROT_DIM // 2
    inv = ROPE_THETA ** (-jnp.arange(half, dtype=F32) * (2.0 / ROT_DIM))
    zeros = jnp.zeros((HEAD_DIM - ROT_DIM,), F32)
    inv_head = jnp.concatenate([inv, inv, zeros])
    sgn_head = jnp.concatenate([-jnp.ones((half,), F32), jnp.ones((half,), F32), zeros])
    reps = LANES // HEAD_DIM
    return jnp.tile(inv_head, reps)[None, :], jnp.tile(sgn_head, reps)[None, :]


def _attn_layer(xp, xs, ck, cv, w_in, sink, w_out, ln_g, ln_b):
    seq = xp.shape[0]
    n_batch, n_cache = ck.shape[0], ck.shape[1]
    n_new = xs.shape[0] // n_batch
    wi = w_in.astype(BF16)
    wo = w_out.astype(BF16)
    inv, sgn = _rope_lane_consts()
    lng, lnb = ln_g[None, :], ln_b[None, :]
    params = pltpu.CompilerParams(dimension_semantics=("arbitrary",), vmem_limit_bytes=VMEM_LIMIT_BYTES)

    tm = ROW_BLOCK
    kv_scr = pltpu.VMEM((WIN_ROWS + tm, N_KV_HEADS * LANES), BF16)
    yp, kp, vp = pl.pallas_call(
        _attn_prompt_kernel,
        grid=(seq // tm,),
        in_specs=[pl.BlockSpec((tm, D_MODEL), lambda i: (i, 0)),
                  _const_spec(wi.shape), _const_spec(wo.shape),
                  _const_spec(inv.shape), _const_spec(sgn.shape), _smem_spec(),
                  _const_spec(lng.shape), _const_spec(lnb.shape)],
        out_specs=[pl.BlockSpec((tm, D_MODEL), lambda i: (i, 0)),
                   pl.BlockSpec((WIN_ROWS, KV_WIDTH), lambda i: (0, 0)),
                   pl.BlockSpec((WIN_ROWS, KV_WIDTH), lambda i: (0, 0))],
        out_shape=[jax.ShapeDtypeStruct((seq, D_MODEL), F32),
                   jax.ShapeDtypeStruct((WIN_ROWS, KV_WIDTH), F32),
                   jax.ShapeDtypeStruct((WIN_ROWS, KV_WIDTH), F32)],
        scratch_shapes=[pltpu.VMEM((tm, ATTN_WIDTH), BF16), kv_scr, kv_scr, kv_scr, kv_scr,
                        pltpu.VMEM((tm, ATTN_WIDTH), F32), pltpu.VMEM((4, tm, LANES), F32)],
        compiler_params=params,
        name="attn_prompt",
    )(xp, wi, wo, inv, sgn, sink, lng, lnb)

    rows = xs.shape[0]
    ck2 = ck.reshape(n_batch * n_cache, KV_WIDTH)
    cv2 = cv.reshape(n_batch * n_cache, KV_WIDTH)
    zs, ksn, vsn = pl.pallas_call(
        functools.partial(_attn_sample_kernel, n_batch=n_batch, n_new=n_new, n_cache=n_cache),
        in_specs=[_vmem_spec()] * 6 + [_smem_spec()],
        out_specs=[_vmem_spec()] * 3,
        out_shape=[jax.ShapeDtypeStruct((rows, ATTN_WIDTH), BF16),
                   jax.ShapeDtypeStruct(ck2.shape, F32),
                   jax.ShapeDtypeStruct(cv2.shape, F32)],
        scratch_shapes=[pltpu.VMEM((rows, ATTN_WIDTH), BF16),
                        pltpu.VMEM((rows, KV_WIDTH), F32), pltpu.VMEM((rows, KV_WIDTH), F32),
                        pltpu.VMEM((rows, ATTN_WIDTH), F32)],
        compiler_params=pltpu.CompilerParams(vmem_limit_bytes=VMEM_LIMIT_BYTES),
        name="attn_sample",
    )(xs, ck2, cv2, wi, inv, sgn, sink)
    ys = _out_proj(zs, xs, wo, lng, lnb)
    return yp, ys, kp, vp, ksn, vsn


def _conv_layer(xp, xs, state, w_in, w_dw, b_dw, cg, cb, w_out, ln_g, ln_b):
    seq = xp.shape[0]
    n_batch = state.shape[0]
    n_new = xs.shape[0] // n_batch
    hist = CONV_WIDTH - 1
    wi = w_in.astype(BF16)
    wo = w_out.astype(BF16)
    vecs = [b_dw[None, :], cg[None, :], cb[None, :], ln_g[None, :], ln_b[None, :]]

    tm = ROW_BLOCK
    yp, cp = pl.pallas_call(
        _conv_prompt_kernel,
        grid=(seq // tm,),
        in_specs=[pl.BlockSpec((tm, D_MODEL), lambda i: (i, 0)),
                  _const_spec(wi.shape), _const_spec(wo.shape), _const_spec(w_dw.shape)]
                 + [_const_spec(t.shape) for t in vecs],
        out_specs=[pl.BlockSpec((tm, D_MODEL), lambda i: (i, 0)),
                   pl.BlockSpec((hist, D_MODEL), lambda i: (0, 0))],
        out_shape=[jax.ShapeDtypeStruct((seq, D_MODEL), F32),
                   jax.ShapeDtypeStruct((hist, D_MODEL), F32)],
        scratch_shapes=[pltpu.VMEM((D_MODEL // LANES, CONV_HIST + tm, LANES), F32),
                        pltpu.VMEM((tm, D_MODEL), F32)],
        compiler_params=pltpu.CompilerParams(dimension_semantics=("arbitrary",),
                                             vmem_limit_bytes=VMEM_LIMIT_BYTES),
        name="conv_prompt",
    )(xp, wi, wo, w_dw, *vecs)

    rows = xs.shape[0]
    zs, cs = pl.pallas_call(
        functools.partial(_conv_sample_kernel, n_batch=n_batch, n_new=n_new),
        in_specs=[_vmem_spec()] * 7,
        out_specs=[_vmem_spec()] * 2,
        out_shape=[jax.ShapeDtypeStruct((rows, D_MODEL), BF16),
                   jax.ShapeDtypeStruct((n_batch, hist, D_MODEL), F32)],
        scratch_shapes=[pltpu.VMEM((D_MODEL // LANES, n_batch * (CONV_HIST + n_new), LANES), F32),
                        pltpu.VMEM((rows, D_MODEL), F32)],
        compiler_params=pltpu.CompilerParams(vmem_limit_bytes=VMEM_LIMIT_BYTES),
        name="conv_sample",
    )(xs, state, wi, w_dw, *vecs[:3])
    ys = _out_proj(zs, xs, wo, vecs[3], vecs[4])
    return yp, ys, cp, cs


def kernel(x_prompt, x_sample, cache_k, cache_v, state_conv, attn_w_in, attn_sink, attn_w_out,
           conv_w_in, conv_w_dw, conv_b_dw, conv_ln_g, conv_ln_b, conv_w_out, post_ln_g, post_ln_b):
    batch, seq, _ = x_prompt.shape
    dec_batch, dec_seq, _ = x_sample.shape
    wc = cache_k.shape[2]
    assert batch == 1 and wc == WIN_ROWS and seq % ROW_BLOCK == 0
    xp = x_prompt.reshape(seq, D_MODEL)
    xs = x_sample.reshape(dec_batch * dec_seq, D_MODEL)

    xp, xs, kp, vp, ksn, vsn = _attn_layer(
        xp, xs, cache_k[0].reshape(dec_batch, wc, KV_WIDTH), cache_v[0].reshape(dec_batch, wc, KV_WIDTH),
        attn_w_in[0], attn_sink[0], attn_w_out[0], post_ln_g[0], post_ln_b[0])
    xp, xs, cp, cs = _conv_layer(
        xp, xs, state_conv[0], conv_w_in[0], conv_w_dw[0], conv_b_dw[0], conv_ln_g[0], conv_ln_b[0],
        conv_w_out[0], post_ln_g[1], post_ln_b[1])

    kv_shape = (1, 1, wc, N_KV_HEADS, HEAD_DIM)
    kvs_shape = (1, dec_batch, wc, N_KV_HEADS, HEAD_DIM)
    return (xp.reshape(batch, seq, D_MODEL), xs.reshape(dec_batch, dec_seq, D_MODEL),
            kp.reshape(kv_shape), vp.reshape(kv_shape),
            ksn.reshape(kvs_shape), vsn.reshape(kvs_shape),
            cp[None, None], cs[None])
```

```python
import functools

import jax
import jax.numpy as jnp
from jax import lax
from jax.experimental import pallas as pl
from jax.experimental.pallas import tpu as pltpu

F32 = jnp.float32
BF16 = jnp.bfloat16

D_MODEL = 2048
DEPTH = 2
PAST_LEN = 1024
CHUNK = 64
HEAD_DIM = 64
N_HEADS = D_MODEL // HEAD_DIM
N_KV_HEADS = 8
GROUP = N_HEADS // N_KV_HEADS
ATTN_WIDTH = N_HEADS * HEAD_DIM
KV_WIDTH = N_KV_HEADS * HEAD_DIM
WIN_CHUNKS = 2
WIN_ROWS = WIN_CHUNKS * CHUNK
ROT_DIM = HEAD_DIM // 4
ROPE_THETA = 500000.0
CONV_WIDTH = 31
LN_EPS = 1e-5
DEEPNORM_ALPHA = (2.0 * DEPTH) ** 0.25
ATTN_SCALE = HEAD_DIM ** -0.5
LOG2E = 1.4426950408889634
Q_SCALE = ATTN_SCALE * LOG2E

LANES = 128
SUBLANES = 8
VMEM_LIMIT_BYTES = 58 * 1024 * 1024
ROW_BLOCK = 256
CONV_HIST = 32
CONV_ROW_TILE = 32
PAIR_ROWS = 2 * CHUNK
PAIR_KEYS = WIN_ROWS + PAIR_ROWS


def _layer_norm(r, g, b):
    mu = jnp.mean(r, axis=-1, keepdims=True)
    d = r - mu
    var = jnp.mean(d * d, axis=-1, keepdims=True)
    return d * lax.rsqrt(var + LN_EPS) * g + b


def _silu(t):
    return t * jax.nn.sigmoid(t)


def _rope_tables(pos, inv_ref, sgn_ref):
    ang = pos * inv_ref[...]
    sin = jnp.sin(ang)
    return jnp.cos(ang), sin if sgn_ref is None else sin * sgn_ref[...]


def _rope(t, cosv, sinv):
    lane = lax.broadcasted_iota(jnp.int32, (1, LANES), 1) % HEAD_DIM
    partner = jnp.where(lane < ROT_DIM // 2,
                        pltpu.roll(t, LANES - ROT_DIM // 2, 1),
                        pltpu.roll(t, ROT_DIM // 2, 1))
    return t * cosv + partner * sinv


def _lo_hi(col):
    low = lax.broadcasted_iota(jnp.int32, (1, LANES), 1) < HEAD_DIM
    swapped = pltpu.roll(col, HEAD_DIM, 1)
    zero = jnp.zeros_like(col)
    even_lo = jnp.where(low, col, zero).astype(BF16)
    even_hi = jnp.where(low, zero, swapped).astype(BF16)
    odd_lo = jnp.where(low, swapped, zero).astype(BF16)
    odd_hi = jnp.where(low, zero, col).astype(BF16)
    return even_lo, even_hi, odd_lo, odd_hi


def _scores(q2, k_lo, k_hi):
    qst = jnp.concatenate([q2[:, :LANES], q2[:, LANES:]], axis=0)
    dims = (((1,), (1,)), ((), ()))
    return (lax.dot_general(qst, k_lo, dims, preferred_element_type=F32),
            lax.dot_general(qst, k_hi, dims, preferred_element_type=F32))


def _sink_softmax(s, sk, bias):
    if bias is not None:
        s = s + bias
    m = jnp.maximum(jnp.max(s, axis=-1, keepdims=True), sk)
    p = jnp.exp2(s - m)
    denom = jnp.sum(p, axis=-1, keepdims=True) + jnp.exp2(sk - m)
    return p.astype(BF16), 1.0 / denom


def _weighted_values(p_a, inv_a, p_b, inv_b, v_lo, v_hi):
    o_a = jnp.dot(p_a, v_lo, preferred_element_type=F32)
    o_b = jnp.dot(p_b, v_hi, preferred_element_type=F32)
    return o_a * inv_a + o_b * inv_b


def _attend_heads(q_of, k_of, v_of, sink_ref, q_rows, bias):
    scores = [_scores(q_of(kv), *k_of(kv)) for kv in range(N_KV_HEADS)]
    probs = []
    for kv, (s_a, s_b) in enumerate(scores):
        sink_a, sink_b = _sink_cols(sink_ref, kv, q_rows)
        probs.append(_sink_softmax(s_a, sink_a, bias) + _sink_softmax(s_b, sink_b, bias))
    return [_weighted_values(*probs[kv], *v_of(kv)) for kv in range(N_KV_HEADS)]


def _sink_cols(sink_ref, kv, q_rows):
    row = lax.broadcasted_iota(jnp.int32, (2 * q_rows, 1), 0)
    first = row < q_rows
    h = kv * GROUP
    return (jnp.where(first, sink_ref[h + 0] * LOG2E, sink_ref[h + 2] * LOG2E),
            jnp.where(first, sink_ref[h + 1] * LOG2E, sink_ref[h + 3] * LOG2E))


def _split_attn_w_in(w_ref):
    k0, v0, g0 = ATTN_WIDTH, ATTN_WIDTH + KV_WIDTH, ATTN_WIDTH + 2 * KV_WIDTH
    return w_ref.at[:, 0:k0], w_ref.at[:, k0:v0], w_ref.at[:, v0:g0], w_ref.at[:, g0:g0 + ATTN_WIDTH]


def _attn_prompt_kernel(x_ref, w_ref, wo_ref, inv_ref, sgn_ref,
                        sink_ref, lng_ref, lnb_ref,
                        y_ref, knew_ref, vnew_ref,
                        qs, klo, khi, vlo, vhi, os_, rope_tab):
    i = pl.program_id(0)
    last = pl.num_programs(0) - 1
    tm = x_ref.shape[0]
    wq_ref, wk_ref, wv_ref, wg_ref = _split_attn_w_in(w_ref)

    @pl.when(i == 0)
    def _():
        for ref in (klo, khi, vlo, vhi):
            ref[0:WIN_ROWS, :] = jnp.zeros((WIN_ROWS, ref.shape[1]), BF16)
        off = lax.broadcasted_iota(jnp.int32, (tm, 1), 0).astype(F32)
        cos_r, sin_r = _rope_tables(off, inv_ref, None)
        rope_tab[0] = cos_r
        rope_tab[1] = sin_r
        rope_tab[2] = cos_r * sgn_ref[...]
        rope_tab[3] = sin_r * sgn_ref[...]

    x = x_ref[...]
    xb = x.astype(BF16)
    start = jnp.full((1, 1), i * tm, jnp.int32).astype(F32)
    cos_s, sin_s = _rope_tables(start, inv_ref, None)
    cosv = cos_s * rope_tab[0] - sin_s * rope_tab[1]
    sinv = sin_s * rope_tab[2] + cos_s * rope_tab[3]

    q = jnp.dot(xb, wq_ref[...], preferred_element_type=F32)
    for j in range(ATTN_WIDTH // LANES):
        sl = slice(j * LANES, (j + 1) * LANES)
        qs[:, sl] = (_rope(q[:, sl], cosv, sinv) * Q_SCALE).astype(BF16)

    k = jnp.dot(xb, wk_ref[...], preferred_element_type=F32)
    v = jnp.dot(xb, wv_ref[...], preferred_element_type=F32)
    new_rows = slice(WIN_ROWS, WIN_ROWS + tm)
    for c in range(KV_WIDTH // LANES):
        sl = slice(c * LANES, (c + 1) * LANES)
        kc = _rope(k[:, sl], cosv, sinv)
        vc = v[:, sl]

        @pl.when(i == last)
        def _():
            knew_ref[sl, :] = kc[tm - WIN_ROWS:, :].T
            vnew_ref[sl, :] = vc[tm - WIN_ROWS:, :].T

        ev = slice(2 * c * LANES, (2 * c + 1) * LANES)
        od = slice((2 * c + 1) * LANES, (2 * c + 2) * LANES)
        klo[new_rows, ev], khi[new_rows, ev], klo[new_rows, od], khi[new_rows, od] = _lo_hi(kc)
        vlo[new_rows, ev], vhi[new_rows, ev], vlo[new_rows, od], vhi[new_rows, od] = _lo_hi(vc)

    row = lax.broadcasted_iota(jnp.int32, (2 * PAIR_ROWS, 1), 0)
    col = lax.broadcasted_iota(jnp.int32, (1, PAIR_KEYS), 1)
    first_col = jnp.where(row % PAIR_ROWS >= CHUNK, CHUNK, 0)
    in_window = (col >= first_col) & (col < first_col + (PAIR_KEYS - CHUNK))
    window_bias = jnp.where(in_window, 0.0, -jnp.inf)

    def pair_body(p, carry):
        r0 = pl.multiple_of(p * PAIR_ROWS, PAIR_ROWS)
        first_key_chunk = i * (tm // CHUNK) + 2 * p - WIN_CHUNKS
        started = col >= jnp.maximum(0, -first_key_chunk) * CHUNK
        bias = window_bias + jnp.where(started, 0.0, -jnp.inf)
        keys = pl.ds(r0, PAIR_KEYS)

        def head_lanes(kv):
            return slice(kv * LANES, (kv + 1) * LANES)

        outs = _attend_heads(
            lambda kv: qs[pl.ds(r0, PAIR_ROWS), kv * 2 * LANES:(kv + 1) * 2 * LANES],
            lambda kv: (klo[keys, head_lanes(kv)], khi[keys, head_lanes(kv)]),
            lambda kv: (vlo[keys, head_lanes(kv)], vhi[keys, head_lanes(kv)]),
            sink_ref, PAIR_ROWS, bias)
        for kv, o in enumerate(outs):
            os_[pl.ds(r0, PAIR_ROWS), kv * 2 * LANES:kv * 2 * LANES + LANES] = o[:PAIR_ROWS]
            os_[pl.ds(r0, PAIR_ROWS), kv * 2 * LANES + LANES:(kv + 1) * 2 * LANES] = o[PAIR_ROWS:]
        return carry

    lax.fori_loop(0, tm // PAIR_ROWS, pair_body, 0)

    for ref in (klo, khi, vlo, vhi):
        ref[0:WIN_ROWS, :] = ref[tm:tm + WIN_ROWS, :]

    g = jnp.dot(xb, wg_ref[...], preferred_element_type=F32)
    z = (os_[...] * _silu(g)).astype(BF16)
    y = jnp.dot(z, wo_ref[...], preferred_element_type=F32)
    y_ref[...] = _layer_norm(DEEPNORM_ALPHA * x + y, lng_ref[...], lnb_ref[...])


def _attn_sample_kernel(x_ref, ck_ref, cv_ref, w_ref,
                        inv_ref, sgn_ref, sink_ref,
                        z_ref, knew_ref, vnew_ref,
                        qs, ks, vs, os_, *, n_batch, n_new, n_cache):
    rows = n_batch * n_new
    wq_ref, wk_ref, wv_ref, wg_ref = _split_attn_w_in(w_ref)
    x = x_ref[...]
    xb = x.astype(BF16)
    pos = (PAST_LEN + lax.broadcasted_iota(jnp.int32, (rows, 1), 0) % n_new).astype(F32)
    cosv, sinv = _rope_tables(pos, inv_ref, sgn_ref)

    q = jnp.dot(xb, wq_ref[...], preferred_element_type=F32)
    for j in range(ATTN_WIDTH // LANES):
        sl = slice(j * LANES, (j + 1) * LANES)
        qs[:, sl] = (_rope(q[:, sl], cosv, sinv) * Q_SCALE).astype(BF16)
    k = jnp.dot(xb, wk_ref[...], preferred_element_type=F32)
    v = jnp.dot(xb, wv_ref[...], preferred_element_type=F32)
    k_t = jnp.concatenate([_rope(k[:, c * LANES:(c + 1) * LANES], cosv, sinv)
                           for c in range(KV_WIDTH // LANES)], axis=1).T
    v_t = v.T
    for j in range(rows // LANES):
        ks[j] = k_t[:, j * LANES:(j + 1) * LANES]
        vs[j] = v_t[:, j * LANES:(j + 1) * LANES]

    per_tile = LANES // n_new
    lane = lax.broadcasted_iota(jnp.int32, (1, LANES), 1)
    zeros = jnp.zeros((HEAD_DIM, 2 * LANES), BF16)
    dims_nt = (((1,), (1,)), ((), ()))

    def batch_body(b, carry):
        r0 = pl.multiple_of(b * n_new, n_new)
        off = (b % per_tile) * n_new
        k_old, v_old = ck_ref[b], cv_ref[b]
        k_new, v_new = ks[b // per_tile], vs[b // per_tile]
        mine = (lane >= off) & (lane < off + n_new)

        to_end = n_cache - n_new - off
        to_end = jnp.where(to_end < 0, to_end + LANES, to_end)
        for old, new, out_ref in ((k_old, k_new, knew_ref), (v_old, v_new, vnew_ref)):
            out_ref[b] = jnp.where(lane < n_cache - n_new,
                                   pltpu.roll(old, n_cache - n_new, 1), pltpu.roll(new, to_end, 1))

        bias = jnp.concatenate([jnp.zeros((1, n_cache), F32), jnp.where(mine, 0.0, -jnp.inf)], axis=1)
        k_all = jnp.concatenate([k_old, k_new], axis=1).astype(BF16)
        v_all = jnp.concatenate([v_old, jnp.where(mine, v_new, 0.0)], axis=1).astype(BF16)

        scores = []
        for kv in range(N_KV_HEADS):
            q2 = qs[pl.ds(r0, n_new), kv * 2 * LANES:(kv + 1) * 2 * LANES]
            qst = jnp.concatenate([q2[:, :LANES], q2[:, LANES:]], axis=0)
            k_kv = k_all[kv * HEAD_DIM:(kv + 1) * HEAD_DIM, :]
            scores.append((jnp.dot(qst, jnp.concatenate([k_kv, zeros], axis=0), preferred_element_type=F32),
                           jnp.dot(qst, jnp.concatenate([zeros, k_kv], axis=0), preferred_element_type=F32)))
        probs = []
        for kv, (s_a, s_b) in enumerate(scores):
            sink_a, sink_b = _sink_cols(sink_ref, kv, n_new)
            probs.append(_sink_softmax(s_a, sink_a, bias) + _sink_softmax(s_b, sink_b, bias))
        for kv, (p_a, inv_a, p_b, inv_b) in enumerate(probs):
            v_kv = v_all[kv * HEAD_DIM:(kv + 1) * HEAD_DIM, :]
            o_a = lax.dot_general(p_a, jnp.concatenate([v_kv, zeros], axis=0), dims_nt,
                                  preferred_element_type=F32)
            o_b = lax.dot_general(p_b, jnp.concatenate([zeros, v_kv], axis=0), dims_nt,
                                  preferred_element_type=F32)
            o = o_a * inv_a + o_b * inv_b
            os_[pl.ds(r0, n_new), kv * 2 * LANES:kv * 2 * LANES + LANES] = o[:n_new]
            os_[pl.ds(r0, n_new), kv * 2 * LANES + LANES:(kv + 1) * 2 * LANES] = o[n_new:]
        return carry

    lax.fori_loop(0, n_batch, batch_body, 0)

    g = jnp.dot(xb, wg_ref[...], preferred_element_type=F32)
    z_ref[...] = (os_[...] * _silu(g)).astype(BF16)


def _out_proj_kernel(z_ref, x_ref, wo_ref, lng_ref, lnb_ref, y_ref):
    y = jnp.dot(z_ref[...], wo_ref[...], preferred_element_type=F32)
    y_ref[...] = _layer_norm(DEEPNORM_ALPHA * x_ref[...] + y, lng_ref[...], lnb_ref[...])


def _depthwise(u_scr, c_scr, wdw_ref, bdw_ref, base, out_base, n_rows):
    off = CONV_HIST - (CONV_WIDTH - 1)
    for l in range(D_MODEL // LANES):
        sl = slice(l * LANES, (l + 1) * LANES)
        acc = jnp.zeros((n_rows, LANES), F32)
        for j in range(CONV_WIDTH):
            acc = acc + wdw_ref[j:j + 1, sl] * u_scr[l, pl.ds(base + j + off, n_rows, stride=1), :]
        c_scr[pl.ds(out_base, n_rows), sl] = acc + bdw_ref[:, sl]


def _conv_gate(c, g, cg_ref, cb_ref):
    cn = _layer_norm(c, cg_ref[...], cb_ref[...])
    return (_silu(cn) * _silu(g)).astype(BF16)


def _split_conv_w_in(w_ref):
    return tuple(w_ref.at[:, n * D_MODEL:(n + 1) * D_MODEL] for n in range(3))


def _conv_prompt_kernel(x_ref, w_ref, wo_ref, wdw_ref, bdw_ref, cg_ref, cb_ref,
                        lng_ref, lnb_ref,
                        y_ref, state_ref,
                        u_scr, c_scr):
    i = pl.program_id(0)
    last = pl.num_programs(0) - 1
    tm = x_ref.shape[0]
    wa_ref, wb_ref, wg_ref = _split_conv_w_in(w_ref)
    n_tiles = D_MODEL // LANES

    @pl.when(i == 0)
    def _():
        u_scr[:, 0:CONV_HIST, :] = jnp.zeros((n_tiles, CONV_HIST, LANES), F32)

    x = x_ref[...]
    xb = x.astype(BF16)
    a = jnp.dot(xb, wa_ref[...], preferred_element_type=F32)
    b = jnp.dot(xb, wb_ref[...], preferred_element_type=F32)
    u = a * jax.nn.sigmoid(b)
    for l in range(n_tiles):
        u_scr[l, CONV_HIST:CONV_HIST + tm, :] = u[:, l * LANES:(l + 1) * LANES]

    def row_body(rt, carry):
        base = pl.multiple_of(rt * CONV_ROW_TILE, CONV_ROW_TILE)
        _depthwise(u_scr, c_scr, wdw_ref, bdw_ref, base, base, CONV_ROW_TILE)
        return carry

    lax.fori_loop(0, tm // CONV_ROW_TILE, row_body, 0)

    @pl.when(i == last)
    def _():
        for l in range(n_tiles):
            state_ref[:, l * LANES:(l + 1) * LANES] = (
                u_scr[l, CONV_HIST + tm - (CONV_WIDTH - 1):CONV_HIST + tm, :])

    u_scr[:, 0:CONV_HIST, :] = u_scr[:, tm:tm + CONV_HIST, :]

    g = jnp.dot(xb, wg_ref[...], preferred_element_type=F32)
    z = _conv_gate(c_scr[...], g, cg_ref, cb_ref)
    y = jnp.dot(z, wo_ref[...], preferred_element_type=F32)
    y_ref[...] = _layer_norm(DEEPNORM_ALPHA * x + y, lng_ref[...], lnb_ref[...])


def _conv_sample_kernel(x_ref, st_ref, w_ref, wdw_ref, bdw_ref, cg_ref, cb_ref,
                        z_ref, state_ref,
                        u_scr, c_scr, *, n_batch, n_new):
    slot = CONV_HIST + n_new
    hist = CONV_WIDTH - 1
    wa_ref, wb_ref, wg_ref = _split_conv_w_in(w_ref)
    x = x_ref[...]
    xb = x.astype(BF16)
    a = jnp.dot(xb, wa_ref[...], preferred_element_type=F32)
    b = jnp.dot(xb, wb_ref[...], preferred_element_type=F32)
    u = a * jax.nn.sigmoid(b)
    pad = CONV_HIST - hist
    for bi in range(n_batch):
        base = bi * slot
        for l in range(D_MODEL // LANES):
            sl = slice(l * LANES, (l + 1) * LANES)
            u_scr[l, base:base + pad, :] = jnp.zeros((pad, LANES), F32)
            u_scr[l, base + pad:base + CONV_HIST, :] = st_ref[bi, :, sl]
            u_scr[l, base + CONV_HIST:base + slot, :] = u[bi * n_new:(bi + 1) * n_new, sl]
            state_ref[bi, :, sl] = u_scr[l, base + slot - hist:base + slot, :]

    def batch_body(bi, carry):
        _depthwise(u_scr, c_scr, wdw_ref, bdw_ref, pl.multiple_of(bi * slot, SUBLANES),
                   pl.multiple_of(bi * n_new, n_new), n_new)
        return carry

    lax.fori_loop(0, n_batch, batch_body, 0)

    g = jnp.dot(xb, wg_ref[...], preferred_element_type=F32)
    z_ref[...] = _conv_gate(c_scr[...], g, cg_ref, cb_ref)


def _const_spec(shape):
    return pl.BlockSpec(shape, lambda i: (0,) * len(shape), pipeline_mode=pl.Buffered(1))


def _vmem_spec():
    return pl.BlockSpec(memory_space=pltpu.VMEM)


def _smem_spec():
    return pl.BlockSpec(memory_space=pltpu.SMEM)


def _out_proj(z, x, wo, lng, lnb):
    return pl.pallas_call(
        _out_proj_kernel,
        in_specs=[_vmem_spec()] * 5,
        out_specs=_vmem_spec(),
        out_shape=jax.ShapeDtypeStruct(x.shape, F32),
        compiler_params=pltpu.CompilerParams(vmem_limit_bytes=VMEM_LIMIT_BYTES),
        name="out_proj_sample",
    )(z, x, wo, lng, lnb)


def _rope_lane_consts():
    half = ROT_DIM // 2
    inv = ROPE_THETA ** (-jnp.arange(half, dtype=F32) * (2.0 / ROT_DIM))
    zeros = jnp.zeros((HEAD_DIM - ROT_DIM,), F32)
    inv_head = jnp.concatenate([inv, inv, zeros])
    sgn_head = jnp.concatenate([-jnp.ones((half,), F32), jnp.ones((half,), F32), zeros])
    reps = LANES // HEAD_DIM
    return jnp.tile(inv_head, reps)[None, :], jnp.tile(sgn_head, reps)[None, :]


def _attn_layer(xp, xs, ck, cv, w_in, sink, w_out, ln_g, ln_b):
    seq = xp.shape[0]
    n_batch, n_cache = ck.shape[0], ck.shape[2]
    n_new = xs.shape[0] // n_batch
    assert n_cache == LANES and LANES % n_new == 0 and xs.shape[0] % LANES == 0
    wi = w_in.astype(BF16)
    wo = w_out.astype(BF16)
    inv, sgn = _rope_lane_consts()
    lng, lnb = ln_g[None, :], ln_b[None, :]
    params = pltpu.CompilerParams(dimension_semantics=("arbitrary",), vmem_limit_bytes=VMEM_LIMIT_BYTES)

    tm = ROW_BLOCK
    kv_scr = pltpu.VMEM((WIN_ROWS + tm, N_KV_HEADS * LANES), BF16)
    yp, kp, vp = pl.pallas_call(
        _attn_prompt_kernel,
        grid=(seq // tm,),
        in_specs=[pl.BlockSpec((tm, D_MODEL), lambda i: (i, 0)),
                  _const_spec(wi.shape), _const_spec(wo.shape),
                  _const_spec(inv.shape), _const_spec(sgn.shape), _smem_spec(),
                  _const_spec(lng.shape), _const_spec(lnb.shape)],
        out_specs=[pl.BlockSpec((tm, D_MODEL), lambda i: (i, 0)),
                   pl.BlockSpec((KV_WIDTH, WIN_ROWS), lambda i: (0, 0)),
                   pl.BlockSpec((KV_WIDTH, WIN_ROWS), lambda i: (0, 0))],
        out_shape=[jax.ShapeDtypeStruct((seq, D_MODEL), F32),
                   jax.ShapeDtypeStruct((KV_WIDTH, WIN_ROWS), F32),
                   jax.ShapeDtypeStruct((KV_WIDTH, WIN_ROWS), F32)],
        scratch_shapes=[pltpu.VMEM((tm, ATTN_WIDTH), BF16), kv_scr, kv_scr, kv_scr, kv_scr,
                        pltpu.VMEM((tm, ATTN_WIDTH), F32), pltpu.VMEM((4, tm, LANES), F32)],
        compiler_params=params,
        name="attn_prompt",
    )(xp, wi, wo, inv, sgn, sink, lng, lnb)

    rows = xs.shape[0]
    new_t = pltpu.VMEM((rows // LANES, KV_WIDTH, LANES), F32)
    zs, ksn, vsn = pl.pallas_call(
        functools.partial(_attn_sample_kernel, n_batch=n_batch, n_new=n_new, n_cache=n_cache),
        in_specs=[_vmem_spec()] * 6 + [_smem_spec()],
        out_specs=[_vmem_spec()] * 3,
        out_shape=[jax.ShapeDtypeStruct((rows, ATTN_WIDTH), BF16),
                   jax.ShapeDtypeStruct(ck.shape, F32),
                   jax.ShapeDtypeStruct(cv.shape, F32)],
        scratch_shapes=[pltpu.VMEM((rows, ATTN_WIDTH), BF16), new_t, new_t,
                        pltpu.VMEM((rows, ATTN_WIDTH), F32)],
        compiler_params=pltpu.CompilerParams(vmem_limit_bytes=VMEM_LIMIT_BYTES),
        name="attn_sample",
    )(xs, ck, cv, wi, inv, sgn, sink)
    ys = _out_proj(zs, xs, wo, lng, lnb)
    return yp, ys, kp, vp, ksn, vsn


def _conv_layer(xp, xs, state, w_in, w_dw, b_dw, cg, cb, w_out, ln_g, ln_b):
    seq = xp.shape[0]
    n_batch = state.shape[0]
    n_new = xs.shape[0] // n_batch
    hist = CONV_WIDTH - 1
    wi = w_in.astype(BF16)
    wo = w_out.astype(BF16)
    vecs = [b_dw[None, :], cg[None, :], cb[None, :], ln_g[None, :], ln_b[None, :]]

    tm = ROW_BLOCK
    yp, cp = pl.pallas_call(
        _conv_prompt_kernel,
        grid=(seq // tm,),
        in_specs=[pl.BlockSpec((tm, D_MODEL), lambda i: (i, 0)),
                  _const_spec(wi.shape), _const_spec(wo.shape), _const_spec(w_dw.shape)]
                 + [_const_spec(t.shape) for t in vecs],
        out_specs=[pl.BlockSpec((tm, D_MODEL), lambda i: (i, 0)),
                   pl.BlockSpec((hist, D_MODEL), lambda i: (0, 0))],
        out_shape=[jax.ShapeDtypeStruct((seq, D_MODEL), F32),
                   jax.ShapeDtypeStruct((hist, D_MODEL), F32)],
        scratch_shapes=[pltpu.VMEM((D_MODEL // LANES, CONV_HIST + tm, LANES), F32),
                        pltpu.VMEM((tm, D_MODEL), F32)],
        compiler_params=pltpu.CompilerParams(dimension_semantics=("arbitrary",),
                                             vmem_limit_bytes=VMEM_LIMIT_BYTES),
        name="conv_prompt",
    )(xp, wi, wo, w_dw, *vecs)

    rows = xs.shape[0]
    zs, cs = pl.pallas_call(
        functools.partial(_conv_sample_kernel, n_batch=n_batch, n_new=n_new),
        in_specs=[_vmem_spec()] * 7,
        out_specs=[_vmem_spec()] * 2,
        out_shape=[jax.ShapeDtypeStruct((rows, D_MODEL), BF16),
                   jax.ShapeDtypeStruct((n_batch, hist, D_MODEL), F32)],
        scratch_shapes=[pltpu.VMEM((D_MODEL // LANES, n_batch * (CONV_HIST + n_new), LANES), F32),
                        pltpu.VMEM((rows, D_MODEL), F32)],
        compiler_params=pltpu.CompilerParams(vmem_limit_bytes=VMEM_LIMIT_BYTES),
        name="conv_sample",
    )(xs, state, wi, w_dw, *vecs[:3])
    ys = _out_proj(zs, xs, wo, vecs[3], vecs[4])
    return yp, ys, cp, cs


def kernel(x_prompt, x_sample, cache_k, cache_v, state_conv, attn_w_in, attn_sink, attn_w_out,
           conv_w_in, conv_w_dw, conv_b_dw, conv_ln_g, conv_ln_b, conv_w_out, post_ln_g, post_ln_b):
    batch, seq, _ = x_prompt.shape
    dec_batch, dec_seq, _ = x_sample.shape
    wc = cache_k.shape[2]
    assert batch == 1 and wc == WIN_ROWS and seq % ROW_BLOCK == 0
    xp = x_prompt.reshape(seq, D_MODEL)
    xs = x_sample.reshape(dec_batch * dec_seq, D_MODEL)

    def feature_major(c):
        return jnp.transpose(c, (0, 2, 3, 1)).reshape(c.shape[0], KV_WIDTH, wc)

    def frame_major(c_t, n):
        return jnp.transpose(c_t.reshape(n, N_KV_HEADS, HEAD_DIM, wc), (0, 3, 1, 2))[None]

    xp, xs, kp, vp, ksn, vsn = _attn_layer(
        xp, xs, feature_major(cache_k[0]), feature_major(cache_v[0]),
        attn_w_in[0], attn_sink[0], attn_w_out[0], post_ln_g[0], post_ln_b[0])
    xp, xs, cp, cs = _conv_layer(
        xp, xs, state_conv[0], conv_w_in[0], conv_w_dw[0], conv_b_dw[0], conv_ln_g[0], conv_ln_b[0],
        conv_w_out[0], post_ln_g[1], post_ln_b[1])

    return (xp.reshape(batch, seq, D_MODEL), xs.reshape(dec_batch, dec_seq, D_MODEL),
            frame_major(kp, 1), frame_major(vp, 1),
            frame_major(ksn, dec_batch), frame_major(vsn, dec_batch),
            cp[None, None], cs[None])
```

```python
import functools

import jax
import jax.numpy as jnp
from jax import lax
from jax.experimental import pallas as pl
from jax.experimental.pallas import tpu as pltpu

F32 = jnp.float32
BF16 = jnp.bfloat16

D_MODEL = 2048
DEPTH = 2
PAST_LEN = 1024
CHUNK = 64
HEAD_DIM = 64
N_HEADS = D_MODEL // HEAD_DIM
N_KV_HEADS = 8
GROUP = N_HEADS // N_KV_HEADS
ATTN_WIDTH = N_HEADS * HEAD_DIM
KV_WIDTH = N_KV_HEADS * HEAD_DIM
WIN_CHUNKS = 2
WIN_ROWS = WIN_CHUNKS * CHUNK
ROT_DIM = HEAD_DIM // 4
ROPE_THETA = 500000.0
CONV_WIDTH = 31
LN_EPS = 1e-5
DEEPNORM_ALPHA = (2.0 * DEPTH) ** 0.25
ATTN_SCALE = HEAD_DIM ** -0.5
LOG2E = 1.4426950408889634
Q_SCALE = ATTN_SCALE * LOG2E

LANES = 128
SUBLANES = 8
VMEM_LIMIT_BYTES = 60 * 1024 * 1024
ROW_BLOCK = 256
CONV_HIST = 32
CONV_ROW_TILE = 32
PAIR_ROWS = 2 * CHUNK
PAIR_KEYS = WIN_ROWS + PAIR_ROWS


def _layer_norm(r, g, b):
    mu = jnp.mean(r, axis=-1, keepdims=True)
    d = r - mu
    var = jnp.mean(d * d, axis=-1, keepdims=True)
    return d * lax.rsqrt(var + LN_EPS) * g + b


def _silu(t):
    return t * jax.nn.sigmoid(t)


def _rope_tables(pos, inv_ref, sgn_ref):
    ang = pos * inv_ref[...]
    sin = jnp.sin(ang)
    return jnp.cos(ang), sin if sgn_ref is None else sin * sgn_ref[...]


def _rope(t, cosv, sinv):
    lane = lax.broadcasted_iota(jnp.int32, (1, LANES), 1) % HEAD_DIM
    partner = jnp.where(lane < ROT_DIM // 2,
                        pltpu.roll(t, LANES - ROT_DIM // 2, 1),
                        pltpu.roll(t, ROT_DIM // 2, 1))
    return t * cosv + partner * sinv


def _lo_hi(col):
    low = lax.broadcasted_iota(jnp.int32, (1, LANES), 1) < HEAD_DIM
    swapped = pltpu.roll(col, HEAD_DIM, 1)
    zero = jnp.zeros_like(col)
    even_lo = jnp.where(low, col, zero).astype(BF16)
    even_hi = jnp.where(low, zero, swapped).astype(BF16)
    odd_lo = jnp.where(low, swapped, zero).astype(BF16)
    odd_hi = jnp.where(low, zero, col).astype(BF16)
    return even_lo, even_hi, odd_lo, odd_hi


def _scores(q2, k_lo, k_hi):
    qst = jnp.concatenate([q2[:, :LANES], q2[:, LANES:]], axis=0)
    dims = (((1,), (1,)), ((), ()))
    return (lax.dot_general(qst, k_lo, dims, preferred_element_type=F32),
            lax.dot_general(qst, k_hi, dims, preferred_element_type=F32))


def _sink_softmax(s, sk, bias):
    if bias is not None:
        s = s + bias
    m = jnp.maximum(jnp.max(s, axis=-1, keepdims=True), sk)
    p = jnp.exp2(s - m)
    denom = jnp.sum(p, axis=-1, keepdims=True) + jnp.exp2(sk - m)
    return p.astype(BF16), 1.0 / denom


def _weighted_values(p_a, inv_a, p_b, inv_b, v_lo, v_hi):
    o_a = jnp.dot(p_a, v_lo, preferred_element_type=F32)
    o_b = jnp.dot(p_b, v_hi, preferred_element_type=F32)
    return o_a * inv_a + o_b * inv_b


def _attend_heads(q_of, k_of, v_of, sink_ref, q_rows, bias):
    scores = [_scores(q_of(kv), *k_of(kv)) for kv in range(N_KV_HEADS)]
    probs = []
    for kv, (s_a, s_b) in enumerate(scores):
        sink_a, sink_b = _sink_cols(sink_ref, kv, q_rows)
        probs.append(_sink_softmax(s_a, sink_a, bias) + _sink_softmax(s_b, sink_b, bias))
    return [_weighted_values(*probs[kv], *v_of(kv)) for kv in range(N_KV_HEADS)]


def _sink_cols(sink_ref, kv, q_rows):
    row = lax.broadcasted_iota(jnp.int32, (2 * q_rows, 1), 0)
    first = row < q_rows
    h = kv * GROUP
    return (jnp.where(first, sink_ref[h + 0] * LOG2E, sink_ref[h + 2] * LOG2E),
            jnp.where(first, sink_ref[h + 1] * LOG2E, sink_ref[h + 3] * LOG2E))


def _split_attn_w_in(w_ref):
    k0, v0, g0 = ATTN_WIDTH, ATTN_WIDTH + KV_WIDTH, ATTN_WIDTH + 2 * KV_WIDTH
    return w_ref.at[:, 0:k0], w_ref.at[:, k0:v0], w_ref.at[:, v0:g0], w_ref.at[:, g0:g0 + ATTN_WIDTH]


def _attn_prompt_kernel(x_ref, w_ref, wo_ref, inv_ref, sgn_ref,
                        sink_ref, lng_ref, lnb_ref, next_wi_ref, next_wo_ref,
                        y_ref, knew_ref, vnew_ref, next_wi_bf_ref, next_wo_bf_ref,
                        qs, klo, khi, vlo, vhi, os_, rope_tab):
    i = pl.program_id(0)
    last = pl.num_programs(0) - 1
    tm = x_ref.shape[0]
    next_wi_bf_ref[...] = next_wi_ref[...].astype(BF16)
    next_wo_bf_ref[...] = next_wo_ref[...].astype(BF16)
    wq_ref, wk_ref, wv_ref, wg_ref = _split_attn_w_in(w_ref)

    @pl.when(i == 0)
    def _():
        for ref in (klo, khi, vlo, vhi):
            ref[0:WIN_ROWS, :] = jnp.zeros((WIN_ROWS, ref.shape[1]), BF16)
        off = lax.broadcasted_iota(jnp.int32, (tm, 1), 0).astype(F32)
        cos_r, sin_r = _rope_tables(off, inv_ref, None)
        rope_tab[0] = cos_r
        rope_tab[1] = sin_r
        rope_tab[2] = cos_r * sgn_ref[...]
        rope_tab[3] = sin_r * sgn_ref[...]

    x = x_ref[...]
    xb = x.astype(BF16)
    start = jnp.full((1, 1), i * tm, jnp.int32).astype(F32)
    cos_s, sin_s = _rope_tables(start, inv_ref, None)
    cosv = cos_s * rope_tab[0] - sin_s * rope_tab[1]
    sinv = sin_s * rope_tab[2] + cos_s * rope_tab[3]

    q = jnp.dot(xb, wq_ref[...], preferred_element_type=F32)
    for j in range(ATTN_WIDTH // LANES):
        sl = slice(j * LANES, (j + 1) * LANES)
        qs[:, sl] = (_rope(q[:, sl], cosv, sinv) * Q_SCALE).astype(BF16)

    k = jnp.dot(xb, wk_ref[...], preferred_element_type=F32)
    v = jnp.dot(xb, wv_ref[...], preferred_element_type=F32)
    new_rows = slice(WIN_ROWS, WIN_ROWS + tm)
    for c in range(KV_WIDTH // LANES):
        sl = slice(c * LANES, (c + 1) * LANES)
        kc = _rope(k[:, sl], cosv, sinv)
        vc = v[:, sl]

        @pl.when(i == last)
        def _():
            knew_ref[sl, :] = kc[tm - WIN_ROWS:, :].T
            vnew_ref[sl, :] = vc[tm - WIN_ROWS:, :].T

        ev = slice(2 * c * LANES, (2 * c + 1) * LANES)
        od = slice((2 * c + 1) * LANES, (2 * c + 2) * LANES)
        klo[new_rows, ev], khi[new_rows, ev], klo[new_rows, od], khi[new_rows, od] = _lo_hi(kc)
        vlo[new_rows, ev], vhi[new_rows, ev], vlo[new_rows, od], vhi[new_rows, od] = _lo_hi(vc)

    row = lax.broadcasted_iota(jnp.int32, (2 * PAIR_ROWS, 1), 0)
    col = lax.broadcasted_iota(jnp.int32, (1, PAIR_KEYS), 1)
    first_col = jnp.where(row % PAIR_ROWS >= CHUNK, CHUNK, 0)
    in_window = (col >= first_col) & (col < first_col + (PAIR_KEYS - CHUNK))
    window_bias = jnp.where(in_window, 0.0, -jnp.inf)

    def pair_body(p, carry):
        r0 = pl.multiple_of(p * PAIR_ROWS, PAIR_ROWS)
        first_key_chunk = i * (tm // CHUNK) + 2 * p - WIN_CHUNKS
        started = col >= jnp.maximum(0, -first_key_chunk) * CHUNK
        bias = window_bias + jnp.where(started, 0.0, -jnp.inf)
        keys = pl.ds(r0, PAIR_KEYS)

        def head_lanes(kv):
            return slice(kv * LANES, (kv + 1) * LANES)

        outs = _attend_heads(
            lambda kv: qs[pl.ds(r0, PAIR_ROWS), kv * 2 * LANES:(kv + 1) * 2 * LANES],
            lambda kv: (klo[keys, head_lanes(kv)], khi[keys, head_lanes(kv)]),
            lambda kv: (vlo[keys, head_lanes(kv)], vhi[keys, head_lanes(kv)]),
            sink_ref, PAIR_ROWS, bias)
        for kv, o in enumerate(outs):
            os_[pl.ds(r0, PAIR_ROWS), kv * 2 * LANES:kv * 2 * LANES + LANES] = o[:PAIR_ROWS]
            os_[pl.ds(r0, PAIR_ROWS), kv * 2 * LANES + LANES:(kv + 1) * 2 * LANES] = o[PAIR_ROWS:]
        return carry

    lax.fori_loop(0, tm // PAIR_ROWS, pair_body, 0)

    for ref in (klo, khi, vlo, vhi):
        ref[0:WIN_ROWS, :] = ref[tm:tm + WIN_ROWS, :]

    g = jnp.dot(xb, wg_ref[...], preferred_element_type=F32)
    z = (os_[...] * _silu(g)).astype(BF16)
    y = jnp.dot(z, wo_ref[...], preferred_element_type=F32)
    y_ref[...] = _layer_norm(DEEPNORM_ALPHA * x + y, lng_ref[...], lnb_ref[...])


def _attn_sample_kernel(x_ref, ck_ref, cv_ref, w_ref, wo_ref,
                        inv_ref, sgn_ref, sink_ref, lng_ref, lnb_ref,
                        y_ref, knew_ref, vnew_ref,
                        qs, ks, vs, os_, *, n_batch, n_new, n_cache):
    rows = n_batch * n_new
    wq_ref, wk_ref, wv_ref, wg_ref = _split_attn_w_in(w_ref)
    x = x_ref[...]
    xb = x.astype(BF16)
    pos = (PAST_LEN + lax.broadcasted_iota(jnp.int32, (rows, 1), 0) % n_new).astype(F32)
    cosv, sinv = _rope_tables(pos, inv_ref, sgn_ref)

    q = jnp.dot(xb, wq_ref[...], preferred_element_type=F32)
    for j in range(ATTN_WIDTH // LANES):
        sl = slice(j * LANES, (j + 1) * LANES)
        qs[:, sl] = (_rope(q[:, sl], cosv, sinv) * Q_SCALE).astype(BF16)
    k = jnp.dot(xb, wk_ref[...], preferred_element_type=F32)
    v = jnp.dot(xb, wv_ref[...], preferred_element_type=F32)
    k_t = jnp.concatenate([_rope(k[:, c * LANES:(c + 1) * LANES], cosv, sinv)
                           for c in range(KV_WIDTH // LANES)], axis=1).T
    v_t = v.T
    for j in range(rows // LANES):
        ks[j] = k_t[:, j * LANES:(j + 1) * LANES]
        vs[j] = v_t[:, j * LANES:(j + 1) * LANES]

    per_tile = LANES // n_new
    lane = lax.broadcasted_iota(jnp.int32, (1, LANES), 1)
    zeros = jnp.zeros((HEAD_DIM, 2 * LANES), BF16)
    dims_nt = (((1,), (1,)), ((), ()))

    def batch_body(b, carry):
        r0 = pl.multiple_of(b * n_new, n_new)
        off = (b % per_tile) * n_new
        k_old, v_old = ck_ref[b], cv_ref[b]
        k_new, v_new = ks[b // per_tile], vs[b // per_tile]
        mine = (lane >= off) & (lane < off + n_new)

        to_end = n_cache - n_new - off
        to_end = jnp.where(to_end < 0, to_end + LANES, to_end)
        for old, new, out_ref in ((k_old, k_new, knew_ref), (v_old, v_new, vnew_ref)):
            out_ref[b] = jnp.where(lane < n_cache - n_new,
                                   pltpu.roll(old, n_cache - n_new, 1), pltpu.roll(new, to_end, 1))

        bias = jnp.concatenate([jnp.zeros((1, n_cache), F32), jnp.where(mine, 0.0, -jnp.inf)], axis=1)
        k_all = jnp.concatenate([k_old, k_new], axis=1).astype(BF16)
        v_all = jnp.concatenate([v_old, jnp.where(mine, v_new, 0.0)], axis=1).astype(BF16)

        scores = []
        for kv in range(N_KV_HEADS):
            q2 = qs[pl.ds(r0, n_new), kv * 2 * LANES:(kv + 1) * 2 * LANES]
            qst = jnp.concatenate([q2[:, :LANES], q2[:, LANES:]], axis=0)
            k_kv = k_all[kv * HEAD_DIM:(kv + 1) * HEAD_DIM, :]
            scores.append((jnp.dot(qst, jnp.concatenate([k_kv, zeros], axis=0), preferred_element_type=F32),
                           jnp.dot(qst, jnp.concatenate([zeros, k_kv], axis=0), preferred_element_type=F32)))
        probs = []
        for kv, (s_a, s_b) in enumerate(scores):
            sink_a, sink_b = _sink_cols(sink_ref, kv, n_new)
            probs.append(_sink_softmax(s_a, sink_a, bias) + _sink_softmax(s_b, sink_b, bias))
        for kv, (p_a, inv_a, p_b, inv_b) in enumerate(probs):
            v_kv = v_all[kv * HEAD_DIM:(kv + 1) * HEAD_DIM, :]
            o_a = lax.dot_general(p_a, jnp.concatenate([v_kv, zeros], axis=0), dims_nt,
                                  preferred_element_type=F32)
            o_b = lax.dot_general(p_b, jnp.concatenate([zeros, v_kv], axis=0), dims_nt,
                                  preferred_element_type=F32)
            o = o_a * inv_a + o_b * inv_b
            os_[pl.ds(r0, n_new), kv * 2 * LANES:kv * 2 * LANES + LANES] = o[:n_new]
            os_[pl.ds(r0, n_new), kv * 2 * LANES + LANES:(kv + 1) * 2 * LANES] = o[n_new:]
        return carry

    lax.fori_loop(0, n_batch, batch_body, 0)

    g = jnp.dot(xb, wg_ref[...], preferred_element_type=F32)
    z = (os_[...] * _silu(g)).astype(BF16)
    y = jnp.dot(z, wo_ref[...], preferred_element_type=F32)
    y_ref[...] = _layer_norm(DEEPNORM_ALPHA * x + y, lng_ref[...], lnb_ref[...])


def _depthwise(u_scr, c_scr, wdw_ref, bdw_ref, base, out_base, n_rows):
    off = CONV_HIST - (CONV_WIDTH - 1)
    for l in range(D_MODEL // LANES):
        sl = slice(l * LANES, (l + 1) * LANES)
        acc = jnp.zeros((n_rows, LANES), F32)
        for j in range(CONV_WIDTH):
            acc = acc + wdw_ref[j:j + 1, sl] * u_scr[l, pl.ds(base + j + off, n_rows, stride=1), :]
        c_scr[pl.ds(out_base, n_rows), sl] = acc + bdw_ref[:, sl]


def _conv_gate(c, g, cg_ref, cb_ref):
    cn = _layer_norm(c, cg_ref[...], cb_ref[...])
    return (_silu(cn) * _silu(g)).astype(BF16)


def _split_conv_w_in(w_ref):
    return tuple(w_ref.at[:, n * D_MODEL:(n + 1) * D_MODEL] for n in range(3))


def _conv_prompt_kernel(x_ref, w_ref, wo_ref, wdw_ref, bdw_ref, cg_ref, cb_ref,
                        lng_ref, lnb_ref,
                        y_ref, state_ref,
                        u_scr, c_scr):
    i = pl.program_id(0)
    last = pl.num_programs(0) - 1
    tm = x_ref.shape[0]
    wa_ref, wb_ref, wg_ref = _split_conv_w_in(w_ref)
    n_tiles = D_MODEL // LANES

    @pl.when(i == 0)
    def _():
        u_scr[:, 0:CONV_HIST, :] = jnp.zeros((n_tiles, CONV_HIST, LANES), F32)

    x = x_ref[...]
    xb = x.astype(BF16)
    a = jnp.dot(xb, wa_ref[...], preferred_element_type=F32)
    b = jnp.dot(xb, wb_ref[...], preferred_element_type=F32)
    u = a * jax.nn.sigmoid(b)
    for l in range(n_tiles):
        u_scr[l, CONV_HIST:CONV_HIST + tm, :] = u[:, l * LANES:(l + 1) * LANES]

    def row_body(rt, carry):
        base = pl.multiple_of(rt * CONV_ROW_TILE, CONV_ROW_TILE)
        _depthwise(u_scr, c_scr, wdw_ref, bdw_ref, base, base, CONV_ROW_TILE)
        return carry

    lax.fori_loop(0, tm // CONV_ROW_TILE, row_body, 0)

    @pl.when(i == last)
    def _():
        for l in range(n_tiles):
            state_ref[:, l * LANES:(l + 1) * LANES] = (
                u_scr[l, CONV_HIST + tm - (CONV_WIDTH - 1):CONV_HIST + tm, :])

    u_scr[:, 0:CONV_HIST, :] = u_scr[:, tm:tm + CONV_HIST, :]

    g = jnp.dot(xb, wg_ref[...], preferred_element_type=F32)
    z = _conv_gate(c_scr[...], g, cg_ref, cb_ref)
    y = jnp.dot(z, wo_ref[...], preferred_element_type=F32)
    y_ref[...] = _layer_norm(DEEPNORM_ALPHA * x + y, lng_ref[...], lnb_ref[...])


def _conv_sample_kernel(x_ref, st_ref, w_ref, wo_ref, wdw_ref, bdw_ref, cg_ref, cb_ref,
                        lng_ref, lnb_ref,
                        y_ref, state_ref,
                        u_scr, c_scr, *, n_batch, n_new):
    slot = CONV_HIST + n_new
    hist = CONV_WIDTH - 1
    wa_ref, wb_ref, wg_ref = _split_conv_w_in(w_ref)
    x = x_ref[...]
    xb = x.astype(BF16)
    a = jnp.dot(xb, wa_ref[...], preferred_element_type=F32)
    b = jnp.dot(xb, wb_ref[...], preferred_element_type=F32)
    u = a * jax.nn.sigmoid(b)
    pad = CONV_HIST - hist
    for bi in range(n_batch):
        base = bi * slot
        for l in range(D_MODEL // LANES):
            sl = slice(l * LANES, (l + 1) * LANES)
            u_scr[l, base:base + pad, :] = jnp.zeros((pad, LANES), F32)
            u_scr[l, base + pad:base + CONV_HIST, :] = st_ref[bi, :, sl]
            u_scr[l, base + CONV_HIST:base + slot, :] = u[bi * n_new:(bi + 1) * n_new, sl]
            state_ref[bi, :, sl] = u_scr[l, base + slot - hist:base + slot, :]

    def batch_body(bi, carry):
        _depthwise(u_scr, c_scr, wdw_ref, bdw_ref, pl.multiple_of(bi * slot, SUBLANES),
                   pl.multiple_of(bi * n_new, n_new), n_new)
        return carry

    lax.fori_loop(0, n_batch, batch_body, 0)

    g = jnp.dot(xb, wg_ref[...], preferred_element_type=F32)
    z = _conv_gate(c_scr[...], g, cg_ref, cb_ref)
    y = jnp.dot(z, wo_ref[...], preferred_element_type=F32)
    y_ref[...] = _layer_norm(DEEPNORM_ALPHA * x + y, lng_ref[...], lnb_ref[...])


def _const_spec(shape):
    return pl.BlockSpec(shape, lambda i: (0,) * len(shape), pipeline_mode=pl.Buffered(1))


def _vmem_spec():
    return pl.BlockSpec(memory_space=pltpu.VMEM)


def _smem_spec():
    return pl.BlockSpec(memory_space=pltpu.SMEM)


def _rope_lane_consts():
    half = ROT_DIM // 2
    inv = ROPE_THETA ** (-jnp.arange(half, dtype=F32) * (2.0 / ROT_DIM))
    zeros = jnp.zeros((HEAD_DIM - ROT_DIM,), F32)
    inv_head = jnp.concatenate([inv, inv, zeros])
    sgn_head = jnp.concatenate([-jnp.ones((half,), F32), jnp.ones((half,), F32), zeros])
    reps = LANES // HEAD_DIM
    return jnp.tile(inv_head, reps)[None, :], jnp.tile(sgn_head, reps)[None, :]


def _attn_layer(xp, xs, ck, cv, w_in, sink, w_out, ln_g, ln_b, next_w_in, next_w_out):
    seq = xp.shape[0]
    n_batch, n_cache = ck.shape[0], ck.shape[2]
    n_new = xs.shape[0] // n_batch
    assert n_cache == LANES and LANES % n_new == 0 and xs.shape[0] % LANES == 0
    wi = w_in.astype(BF16)
    wo = w_out.astype(BF16)
    inv, sgn = _rope_lane_consts()
    lng, lnb = ln_g[None, :], ln_b[None, :]
    params = pltpu.CompilerParams(dimension_semantics=("arbitrary",), vmem_limit_bytes=VMEM_LIMIT_BYTES)

    tm = ROW_BLOCK
    n_steps = seq // tm
    w_rows = next_w_in.shape[0] // n_steps
    assert next_w_in.shape[0] == next_w_out.shape[0] == w_rows * n_steps and w_rows % (2 * SUBLANES) == 0
    kv_scr = pltpu.VMEM((WIN_ROWS + tm, N_KV_HEADS * LANES), BF16)
    yp, kp, vp, next_wi, next_wo = pl.pallas_call(
        _attn_prompt_kernel,
        grid=(n_steps,),
        in_specs=[pl.BlockSpec((tm, D_MODEL), lambda i: (i, 0)),
                  _const_spec(wi.shape), _const_spec(wo.shape),
                  _const_spec(inv.shape), _const_spec(sgn.shape), _smem_spec(),
                  _const_spec(lng.shape), _const_spec(lnb.shape),
                  pl.BlockSpec((w_rows, next_w_in.shape[1]), lambda i: (i, 0)),
                  pl.BlockSpec((w_rows, next_w_out.shape[1]), lambda i: (i, 0))],
        out_specs=[pl.BlockSpec((tm, D_MODEL), lambda i: (i, 0)),
                   pl.BlockSpec((KV_WIDTH, WIN_ROWS), lambda i: (0, 0)),
                   pl.BlockSpec((KV_WIDTH, WIN_ROWS), lambda i: (0, 0)),
                   pl.BlockSpec((w_rows, next_w_in.shape[1]), lambda i: (i, 0)),
                   pl.BlockSpec((w_rows, next_w_out.shape[1]), lambda i: (i, 0))],
        out_shape=[jax.ShapeDtypeStruct((seq, D_MODEL), F32),
                   jax.ShapeDtypeStruct((KV_WIDTH, WIN_ROWS), F32),
                   jax.ShapeDtypeStruct((KV_WIDTH, WIN_ROWS), F32),
                   jax.ShapeDtypeStruct(next_w_in.shape, BF16),
                   jax.ShapeDtypeStruct(next_w_out.shape, BF16)],
        scratch_shapes=[pltpu.VMEM((tm, ATTN_WIDTH), BF16), kv_scr, kv_scr, kv_scr, kv_scr,
                        pltpu.VMEM((tm, ATTN_WIDTH), F32), pltpu.VMEM((4, tm, LANES), F32)],
        compiler_params=params,
        name="attn_prompt",
    )(xp, wi, wo, inv, sgn, sink, lng, lnb, next_w_in, next_w_out)

    rows = xs.shape[0]
    new_t = pltpu.VMEM((rows // LANES, KV_WIDTH, LANES), F32)
    ys, ksn, vsn = pl.pallas_call(
        functools.partial(_attn_sample_kernel, n_batch=n_batch, n_new=n_new, n_cache=n_cache),
        in_specs=[_vmem_spec()] * 7 + [_smem_spec(), _vmem_spec(), _vmem_spec()],
        out_specs=[_vmem_spec()] * 3,
        out_shape=[jax.ShapeDtypeStruct((rows, D_MODEL), F32),
                   jax.ShapeDtypeStruct(ck.shape, F32),
                   jax.ShapeDtypeStruct(cv.shape, F32)],
        scratch_shapes=[pltpu.VMEM((rows, ATTN_WIDTH), BF16), new_t, new_t,
                        pltpu.VMEM((rows, ATTN_WIDTH), F32)],
        compiler_params=pltpu.CompilerParams(vmem_limit_bytes=VMEM_LIMIT_BYTES),
        name="attn_sample",
    )(xs, ck, cv, wi, wo, inv, sgn, sink, lng, lnb)
    return yp, ys, kp, vp, ksn, vsn, next_wi, next_wo


def _conv_layer(xp, xs, state, wi, w_dw, b_dw, cg, cb, wo, ln_g, ln_b):
    seq = xp.shape[0]
    n_batch = state.shape[0]
    n_new = xs.shape[0] // n_batch
    hist = CONV_WIDTH - 1
    vecs = [b_dw[None, :], cg[None, :], cb[None, :], ln_g[None, :], ln_b[None, :]]

    tm = ROW_BLOCK
    yp, cp = pl.pallas_call(
        _conv_prompt_kernel,
        grid=(seq // tm,),
        in_specs=[pl.BlockSpec((tm, D_MODEL), lambda i: (i, 0)),
                  _const_spec(wi.shape), _const_spec(wo.shape), _const_spec(w_dw.shape)]
                 + [_const_spec(t.shape) for t in vecs],
        out_specs=[pl.BlockSpec((tm, D_MODEL), lambda i: (i, 0)),
                   pl.BlockSpec((hist, D_MODEL), lambda i: (0, 0))],
        out_shape=[jax.ShapeDtypeStruct((seq, D_MODEL), F32),
                   jax.ShapeDtypeStruct((hist, D_MODEL), F32)],
        scratch_shapes=[pltpu.VMEM((D_MODEL // LANES, CONV_HIST + tm, LANES), F32),
                        pltpu.VMEM((tm, D_MODEL), F32)],
        compiler_params=pltpu.CompilerParams(dimension_semantics=("arbitrary",),
                                             vmem_limit_bytes=VMEM_LIMIT_BYTES),
        name="conv_prompt",
    )(xp, wi, wo, w_dw, *vecs)

    rows = xs.shape[0]
    ys, cs = pl.pallas_call(
        functools.partial(_conv_sample_kernel, n_batch=n_batch, n_new=n_new),
        in_specs=[_vmem_spec()] * 10,
        out_specs=[_vmem_spec()] * 2,
        out_shape=[jax.ShapeDtypeStruct((rows, D_MODEL), F32),
                   jax.ShapeDtypeStruct((n_batch, hist, D_MODEL), F32)],
        scratch_shapes=[pltpu.VMEM((D_MODEL // LANES, n_batch * (CONV_HIST + n_new), LANES), F32),
                        pltpu.VMEM((rows, D_MODEL), F32)],
        compiler_params=pltpu.CompilerParams(vmem_limit_bytes=VMEM_LIMIT_BYTES),
        name="conv_sample",
    )(xs, state, wi, wo, w_dw, *vecs)
    return yp, ys, cp, cs


def kernel(x_prompt, x_sample, cache_k, cache_v, state_conv, attn_w_in, attn_sink, attn_w_out,
           conv_w_in, conv_w_dw, conv_b_dw, conv_ln_g, conv_ln_b, conv_w_out, post_ln_g, post_ln_b):
    batch, seq, _ = x_prompt.shape
    dec_batch, dec_seq, _ = x_sample.shape
    wc = cache_k.shape[2]
    assert batch == 1 and wc == WIN_ROWS and seq % ROW_BLOCK == 0
    xp = x_prompt.reshape(seq, D_MODEL)
    xs = x_sample.reshape(dec_batch * dec_seq, D_MODEL)

    def feature_major(c):
        return jnp.transpose(c, (0, 2, 3, 1)).reshape(c.shape[0], KV_WIDTH, wc)

    def frame_major(c_t, n):
        return jnp.transpose(c_t.reshape(n, N_KV_HEADS, HEAD_DIM, wc), (0, 3, 1, 2))[None]

    xp, xs, kp, vp, ksn, vsn, conv_wi, conv_wo = _attn_layer(
        xp, xs, feature_major(cache_k[0]), feature_major(cache_v[0]),
        attn_w_in[0], attn_sink[0], attn_w_out[0], post_ln_g[0], post_ln_b[0],
        conv_w_in[0], conv_w_out[0])
    xp, xs, cp, cs = _conv_layer(
        xp, xs, state_conv[0], conv_wi, conv_w_dw[0], conv_b_dw[0], conv_ln_g[0], conv_ln_b[0],
        conv_wo, post_ln_g[1], post_ln_b[1])

    return (xp.reshape(batch, seq, D_MODEL), xs.reshape(dec_batch, dec_seq, D_MODEL),
            frame_major(kp, 1), frame_major(vp, 1),
            frame_major(ksn, dec_batch), frame_major(vsn, dec_batch),
            cp[None, None], cs[None])
```

```python
import functools

import jax
import jax.numpy as jnp
from jax import lax
from jax.experimental import pallas as pl
from jax.experimental.pallas import tpu as pltpu

F32 = jnp.float32
BF16 = jnp.bfloat16

D_MODEL = 2048
DEPTH = 2
PAST_LEN = 1024
CHUNK = 64
HEAD_DIM = 64
N_HEADS = D_MODEL // HEAD_DIM
N_KV_HEADS = 8
GROUP = N_HEADS // N_KV_HEADS
ATTN_WIDTH = N_HEADS * HEAD_DIM
KV_WIDTH = N_KV_HEADS * HEAD_DIM
WIN_CHUNKS = 2
WIN_ROWS = WIN_CHUNKS * CHUNK
ROT_DIM = HEAD_DIM // 4
ROPE_THETA = 500000.0
CONV_WIDTH = 31
LN_EPS = 1e-5
DEEPNORM_ALPHA = (2.0 * DEPTH) ** 0.25
ATTN_SCALE = HEAD_DIM ** -0.5
LOG2E = 1.4426950408889634
Q_SCALE = ATTN_SCALE * LOG2E

LANES = 128
SUBLANES = 8
VMEM_LIMIT_BYTES = 60 * 1024 * 1024
ROW_BLOCK = 256
CONV_HIST = 32
CONV_ROW_TILE = 32
PAIR_ROWS = 2 * CHUNK
PAIR_KEYS = WIN_ROWS + PAIR_ROWS


def _layer_norm(r, g, b):
    mu = jnp.mean(r, axis=-1, keepdims=True)
    d = r - mu
    var = jnp.mean(d * d, axis=-1, keepdims=True)
    return d * lax.rsqrt(var + LN_EPS) * g + b


def _silu(t):
    return t * jax.nn.sigmoid(t)


def _rope_tables(pos, inv_ref, sgn_ref):
    ang = pos * inv_ref[...]
    sin = jnp.sin(ang)
    return jnp.cos(ang), sin if sgn_ref is None else sin * sgn_ref[...]


def _rope(t, cosv, sinv):
    lane = lax.broadcasted_iota(jnp.int32, (1, LANES), 1) % HEAD_DIM
    partner = jnp.where(lane < ROT_DIM // 2,
                        pltpu.roll(t, LANES - ROT_DIM // 2, 1),
                        pltpu.roll(t, ROT_DIM // 2, 1))
    return t * cosv + partner * sinv


def _lo_hi(col):
    low = lax.broadcasted_iota(jnp.int32, (1, LANES), 1) < HEAD_DIM
    swapped = pltpu.roll(col, HEAD_DIM, 1)
    zero = jnp.zeros_like(col)
    even_lo = jnp.where(low, col, zero).astype(BF16)
    even_hi = jnp.where(low, zero, swapped).astype(BF16)
    odd_lo = jnp.where(low, swapped, zero).astype(BF16)
    odd_hi = jnp.where(low, zero, col).astype(BF16)
    return even_lo, even_hi, odd_lo, odd_hi


def _scores(q2, k_lo, k_hi):
    qst = jnp.concatenate([q2[:, :LANES], q2[:, LANES:]], axis=0)
    dims = (((1,), (1,)), ((), ()))
    return (lax.dot_general(qst, k_lo, dims, preferred_element_type=F32),
            lax.dot_general(qst, k_hi, dims, preferred_element_type=F32))


def _sink_softmax(s, sk, bias):
    if bias is not None:
        s = s + bias
    m = jnp.maximum(jnp.max(s, axis=-1, keepdims=True), sk)
    p = jnp.exp2(s - m)
    denom = jnp.sum(p, axis=-1, keepdims=True) + jnp.exp2(sk - m)
    return p.astype(BF16), 1.0 / denom


def _weighted_values(p_a, inv_a, p_b, inv_b, v_lo, v_hi):
    o_a = jnp.dot(p_a, v_lo, preferred_element_type=F32)
    o_b = jnp.dot(p_b, v_hi, preferred_element_type=F32)
    return o_a * inv_a + o_b * inv_b


def _attend_heads(q_of, k_of, v_of, sink_ref, q_rows, bias):
    scores = [_scores(q_of(kv), *k_of(kv)) for kv in range(N_KV_HEADS)]
    probs = []
    for kv, (s_a, s_b) in enumerate(scores):
        sink_a, sink_b = _sink_cols(sink_ref, kv, q_rows)
        probs.append(_sink_softmax(s_a, sink_a, bias) + _sink_softmax(s_b, sink_b, bias))
    return [_weighted_values(*probs[kv], *v_of(kv)) for kv in range(N_KV_HEADS)]


def _sink_cols(sink_ref, kv, q_rows):
    row = lax.broadcasted_iota(jnp.int32, (2 * q_rows, 1), 0)
    first = row < q_rows
    h = kv * GROUP
    return (jnp.where(first, sink_ref[h + 0] * LOG2E, sink_ref[h + 2] * LOG2E),
            jnp.where(first, sink_ref[h + 1] * LOG2E, sink_ref[h + 3] * LOG2E))


def _split_attn_w_in(w_ref):
    k0, v0, g0 = ATTN_WIDTH, ATTN_WIDTH + KV_WIDTH, ATTN_WIDTH + 2 * KV_WIDTH
    return w_ref.at[:, 0:k0], w_ref.at[:, k0:v0], w_ref.at[:, v0:g0], w_ref.at[:, g0:g0 + ATTN_WIDTH]


def _attn_prompt_kernel(x_ref, w_ref, wo_ref, inv_ref, sgn_ref,
                        sink_ref, lng_ref, lnb_ref, next_wi_ref, next_wo_ref,
                        y_ref, knew_ref, vnew_ref, next_wi_bf_ref, next_wo_bf_ref,
                        qs, klo, khi, vlo, vhi, os_, rope_tab):
    i = pl.program_id(0)
    last = pl.num_programs(0) - 1
    tm = x_ref.shape[0]
    next_wi_bf_ref[...] = next_wi_ref[...].astype(BF16)
    next_wo_bf_ref[...] = next_wo_ref[...].astype(BF16)
    wq_ref, wk_ref, wv_ref, wg_ref = _split_attn_w_in(w_ref)

    @pl.when(i == 0)
    def _():
        for ref in (klo, khi, vlo, vhi):
            ref[0:WIN_ROWS, :] = jnp.zeros((WIN_ROWS, ref.shape[1]), BF16)
        off = lax.broadcasted_iota(jnp.int32, (tm, 1), 0).astype(F32)
        cos_r, sin_r = _rope_tables(off, inv_ref, None)
        rope_tab[0] = cos_r
        rope_tab[1] = sin_r
        rope_tab[2] = cos_r * sgn_ref[...]
        rope_tab[3] = sin_r * sgn_ref[...]

    x = x_ref[...]
    xb = x.astype(BF16)
    start = jnp.full((1, 1), i * tm, jnp.int32).astype(F32)
    cos_s, sin_s = _rope_tables(start, inv_ref, None)
    cosv = cos_s * rope_tab[0] - sin_s * rope_tab[1]
    sinv = sin_s * rope_tab[2] + cos_s * rope_tab[3]

    q = jnp.dot(xb, wq_ref[...], preferred_element_type=F32)
    for j in range(ATTN_WIDTH // LANES):
        sl = slice(j * LANES, (j + 1) * LANES)
        qs[:, sl] = (_rope(q[:, sl], cosv, sinv) * Q_SCALE).astype(BF16)

    k = jnp.dot(xb, wk_ref[...], preferred_element_type=F32)
    v = jnp.dot(xb, wv_ref[...], preferred_element_type=F32)
    new_rows = slice(WIN_ROWS, WIN_ROWS + tm)
    for c in range(KV_WIDTH // LANES):
        sl = slice(c * LANES, (c + 1) * LANES)
        kc = _rope(k[:, sl], cosv, sinv)
        vc = v[:, sl]

        @pl.when(i == last)
        def _():
            knew_ref[sl, :] = kc[tm - WIN_ROWS:, :].T
            vnew_ref[sl, :] = vc[tm - WIN_ROWS:, :].T

        ev = slice(2 * c * LANES, (2 * c + 1) * LANES)
        od = slice((2 * c + 1) * LANES, (2 * c + 2) * LANES)
        klo[new_rows, ev], khi[new_rows, ev], klo[new_rows, od], khi[new_rows, od] = _lo_hi(kc)
        vlo[new_rows, ev], vhi[new_rows, ev], vlo[new_rows, od], vhi[new_rows, od] = _lo_hi(vc)

    row = lax.broadcasted_iota(jnp.int32, (2 * PAIR_ROWS, 1), 0)
    col = lax.broadcasted_iota(jnp.int32, (1, PAIR_KEYS), 1)
    first_col = jnp.where(row % PAIR_ROWS >= CHUNK, CHUNK, 0)
    in_window = (col >= first_col) & (col < first_col + (PAIR_KEYS - CHUNK))
    window_bias = jnp.where(in_window, 0.0, -jnp.inf)

    def pair_body(p, carry):
        r0 = pl.multiple_of(p * PAIR_ROWS, PAIR_ROWS)
        first_key_chunk = i * (tm // CHUNK) + 2 * p - WIN_CHUNKS
        started = col >= jnp.maximum(0, -first_key_chunk) * CHUNK
        bias = window_bias + jnp.where(started, 0.0, -jnp.inf)
        keys = pl.ds(r0, PAIR_KEYS)

        def head_lanes(kv):
            return slice(kv * LANES, (kv + 1) * LANES)

        outs = _attend_heads(
            lambda kv: qs[pl.ds(r0, PAIR_ROWS), kv * 2 * LANES:(kv + 1) * 2 * LANES],
            lambda kv: (klo[keys, head_lanes(kv)], khi[keys, head_lanes(kv)]),
            lambda kv: (vlo[keys, head_lanes(kv)], vhi[keys, head_lanes(kv)]),
            sink_ref, PAIR_ROWS, bias)
        for kv, o in enumerate(outs):
            os_[pl.ds(r0, PAIR_ROWS), kv * 2 * LANES:kv * 2 * LANES + LANES] = o[:PAIR_ROWS]
            os_[pl.ds(r0, PAIR_ROWS), kv * 2 * LANES + LANES:(kv + 1) * 2 * LANES] = o[PAIR_ROWS:]
        return carry

    lax.fori_loop(0, tm // PAIR_ROWS, pair_body, 0)

    for ref in (klo, khi, vlo, vhi):
        ref[0:WIN_ROWS, :] = ref[tm:tm + WIN_ROWS, :]

    g = jnp.dot(xb, wg_ref[...], preferred_element_type=F32)
    z = (os_[...] * _silu(g)).astype(BF16)
    y = jnp.dot(z, wo_ref[...], preferred_element_type=F32)
    y_ref[...] = _layer_norm(DEEPNORM_ALPHA * x + y, lng_ref[...], lnb_ref[...])


def _attn_sample_kernel(xk_ref, x_ref, ck_ref, cv_ref, w_ref, wo_ref,
                        inv_ref, sgn_ref, sink_ref, lng_ref, lnb_ref,
                        y_ref, knew_ref, vnew_ref, w_bf_ref, wo_bf_ref,
                        h_acc, wo_all, qs, ks, vs, os_, *, n_batch, n_new, n_cache):
    kstep = pl.program_id(0)
    w_bf = w_ref[...].astype(BF16)
    w_bf_ref[...] = w_bf
    wo_bf = wo_ref[...].astype(BF16)
    wo_bf_ref[...] = wo_bf
    k_rows = wo_ref.shape[0]
    wo_all[pl.ds(pl.multiple_of(kstep * k_rows, k_rows), k_rows), :] = wo_bf
    part = jnp.dot(xk_ref[...].astype(BF16), w_bf, preferred_element_type=F32)

    @pl.when(kstep == 0)
    def _():
        h_acc[...] = part

    @pl.when(kstep > 0)
    def _():
        h_acc[...] += part

    @pl.when(kstep == pl.num_programs(0) - 1)
    def _():
        _attn_sample_tail(x_ref, ck_ref, cv_ref, h_acc, wo_all, inv_ref, sgn_ref, sink_ref, lng_ref, lnb_ref,
                          y_ref, knew_ref, vnew_ref, qs, ks, vs, os_,
                          n_batch=n_batch, n_new=n_new, n_cache=n_cache)


def _attn_sample_tail(x_ref, ck_ref, cv_ref, h_ref, wo_ref, inv_ref, sgn_ref, sink_ref, lng_ref, lnb_ref,
                      y_ref, knew_ref, vnew_ref, qs, ks, vs, os_, *, n_batch, n_new, n_cache):
    rows = n_batch * n_new
    hq_ref, hk_ref, hv_ref, hg_ref = _split_attn_w_in(h_ref)
    x = x_ref[...]
    pos = (PAST_LEN + lax.broadcasted_iota(jnp.int32, (rows, 1), 0) % n_new).astype(F32)
    cosv, sinv = _rope_tables(pos, inv_ref, sgn_ref)

    for j in range(ATTN_WIDTH // LANES):
        sl = slice(j * LANES, (j + 1) * LANES)
        qs[:, sl] = (_rope(hq_ref[:, sl], cosv, sinv) * Q_SCALE).astype(BF16)
    k = hk_ref[...]
    v = hv_ref[...]
    k_t = jnp.concatenate([_rope(k[:, c * LANES:(c + 1) * LANES], cosv, sinv)
                           for c in range(KV_WIDTH // LANES)], axis=1).T
    v_t = v.T
    for j in range(rows // LANES):
        ks[j] = k_t[:, j * LANES:(j + 1) * LANES]
        vs[j] = v_t[:, j * LANES:(j + 1) * LANES]

    per_tile = LANES // n_new
    lane = lax.broadcasted_iota(jnp.int32, (1, LANES), 1)
    zeros = jnp.zeros((HEAD_DIM, 2 * LANES), BF16)
    dims_nt = (((1,), (1,)), ((), ()))

    def batch_body(b, carry):
        r0 = pl.multiple_of(b * n_new, n_new)
        off = (b % per_tile) * n_new
        k_old, v_old = ck_ref[b], cv_ref[b]
        k_new, v_new = ks[b // per_tile], vs[b // per_tile]
        mine = (lane >= off) & (lane < off + n_new)

        to_end = n_cache - n_new - off
        to_end = jnp.where(to_end < 0, to_end + LANES, to_end)
        for old, new, out_ref in ((k_old, k_new, knew_ref), (v_old, v_new, vnew_ref)):
            out_ref[b] = jnp.where(lane < n_cache - n_new,
                                   pltpu.roll(old, n_cache - n_new, 1), pltpu.roll(new, to_end, 1))

        bias = jnp.concatenate([jnp.zeros((1, n_cache), F32), jnp.where(mine, 0.0, -jnp.inf)], axis=1)
        k_all = jnp.concatenate([k_old, k_new], axis=1).astype(BF16)
        v_all = jnp.concatenate([v_old, jnp.where(mine, v_new, 0.0)], axis=1).astype(BF16)

        scores = []
        for kv in range(N_KV_HEADS):
            q2 = qs[pl.ds(r0, n_new), kv * 2 * LANES:(kv + 1) * 2 * LANES]
            qst = jnp.concatenate([q2[:, :LANES], q2[:, LANES:]], axis=0)
            k_kv = k_all[kv * HEAD_DIM:(kv + 1) * HEAD_DIM, :]
            scores.append((jnp.dot(qst, jnp.concatenate([k_kv, zeros], axis=0), preferred_element_type=F32),
                           jnp.dot(qst, jnp.concatenate([zeros, k_kv], axis=0), preferred_element_type=F32)))
        probs = []
        for kv, (s_a, s_b) in enumerate(scores):
            sink_a, sink_b = _sink_cols(sink_ref, kv, n_new)
            probs.append(_sink_softmax(s_a, sink_a, bias) + _sink_softmax(s_b, sink_b, bias))
        for kv, (p_a, inv_a, p_b, inv_b) in enumerate(probs):
            v_kv = v_all[kv * HEAD_DIM:(kv + 1) * HEAD_DIM, :]
            o_a = lax.dot_general(p_a, jnp.concatenate([v_kv, zeros], axis=0), dims_nt,
                                  preferred_element_type=F32)
            o_b = lax.dot_general(p_b, jnp.concatenate([zeros, v_kv], axis=0), dims_nt,
                                  preferred_element_type=F32)
            o = o_a * inv_a + o_b * inv_b
            os_[pl.ds(r0, n_new), kv * 2 * LANES:kv * 2 * LANES + LANES] = o[:n_new]
            os_[pl.ds(r0, n_new), kv * 2 * LANES + LANES:(kv + 1) * 2 * LANES] = o[n_new:]
        return carry

    lax.fori_loop(0, n_batch, batch_body, 0)

    z = (os_[...] * _silu(hg_ref[...])).astype(BF16)
    y = jnp.dot(z, wo_ref[...], preferred_element_type=F32)
    y_ref[...] = _layer_norm(DEEPNORM_ALPHA * x + y, lng_ref[...], lnb_ref[...])


def _depthwise(u_scr, c_scr, wdw_ref, bdw_ref, base, out_base, n_rows):
    off = CONV_HIST - (CONV_WIDTH - 1)
    for l in range(D_MODEL // LANES):
        sl = slice(l * LANES, (l + 1) * LANES)
        acc = jnp.zeros((n_rows, LANES), F32)
        for j in range(CONV_WIDTH):
            acc = acc + wdw_ref[j:j + 1, sl] * u_scr[l, pl.ds(base + j + off, n_rows, stride=1), :]
        c_scr[pl.ds(out_base, n_rows), sl] = acc + bdw_ref[:, sl]


def _conv_gate(c, g, cg_ref, cb_ref):
    cn = _layer_norm(c, cg_ref[...], cb_ref[...])
    return (_silu(cn) * _silu(g)).astype(BF16)


def _split_conv_w_in(w_ref):
    return tuple(w_ref.at[:, n * D_MODEL:(n + 1) * D_MODEL] for n in range(3))


def _conv_prompt_kernel(x_ref, w_ref, wo_ref, wdw_ref, bdw_ref, cg_ref, cb_ref,
                        lng_ref, lnb_ref,
                        y_ref, state_ref,
                        u_scr, c_scr):
    i = pl.program_id(0)
    last = pl.num_programs(0) - 1
    tm = x_ref.shape[0]
    wa_ref, wb_ref, wg_ref = _split_conv_w_in(w_ref)
    n_tiles = D_MODEL // LANES

    @pl.when(i == 0)
    def _():
        u_scr[:, 0:CONV_HIST, :] = jnp.zeros((n_tiles, CONV_HIST, LANES), F32)

    x = x_ref[...]
    xb = x.astype(BF16)
    a = jnp.dot(xb, wa_ref[...], preferred_element_type=F32)
    b = jnp.dot(xb, wb_ref[...], preferred_element_type=F32)
    u = a * jax.nn.sigmoid(b)
    for l in range(n_tiles):
        u_scr[l, CONV_HIST:CONV_HIST + tm, :] = u[:, l * LANES:(l + 1) * LANES]

    def row_body(rt, carry):
        base = pl.multiple_of(rt * CONV_ROW_TILE, CONV_ROW_TILE)
        _depthwise(u_scr, c_scr, wdw_ref, bdw_ref, base, base, CONV_ROW_TILE)
        return carry

    lax.fori_loop(0, tm // CONV_ROW_TILE, row_body, 0)

    @pl.when(i == last)
    def _():
        for l in range(n_tiles):
            state_ref[:, l * LANES:(l + 1) * LANES] = (
                u_scr[l, CONV_HIST + tm - (CONV_WIDTH - 1):CONV_HIST + tm, :])

    u_scr[:, 0:CONV_HIST, :] = u_scr[:, tm:tm + CONV_HIST, :]

    g = jnp.dot(xb, wg_ref[...], preferred_element_type=F32)
    z = _conv_gate(c_scr[...], g, cg_ref, cb_ref)
    y = jnp.dot(z, wo_ref[...], preferred_element_type=F32)
    y_ref[...] = _layer_norm(DEEPNORM_ALPHA * x + y, lng_ref[...], lnb_ref[...])


def _conv_sample_kernel(x_ref, st_ref, w_ref, wo_ref, wdw_ref, bdw_ref, cg_ref, cb_ref,
                        lng_ref, lnb_ref,
                        y_ref, state_ref,
                        u_scr, c_scr, *, n_batch, n_new):
    slot = CONV_HIST + n_new
    hist = CONV_WIDTH - 1
    wa_ref, wb_ref, wg_ref = _split_conv_w_in(w_ref)
    x = x_ref[...]
    xb = x.astype(BF16)
    a = jnp.dot(xb, wa_ref[...], preferred_element_type=F32)
    b = jnp.dot(xb, wb_ref[...], preferred_element_type=F32)
    u = a * jax.nn.sigmoid(b)
    pad = CONV_HIST - hist
    for bi in range(n_batch):
        base = bi * slot
        for l in range(D_MODEL // LANES):
            sl = slice(l * LANES, (l + 1) * LANES)
            u_scr[l, base:base + pad, :] = jnp.zeros((pad, LANES), F32)
            u_scr[l, base + pad:base + CONV_HIST, :] = st_ref[bi, :, sl]
            u_scr[l, base + CONV_HIST:base + slot, :] = u[bi * n_new:(bi + 1) * n_new, sl]
            state_ref[bi, :, sl] = u_scr[l, base + slot - hist:base + slot, :]

    def batch_body(bi, carry):
        _depthwise(u_scr, c_scr, wdw_ref, bdw_ref, pl.multiple_of(bi * slot, SUBLANES),
                   pl.multiple_of(bi * n_new, n_new), n_new)
        return carry

    lax.fori_loop(0, n_batch, batch_body, 0)

    g = jnp.dot(xb, wg_ref[...], preferred_element_type=F32)
    z = _conv_gate(c_scr[...], g, cg_ref, cb_ref)
    y = jnp.dot(z, wo_ref[...], preferred_element_type=F32)
    y_ref[...] = _layer_norm(DEEPNORM_ALPHA * x + y, lng_ref[...], lnb_ref[...])


def _const_spec(shape):
    return pl.BlockSpec(shape, lambda i: (0,) * len(shape), pipeline_mode=pl.Buffered(1))


def _const_out_spec(shape):
    return pl.BlockSpec(shape, lambda i: (0,) * len(shape), pipeline_mode=pl.Buffered(1))


def _vmem_spec():
    return pl.BlockSpec(memory_space=pltpu.VMEM)


def _smem_spec():
    return pl.BlockSpec(memory_space=pltpu.SMEM)


def _rope_lane_consts():
    half = ROT_DIM // 2
    inv = ROPE_THETA ** (-jnp.arange(half, dtype=F32) * (2.0 / ROT_DIM))
    zeros = jnp.zeros((HEAD_DIM - ROT_DIM,), F32)
    inv_head = jnp.concatenate([inv, inv, zeros])
    sgn_head = jnp.concatenate([-jnp.ones((half,), F32), jnp.ones((half,), F32), zeros])
    reps = LANES // HEAD_DIM
    return jnp.tile(inv_head, reps)[None, :], jnp.tile(sgn_head, reps)[None, :]


def _attn_layer(xp, xs, ck, cv, w_in, sink, w_out, ln_g, ln_b, next_w_in, next_w_out):
    seq = xp.shape[0]
    n_batch, n_cache = ck.shape[0], ck.shape[2]
    n_new = xs.shape[0] // n_batch
    assert n_cache == LANES and LANES % n_new == 0 and xs.shape[0] % LANES == 0
    inv, sgn = _rope_lane_consts()
    lng, lnb = ln_g[None, :], ln_b[None, :]
    params = pltpu.CompilerParams(dimension_semantics=("arbitrary",), vmem_limit_bytes=VMEM_LIMIT_BYTES)

    rows = xs.shape[0]
    k_rows = LANES
    new_t = pltpu.VMEM((rows // LANES, KV_WIDTH, LANES), F32)
    ys, ksn, vsn, wi, wo = pl.pallas_call(
        functools.partial(_attn_sample_kernel, n_batch=n_batch, n_new=n_new, n_cache=n_cache),
        grid=(D_MODEL // k_rows,),
        in_specs=[pl.BlockSpec((rows, k_rows), lambda k: (0, k)),
                  _const_spec(xs.shape), _const_spec(ck.shape), _const_spec(cv.shape),
                  pl.BlockSpec((k_rows, w_in.shape[1]), lambda k: (k, 0)),
                  pl.BlockSpec((k_rows, w_out.shape[1]), lambda k: (k, 0)),
                  _const_spec(inv.shape), _const_spec(sgn.shape), _smem_spec(),
                  _const_spec(lng.shape), _const_spec(lnb.shape)],
        out_specs=[_const_out_spec((rows, D_MODEL)), _const_out_spec(ck.shape), _const_out_spec(cv.shape),
                   pl.BlockSpec((k_rows, w_in.shape[1]), lambda k: (k, 0)),
                   pl.BlockSpec((k_rows, w_out.shape[1]), lambda k: (k, 0))],
        out_shape=[jax.ShapeDtypeStruct((rows, D_MODEL), F32),
                   jax.ShapeDtypeStruct(ck.shape, F32),
                   jax.ShapeDtypeStruct(cv.shape, F32),
                   jax.ShapeDtypeStruct(w_in.shape, BF16),
                   jax.ShapeDtypeStruct(w_out.shape, BF16)],
        scratch_shapes=[pltpu.VMEM((rows, w_in.shape[1]), F32), pltpu.VMEM(w_out.shape, BF16),
                        pltpu.VMEM((rows, ATTN_WIDTH), BF16), new_t, new_t,
                        pltpu.VMEM((rows, ATTN_WIDTH), F32)],
        compiler_params=params,
        name="attn_sample",
    )(xs, xs, ck, cv, w_in, w_out, inv, sgn, sink, lng, lnb)

    tm = ROW_BLOCK
    n_steps = seq // tm
    w_rows = next_w_in.shape[0] // n_steps
    assert next_w_in.shape[0] == next_w_out.shape[0] == w_rows * n_steps and w_rows % (2 * SUBLANES) == 0
    kv_scr = pltpu.VMEM((WIN_ROWS + tm, N_KV_HEADS * LANES), BF16)
    yp, kp, vp, next_wi, next_wo = pl.pallas_call(
        _attn_prompt_kernel,
        grid=(n_steps,),
        in_specs=[pl.BlockSpec((tm, D_MODEL), lambda i: (i, 0)),
                  _const_spec(wi.shape), _const_spec(wo.shape),
                  _const_spec(inv.shape), _const_spec(sgn.shape), _smem_spec(),
                  _const_spec(lng.shape), _const_spec(lnb.shape),
                  pl.BlockSpec((w_rows, next_w_in.shape[1]), lambda i: (i, 0)),
                  pl.BlockSpec((w_rows, next_w_out.shape[1]), lambda i: (i, 0))],
        out_specs=[pl.BlockSpec((tm, D_MODEL), lambda i: (i, 0)),
                   pl.BlockSpec((KV_WIDTH, WIN_ROWS), lambda i: (0, 0)),
                   pl.BlockSpec((KV_WIDTH, WIN_ROWS), lambda i: (0, 0)),
                   pl.BlockSpec((w_rows, next_w_in.shape[1]), lambda i: (i, 0)),
                   pl.BlockSpec((w_rows, next_w_out.shape[1]), lambda i: (i, 0))],
        out_shape=[jax.ShapeDtypeStruct((seq, D_MODEL), F32),
                   jax.ShapeDtypeStruct((KV_WIDTH, WIN_ROWS), F32),
                   jax.ShapeDtypeStruct((KV_WIDTH, WIN_ROWS), F32),
                   jax.ShapeDtypeStruct(next_w_in.shape, BF16),
                   jax.ShapeDtypeStruct(next_w_out.shape, BF16)],
        scratch_shapes=[pltpu.VMEM((tm, ATTN_WIDTH), BF16), kv_scr, kv_scr, kv_scr, kv_scr,
                        pltpu.VMEM((tm, ATTN_WIDTH), F32), pltpu.VMEM((4, tm, LANES), F32)],
        compiler_params=params,
        name="attn_prompt",
    )(xp, wi, wo, inv, sgn, sink, lng, lnb, next_w_in, next_w_out)

    return yp, ys, kp, vp, ksn, vsn, next_wi, next_wo


def _conv_layer(xp, xs, state, wi, w_dw, b_dw, cg, cb, wo, ln_g, ln_b):
    seq = xp.shape[0]
    n_batch = state.shape[0]
    n_new = xs.shape[0] // n_batch
    hist = CONV_WIDTH - 1
    vecs = [b_dw[None, :], cg[None, :], cb[None, :], ln_g[None, :], ln_b[None, :]]

    tm = ROW_BLOCK
    yp, cp = pl.pallas_call(
        _conv_prompt_kernel,
        grid=(seq // tm,),
        in_specs=[pl.BlockSpec((tm, D_MODEL), lambda i: (i, 0)),
                  _const_spec(wi.shape), _const_spec(wo.shape), _const_spec(w_dw.shape)]
                 + [_const_spec(t.shape) for t in vecs],
        out_specs=[pl.BlockSpec((tm, D_MODEL), lambda i: (i, 0)),
                   pl.BlockSpec((hist, D_MODEL), lambda i: (0, 0))],
        out_shape=[jax.ShapeDtypeStruct((seq, D_MODEL), F32),
                   jax.ShapeDtypeStruct((hist, D_MODEL), F32)],
        scratch_shapes=[pltpu.VMEM((D_MODEL // LANES, CONV_HIST + tm, LANES), F32),
                        pltpu.VMEM((tm, D_MODEL), F32)],
        compiler_params=pltpu.CompilerParams(dimension_semantics=("arbitrary",),
                                             vmem_limit_bytes=VMEM_LIMIT_BYTES),
        name="conv_prompt",
    )(xp, wi, wo, w_dw, *vecs)

    rows = xs.shape[0]
    ys, cs = pl.pallas_call(
        functools.partial(_conv_sample_kernel, n_batch=n_batch, n_new=n_new),
        in_specs=[_vmem_spec()] * 10,
        out_specs=[_vmem_spec()] * 2,
        out_shape=[jax.ShapeDtypeStruct((rows, D_MODEL), F32),
                   jax.ShapeDtypeStruct((n_batch, hist, D_MODEL), F32)],
        scratch_shapes=[pltpu.VMEM((D_MODEL // LANES, n_batch * (CONV_HIST + n_new), LANES), F32),
                        pltpu.VMEM((rows, D_MODEL), F32)],
        compiler_params=pltpu.CompilerParams(vmem_limit_bytes=VMEM_LIMIT_BYTES),
        name="conv_sample",
    )(xs, state, wi, wo, w_dw, *vecs)
    return yp, ys, cp, cs


def kernel(x_prompt, x_sample, cache_k, cache_v, state_conv, attn_w_in, attn_sink, attn_w_out,
           conv_w_in, conv_w_dw, conv_b_dw, conv_ln_g, conv_ln_b, conv_w_out, post_ln_g, post_ln_b):
    batch, seq, _ = x_prompt.shape
    dec_batch, dec_seq, _ = x_sample.shape
    wc = cache_k.shape[2]
    assert batch == 1 and wc == WIN_ROWS and seq % ROW_BLOCK == 0
    xp = x_prompt.reshape(seq, D_MODEL)
    xs = x_sample.reshape(dec_batch * dec_seq, D_MODEL)

    def feature_major(c):
        return jnp.transpose(c, (0, 2, 3, 1)).reshape(c.shape[0], KV_WIDTH, wc)

    def frame_major(c_t, n):
        return jnp.transpose(c_t.reshape(n, N_KV_HEADS, HEAD_DIM, wc), (0, 3, 1, 2))[None]

    xp, xs, kp, vp, ksn, vsn, conv_wi, conv_wo = _attn_layer(
        xp, xs, feature_major(cache_k[0]), feature_major(cache_v[0]),
        attn_w_in[0], attn_sink[0], attn_w_out[0], post_ln_g[0], post_ln_b[0],
        conv_w_in[0], conv_w_out[0])
    xp, xs, cp, cs = _conv_layer(
        xp, xs, state_conv[0], conv_wi, conv_w_dw[0], conv_b_dw[0], conv_ln_g[0], conv_ln_b[0],
        conv_wo, post_ln_g[1], post_ln_b[1])

    return (xp.reshape(batch, seq, D_MODEL), xs.reshape(dec_batch, dec_seq, D_MODEL),
            frame_major(kp, 1), frame_major(vp, 1),
            frame_major(ksn, dec_batch), frame_major(vsn, dec_batch),
            cp[None, None], cs[None])
```

```python
import functools

import jax
import jax.numpy as jnp
from jax import lax
from jax.experimental import pallas as pl
from jax.experimental.pallas import tpu as pltpu

F32 = jnp.float32
BF16 = jnp.bfloat16

D_MODEL = 2048
DEPTH = 2
PAST_LEN = 1024
CHUNK = 64
HEAD_DIM = 64
N_HEADS = D_MODEL // HEAD_DIM
N_KV_HEADS = 8
GROUP = N_HEADS // N_KV_HEADS
ATTN_WIDTH = N_HEADS * HEAD_DIM
KV_WIDTH = N_KV_HEADS * HEAD_DIM
WIN_CHUNKS = 2
WIN_ROWS = WIN_CHUNKS * CHUNK
ROT_DIM = HEAD_DIM // 4
ROPE_THETA = 500000.0
CONV_WIDTH = 31
LN_EPS = 1e-5
DEEPNORM_ALPHA = (2.0 * DEPTH) ** 0.25
ATTN_SCALE = HEAD_DIM ** -0.5
LOG2E = 1.4426950408889634
Q_SCALE = ATTN_SCALE * LOG2E

LANES = 128
SUBLANES = 8
VMEM_LIMIT_BYTES = 60 * 1024 * 1024
ROW_BLOCK = 256
CONV_HIST = 32
CONV_ROW_TILE = 32
PAIR_ROWS = 2 * CHUNK
PAIR_KEYS = WIN_ROWS + PAIR_ROWS


def _layer_norm(r, g, b):
    mu = jnp.mean(r, axis=-1, keepdims=True)
    d = r - mu
    var = jnp.mean(d * d, axis=-1, keepdims=True)
    return d * lax.rsqrt(var + LN_EPS) * g + b


def _silu(t):
    return t * jax.nn.sigmoid(t)


def _rope_tables(pos, inv_ref, sgn_ref):
    ang = pos * inv_ref[...]
    sin = jnp.sin(ang)
    return jnp.cos(ang), sin if sgn_ref is None else sin * sgn_ref[...]


def _rope(t, cosv, sinv):
    lane = lax.broadcasted_iota(jnp.int32, (1, LANES), 1) % HEAD_DIM
    partner = jnp.where(lane < ROT_DIM // 2,
                        pltpu.roll(t, LANES - ROT_DIM // 2, 1),
                        pltpu.roll(t, ROT_DIM // 2, 1))
    return t * cosv + partner * sinv


def _lo_hi(col):
    low = lax.broadcasted_iota(jnp.int32, (1, LANES), 1) < HEAD_DIM
    swapped = pltpu.roll(col, HEAD_DIM, 1)
    zero = jnp.zeros_like(col)
    even_lo = jnp.where(low, col, zero).astype(BF16)
    even_hi = jnp.where(low, zero, swapped).astype(BF16)
    odd_lo = jnp.where(low, swapped, zero).astype(BF16)
    odd_hi = jnp.where(low, zero, col).astype(BF16)
    return even_lo, even_hi, odd_lo, odd_hi


def _scores(q2, k_lo, k_hi):
    qst = jnp.concatenate([q2[:, :LANES], q2[:, LANES:]], axis=0)
    dims = (((1,), (1,)), ((), ()))
    return (lax.dot_general(qst, k_lo, dims, preferred_element_type=F32),
            lax.dot_general(qst, k_hi, dims, preferred_element_type=F32))


def _sink_softmax(s, sk, bias):
    if bias is not None:
        s = s + bias
    m = jnp.maximum(jnp.max(s, axis=-1, keepdims=True), sk)
    p = jnp.exp2(s - m)
    denom = jnp.sum(p, axis=-1, keepdims=True) + jnp.exp2(sk - m)
    return p.astype(BF16), 1.0 / denom


def _weighted_values(p_a, inv_a, p_b, inv_b, v_lo, v_hi):
    o_a = jnp.dot(p_a, v_lo, preferred_element_type=F32)
    o_b = jnp.dot(p_b, v_hi, preferred_element_type=F32)
    return o_a * inv_a + o_b * inv_b


def _attend_heads(q_of, k_of, v_of, sink_ref, q_rows, bias):
    scores = [_scores(q_of(kv), *k_of(kv)) for kv in range(N_KV_HEADS)]
    probs = []
    for kv, (s_a, s_b) in enumerate(scores):
        sink_a, sink_b = _sink_cols(sink_ref, kv, q_rows)
        probs.append(_sink_softmax(s_a, sink_a, bias) + _sink_softmax(s_b, sink_b, bias))
    return [_weighted_values(*probs[kv], *v_of(kv)) for kv in range(N_KV_HEADS)]


def _sink_cols(sink_ref, kv, q_rows):
    row = lax.broadcasted_iota(jnp.int32, (2 * q_rows, 1), 0)
    first = row < q_rows
    h = kv * GROUP
    return (jnp.where(first, sink_ref[h + 0] * LOG2E, sink_ref[h + 2] * LOG2E),
            jnp.where(first, sink_ref[h + 1] * LOG2E, sink_ref[h + 3] * LOG2E))


def _split_attn_w_in(w_ref):
    k0, v0, g0 = ATTN_WIDTH, ATTN_WIDTH + KV_WIDTH, ATTN_WIDTH + 2 * KV_WIDTH
    return w_ref.at[:, 0:k0], w_ref.at[:, k0:v0], w_ref.at[:, v0:g0], w_ref.at[:, g0:g0 + ATTN_WIDTH]


def _attn_prompt_kernel(x_ref, w_ref, wo_ref, inv_ref, sgn_ref,
                        sink_ref, lng_ref, lnb_ref, next_wi_ref, next_wo_ref,
                        y_ref, knew_ref, vnew_ref, next_wi_bf_ref, next_wo_bf_ref,
                        qs, klo, khi, vlo, vhi, os_, rope_tab):
    i = pl.program_id(0)
    last = pl.num_programs(0) - 1
    tm = x_ref.shape[0]
    next_wi_bf_ref[...] = next_wi_ref[...].astype(BF16)
    next_wo_bf_ref[...] = next_wo_ref[...].astype(BF16)
    wq_ref, wk_ref, wv_ref, wg_ref = _split_attn_w_in(w_ref)

    @pl.when(i == 0)
    def _():
        for ref in (klo, khi, vlo, vhi):
            ref[0:WIN_ROWS, :] = jnp.zeros((WIN_ROWS, ref.shape[1]), BF16)
        off = lax.broadcasted_iota(jnp.int32, (tm, 1), 0).astype(F32)
        cos_r, sin_r = _rope_tables(off, inv_ref, None)
        rope_tab[0] = cos_r
        rope_tab[1] = sin_r
        rope_tab[2] = cos_r * sgn_ref[...]
        rope_tab[3] = sin_r * sgn_ref[...]

    x = x_ref[...]
    xb = x.astype(BF16)
    start = jnp.full((1, 1), i * tm, jnp.int32).astype(F32)
    cos_s, sin_s = _rope_tables(start, inv_ref, None)
    cosv = cos_s * rope_tab[0] - sin_s * rope_tab[1]
    sinv = sin_s * rope_tab[2] + cos_s * rope_tab[3]

    q = jnp.dot(xb, wq_ref[...], preferred_element_type=F32)
    for j in range(ATTN_WIDTH // LANES):
        sl = slice(j * LANES, (j + 1) * LANES)
        qs[:, sl] = (_rope(q[:, sl], cosv, sinv) * Q_SCALE).astype(BF16)

    k = jnp.dot(xb, wk_ref[...], preferred_element_type=F32)
    v = jnp.dot(xb, wv_ref[...], preferred_element_type=F32)
    new_rows = slice(WIN_ROWS, WIN_ROWS + tm)
    for c in range(KV_WIDTH // LANES):
        sl = slice(c * LANES, (c + 1) * LANES)
        kc = _rope(k[:, sl], cosv, sinv)
        vc = v[:, sl]

        @pl.when(i == last)
        def _():
            knew_ref[sl, :] = kc[tm - WIN_ROWS:, :].T
            vnew_ref[sl, :] = vc[tm - WIN_ROWS:, :].T

        ev = slice(2 * c * LANES, (2 * c + 1) * LANES)
        od = slice((2 * c + 1) * LANES, (2 * c + 2) * LANES)
        klo[new_rows, ev], khi[new_rows, ev], klo[new_rows, od], khi[new_rows, od] = _lo_hi(kc)
        vlo[new_rows, ev], vhi[new_rows, ev], vlo[new_rows, od], vhi[new_rows, od] = _lo_hi(vc)

    row = lax.broadcasted_iota(jnp.int32, (2 * PAIR_ROWS, 1), 0)
    col = lax.broadcasted_iota(jnp.int32, (1, PAIR_KEYS), 1)
    first_col = jnp.where(row % PAIR_ROWS >= CHUNK, CHUNK, 0)
    in_window = (col >= first_col) & (col < first_col + (PAIR_KEYS - CHUNK))
    window_bias = jnp.where(in_window, 0.0, -jnp.inf)

    def pair_body(p, carry):
        r0 = pl.multiple_of(p * PAIR_ROWS, PAIR_ROWS)
        first_key_chunk = i * (tm // CHUNK) + 2 * p - WIN_CHUNKS
        started = col >= jnp.maximum(0, -first_key_chunk) * CHUNK
        bias = window_bias + jnp.where(started, 0.0, -jnp.inf)
        keys = pl.ds(r0, PAIR_KEYS)

        def head_lanes(kv):
            return slice(kv * LANES, (kv + 1) * LANES)

        outs = _attend_heads(
            lambda kv: qs[pl.ds(r0, PAIR_ROWS), kv * 2 * LANES:(kv + 1) * 2 * LANES],
            lambda kv: (klo[keys, head_lanes(kv)], khi[keys, head_lanes(kv)]),
            lambda kv: (vlo[keys, head_lanes(kv)], vhi[keys, head_lanes(kv)]),
            sink_ref, PAIR_ROWS, bias)
        for kv, o in enumerate(outs):
            os_[pl.ds(r0, PAIR_ROWS), kv * 2 * LANES:kv * 2 * LANES + LANES] = o[:PAIR_ROWS]
            os_[pl.ds(r0, PAIR_ROWS), kv * 2 * LANES + LANES:(kv + 1) * 2 * LANES] = o[PAIR_ROWS:]
        return carry

    lax.fori_loop(0, tm // PAIR_ROWS, pair_body, 0)

    for ref in (klo, khi, vlo, vhi):
        ref[0:WIN_ROWS, :] = ref[tm:tm + WIN_ROWS, :]

    g = jnp.dot(xb, wg_ref[...], preferred_element_type=F32)
    z = (os_[...] * _silu(g)).astype(BF16)
    y = jnp.dot(z, wo_ref[...], preferred_element_type=F32)
    y_ref[...] = _layer_norm(DEEPNORM_ALPHA * x + y, lng_ref[...], lnb_ref[...])


def _attn_sample_proj_kernel(xk_ref, w_ref, wo_ref, h_ref, w_bf_ref, wo_bf_ref):
    kstep = pl.program_id(0)
    w_bf = w_ref[...].astype(BF16)
    w_bf_ref[...] = w_bf
    wo_bf_ref[...] = wo_ref[...].astype(BF16)
    part = jnp.dot(xk_ref[...].astype(BF16), w_bf, preferred_element_type=F32)

    @pl.when(kstep == 0)
    def _():
        h_ref[...] = part

    @pl.when(kstep > 0)
    def _():
        h_ref[...] += part


def _attn_sample_kernel(x_ref, ck_ref, cv_ref, h_ref, wo_ref, inv_ref, sgn_ref, sink_ref, lng_ref, lnb_ref,
                        y_ref, knew_ref, vnew_ref, qs, ks, vs, os_, *, n_batch, n_new, n_cache):
    rows = n_batch * n_new
    hq_ref, hk_ref, hv_ref, hg_ref = _split_attn_w_in(h_ref)
    x = x_ref[...]
    pos = (PAST_LEN + lax.broadcasted_iota(jnp.int32, (rows, 1), 0) % n_new).astype(F32)
    cosv, sinv = _rope_tables(pos, inv_ref, sgn_ref)

    for j in range(ATTN_WIDTH // LANES):
        sl = slice(j * LANES, (j + 1) * LANES)
        qs[:, sl] = (_rope(hq_ref[:, sl], cosv, sinv) * Q_SCALE).astype(BF16)
    k = hk_ref[...]
    v = hv_ref[...]
    k_t = jnp.concatenate([_rope(k[:, c * LANES:(c + 1) * LANES], cosv, sinv)
                           for c in range(KV_WIDTH // LANES)], axis=1).T
    v_t = v.T
    for j in range(rows // LANES):
        ks[j] = k_t[:, j * LANES:(j + 1) * LANES]
        vs[j] = v_t[:, j * LANES:(j + 1) * LANES]

    per_tile = LANES // n_new
    lane = lax.broadcasted_iota(jnp.int32, (1, LANES), 1)
    zeros = jnp.zeros((HEAD_DIM, 2 * LANES), BF16)
    dims_nt = (((1,), (1,)), ((), ()))

    def batch_body(b, carry):
        r0 = pl.multiple_of(b * n_new, n_new)
        off = (b % per_tile) * n_new
        k_old, v_old = ck_ref[b], cv_ref[b]
        k_new, v_new = ks[b // per_tile], vs[b // per_tile]
        mine = (lane >= off) & (lane < off + n_new)

        to_end = n_cache - n_new - off
        to_end = jnp.where(to_end < 0, to_end + LANES, to_end)
        for old, new, out_ref in ((k_old, k_new, knew_ref), (v_old, v_new, vnew_ref)):
            out_ref[b] = jnp.where(lane < n_cache - n_new,
                                   pltpu.roll(old, n_cache - n_new, 1), pltpu.roll(new, to_end, 1))

        bias = jnp.concatenate([jnp.zeros((1, n_cache), F32), jnp.where(mine, 0.0, -jnp.inf)], axis=1)
        k_all = jnp.concatenate([k_old, k_new], axis=1).astype(BF16)
        v_all = jnp.concatenate([v_old, jnp.where(mine, v_new, 0.0)], axis=1).astype(BF16)

        scores = []
        for kv in range(N_KV_HEADS):
            q2 = qs[pl.ds(r0, n_new), kv * 2 * LANES:(kv + 1) * 2 * LANES]
            qst = jnp.concatenate([q2[:, :LANES], q2[:, LANES:]], axis=0)
            k_kv = k_all[kv * HEAD_DIM:(kv + 1) * HEAD_DIM, :]
            scores.append((jnp.dot(qst, jnp.concatenate([k_kv, zeros], axis=0), preferred_element_type=F32),
                           jnp.dot(qst, jnp.concatenate([zeros, k_kv], axis=0), preferred_element_type=F32)))
        probs = []
        for kv, (s_a, s_b) in enumerate(scores):
            sink_a, sink_b = _sink_cols(sink_ref, kv, n_new)
            probs.append(_sink_softmax(s_a, sink_a, bias) + _sink_softmax(s_b, sink_b, bias))
        for kv, (p_a, inv_a, p_b, inv_b) in enumerate(probs):
            v_kv = v_all[kv * HEAD_DIM:(kv + 1) * HEAD_DIM, :]
            o_a = lax.dot_general(p_a, jnp.concatenate([v_kv, zeros], axis=0), dims_nt,
                                  preferred_element_type=F32)
            o_b = lax.dot_general(p_b, jnp.concatenate([zeros, v_kv], axis=0), dims_nt,
                                  preferred_element_type=F32)
            o = o_a * inv_a + o_b * inv_b
            os_[pl.ds(r0, n_new), kv * 2 * LANES:kv * 2 * LANES + LANES] = o[:n_new]
            os_[pl.ds(r0, n_new), kv * 2 * LANES + LANES:(kv + 1) * 2 * LANES] = o[n_new:]
        return carry

    lax.fori_loop(0, n_batch, batch_body, 0)

    z = (os_[...] * _silu(hg_ref[...])).astype(BF16)
    y = jnp.dot(z, wo_ref[...], preferred_element_type=F32)
    y_ref[...] = _layer_norm(DEEPNORM_ALPHA * x + y, lng_ref[...], lnb_ref[...])


def _depthwise(u_scr, c_scr, wdw_ref, bdw_ref, base, out_base, n_rows):
    off = CONV_HIST - (CONV_WIDTH - 1)
    for l in range(D_MODEL // LANES):
        sl = slice(l * LANES, (l + 1) * LANES)
        acc = jnp.zeros((n_rows, LANES), F32)
        for j in range(CONV_WIDTH):
            acc = acc + wdw_ref[j:j + 1, sl] * u_scr[l, pl.ds(base + j + off, n_rows, stride=1), :]
        c_scr[pl.ds(out_base, n_rows), sl] = acc + bdw_ref[:, sl]


def _conv_gate(c, g, cg_ref, cb_ref):
    cn = _layer_norm(c, cg_ref[...], cb_ref[...])
    return (_silu(cn) * _silu(g)).astype(BF16)


def _split_conv_w_in(w_ref):
    return tuple(w_ref.at[:, n * D_MODEL:(n + 1) * D_MODEL] for n in range(3))


def _conv_prompt_kernel(x_ref, w_ref, wo_ref, wdw_ref, bdw_ref, cg_ref, cb_ref,
                        lng_ref, lnb_ref,
                        y_ref, state_ref,
                        u_scr, c_scr):
    i = pl.program_id(0)
    last = pl.num_programs(0) - 1
    tm = x_ref.shape[0]
    wa_ref, wb_ref, wg_ref = _split_conv_w_in(w_ref)
    n_tiles = D_MODEL // LANES

    @pl.when(i == 0)
    def _():
        u_scr[:, 0:CONV_HIST, :] = jnp.zeros((n_tiles, CONV_HIST, LANES), F32)

    x = x_ref[...]
    xb = x.astype(BF16)
    a = jnp.dot(xb, wa_ref[...], preferred_element_type=F32)
    b = jnp.dot(xb, wb_ref[...], preferred_element_type=F32)
    u = a * jax.nn.sigmoid(b)
    for l in range(n_tiles):
        u_scr[l, CONV_HIST:CONV_HIST + tm, :] = u[:, l * LANES:(l + 1) * LANES]

    def row_body(rt, carry):
        base = pl.multiple_of(rt * CONV_ROW_TILE, CONV_ROW_TILE)
        _depthwise(u_scr, c_scr, wdw_ref, bdw_ref, base, base, CONV_ROW_TILE)
        return carry

    lax.fori_loop(0, tm // CONV_ROW_TILE, row_body, 0)

    @pl.when(i == last)
    def _():
        for l in range(n_tiles):
            state_ref[:, l * LANES:(l + 1) * LANES] = (
                u_scr[l, CONV_HIST + tm - (CONV_WIDTH - 1):CONV_HIST + tm, :])

    u_scr[:, 0:CONV_HIST, :] = u_scr[:, tm:tm + CONV_HIST, :]

    g = jnp.dot(xb, wg_ref[...], preferred_element_type=F32)
    z = _conv_gate(c_scr[...], g, cg_ref, cb_ref)
    y = jnp.dot(z, wo_ref[...], preferred_element_type=F32)
    y_ref[...] = _layer_norm(DEEPNORM_ALPHA * x + y, lng_ref[...], lnb_ref[...])


def _conv_sample_kernel(x_ref, st_ref, w_ref, wo_ref, wdw_ref, bdw_ref, cg_ref, cb_ref,
                        lng_ref, lnb_ref,
                        y_ref, state_ref,
                        u_scr, c_scr, *, n_batch, n_new):
    slot = CONV_HIST + n_new
    hist = CONV_WIDTH - 1
    wa_ref, wb_ref, wg_ref = _split_conv_w_in(w_ref)
    x = x_ref[...]
    xb = x.astype(BF16)
    a = jnp.dot(xb, wa_ref[...], preferred_element_type=F32)
    b = jnp.dot(xb, wb_ref[...], preferred_element_type=F32)
    u = a * jax.nn.sigmoid(b)
    pad = CONV_HIST - hist
    for bi in range(n_batch):
        base = bi * slot
        for l in range(D_MODEL // LANES):
            sl = slice(l * LANES, (l + 1) * LANES)
            u_scr[l, base:base + pad, :] = jnp.zeros((pad, LANES), F32)
            u_scr[l, base + pad:base + CONV_HIST, :] = st_ref[bi, :, sl]
            u_scr[l, base + CONV_HIST:base + slot, :] = u[bi * n_new:(bi + 1) * n_new, sl]
            state_ref[bi, :, sl] = u_scr[l, base + slot - hist:base + slot, :]

    def batch_body(bi, carry):
        _depthwise(u_scr, c_scr, wdw_ref, bdw_ref, pl.multiple_of(bi * slot, SUBLANES),
                   pl.multiple_of(bi * n_new, n_new), n_new)
        return carry

    lax.fori_loop(0, n_batch, batch_body, 0)

    g = jnp.dot(xb, wg_ref[...], preferred_element_type=F32)
    z = _conv_gate(c_scr[...], g, cg_ref, cb_ref)
    y = jnp.dot(z, wo_ref[...], preferred_element_type=F32)
    y_ref[...] = _layer_norm(DEEPNORM_ALPHA * x + y, lng_ref[...], lnb_ref[...])


def _const_spec(shape):
    return pl.BlockSpec(shape, lambda i: (0,) * len(shape), pipeline_mode=pl.Buffered(1))


def _const_out_spec(shape):
    return pl.BlockSpec(shape, lambda i: (0,) * len(shape), pipeline_mode=pl.Buffered(1))


def _vmem_spec():
    return pl.BlockSpec(memory_space=pltpu.VMEM)


def _smem_spec():
    return pl.BlockSpec(memory_space=pltpu.SMEM)


def _rope_lane_consts():
    half = ROT_DIM // 2
    inv = ROPE_THETA ** (-jnp.arange(half, dtype=F32) * (2.0 / ROT_DIM))
    zeros = jnp.zeros((HEAD_DIM - ROT_DIM,), F32)
    inv_head = jnp.concatenate([inv, inv, zeros])
    sgn_head = jnp.concatenate([-jnp.ones((half,), F32), jnp.ones((half,), F32), zeros])
    reps = LANES // HEAD_DIM
    return jnp.tile(inv_head, reps)[None, :], jnp.tile(sgn_head, reps)[None, :]


def _attn_layer(xp, xs, ck, cv, w_in, sink, w_out, ln_g, ln_b, next_w_in, next_w_out):
    seq = xp.shape[0]
    n_batch, n_cache = ck.shape[0], ck.shape[2]
    n_new = xs.shape[0] // n_batch
    assert n_cache == LANES and LANES % n_new == 0 and xs.shape[0] % LANES == 0
    inv, sgn = _rope_lane_consts()
    lng, lnb = ln_g[None, :], ln_b[None, :]
    params = pltpu.CompilerParams(dimension_semantics=("arbitrary",), vmem_limit_bytes=VMEM_LIMIT_BYTES)

    rows = xs.shape[0]
    k_rows = 2 * LANES
    h, wi, wo = pl.pallas_call(
        _attn_sample_proj_kernel,
        grid=(D_MODEL // k_rows,),
        in_specs=[pl.BlockSpec((rows, k_rows), lambda k: (0, k)),
                  pl.BlockSpec((k_rows, w_in.shape[1]), lambda k: (k, 0)),
                  pl.BlockSpec((k_rows, w_out.shape[1]), lambda k: (k, 0))],
        out_specs=[_const_out_spec((rows, w_in.shape[1])),
                   pl.BlockSpec((k_rows, w_in.shape[1]), lambda k: (k, 0)),
                   pl.BlockSpec((k_rows, w_out.shape[1]), lambda k: (k, 0))],
        out_shape=[jax.ShapeDtypeStruct((rows, w_in.shape[1]), F32),
                   jax.ShapeDtypeStruct(w_in.shape, BF16),
                   jax.ShapeDtypeStruct(w_out.shape, BF16)],
        compiler_params=params,
        name="attn_sample_proj",
    )(xs, w_in, w_out)

    new_t = pltpu.VMEM((rows // LANES, KV_WIDTH, LANES), F32)
    ys, ksn, vsn = pl.pallas_call(
        functools.partial(_attn_sample_kernel, n_batch=n_batch, n_new=n_new, n_cache=n_cache),
        in_specs=[_vmem_spec()] * 7 + [_smem_spec(), _vmem_spec(), _vmem_spec()],
        out_specs=[_vmem_spec()] * 3,
        out_shape=[jax.ShapeDtypeStruct((rows, D_MODEL), F32),
                   jax.ShapeDtypeStruct(ck.shape, F32),
                   jax.ShapeDtypeStruct(cv.shape, F32)],
        scratch_shapes=[pltpu.VMEM((rows, ATTN_WIDTH), BF16), new_t, new_t,
                        pltpu.VMEM((rows, ATTN_WIDTH), F32)],
        compiler_params=pltpu.CompilerParams(vmem_limit_bytes=VMEM_LIMIT_BYTES),
        name="attn_sample",
    )(xs, ck, cv, h, wo, inv, sgn, sink, lng, lnb)

    tm = ROW_BLOCK
    n_steps = seq // tm
    w_rows = next_w_in.shape[0] // n_steps
    assert next_w_in.shape[0] == next_w_out.shape[0] == w_rows * n_steps and w_rows % (2 * SUBLANES) == 0
    kv_scr = pltpu.VMEM((WIN_ROWS + tm, N_KV_HEADS * LANES), BF16)
    yp, kp, vp, next_wi, next_wo = pl.pallas_call(
        _attn_prompt_kernel,
        grid=(n_steps,),
        in_specs=[pl.BlockSpec((tm, D_MODEL), lambda i: (i, 0)),
                  _const_spec(wi.shape), _const_spec(wo.shape),
                  _const_spec(inv.shape), _const_spec(sgn.shape), _smem_spec(),
                  _const_spec(lng.shape), _const_spec(lnb.shape),
                  pl.BlockSpec((w_rows, next_w_in.shape[1]), lambda i: (i, 0)),
                  pl.BlockSpec((w_rows, next_w_out.shape[1]), lambda i: (i, 0))],
        out_specs=[pl.BlockSpec((tm, D_MODEL), lambda i: (i, 0)),
                   pl.BlockSpec((KV_WIDTH, WIN_ROWS), lambda i: (0, 0)),
                   pl.BlockSpec((KV_WIDTH, WIN_ROWS), lambda i: (0, 0)),
                   pl.BlockSpec((w_rows, next_w_in.shape[1]), lambda i: (i, 0)),
                   pl.BlockSpec((w_rows, next_w_out.shape[1]), lambda i: (i, 0))],
        out_shape=[jax.ShapeDtypeStruct((seq, D_MODEL), F32),
                   jax.ShapeDtypeStruct((KV_WIDTH, WIN_ROWS), F32),
                   jax.ShapeDtypeStruct((KV_WIDTH, WIN_ROWS), F32),
                   jax.ShapeDtypeStruct(next_w_in.shape, BF16),
                   jax.ShapeDtypeStruct(next_w_out.shape, BF16)],
        scratch_shapes=[pltpu.VMEM((tm, ATTN_WIDTH), BF16), kv_scr, kv_scr, kv_scr, kv_scr,
                        pltpu.VMEM((tm, ATTN_WIDTH), F32), pltpu.VMEM((4, tm, LANES), F32)],
        compiler_params=params,
        name="attn_prompt",
    )(xp, wi, wo, inv, sgn, sink, lng, lnb, next_w_in, next_w_out)

    return yp, ys, kp, vp, ksn, vsn, next_wi, next_wo


def _conv_layer(xp, xs, state, wi, w_dw, b_dw, cg, cb, wo, ln_g, ln_b):
    seq = xp.shape[0]
    n_batch = state.shape[0]
    n_new = xs.shape[0] // n_batch
    hist = CONV_WIDTH - 1
    vecs = [b_dw[None, :], cg[None, :], cb[None, :], ln_g[None, :], ln_b[None, :]]

    tm = ROW_BLOCK
    yp, cp = pl.pallas_call(
        _conv_prompt_kernel,
        grid=(seq // tm,),
        in_specs=[pl.BlockSpec((tm, D_MODEL), lambda i: (i, 0)),
                  _const_spec(wi.shape), _const_spec(wo.shape), _const_spec(w_dw.shape)]
                 + [_const_spec(t.shape) for t in vecs],
        out_specs=[pl.BlockSpec((tm, D_MODEL), lambda i: (i, 0)),
                   pl.BlockSpec((hist, D_MODEL), lambda i: (0, 0))],
        out_shape=[jax.ShapeDtypeStruct((seq, D_MODEL), F32),
                   jax.ShapeDtypeStruct((hist, D_MODEL), F32)],
        scratch_shapes=[pltpu.VMEM((D_MODEL // LANES, CONV_HIST + tm, LANES), F32),
                        pltpu.VMEM((tm, D_MODEL), F32)],
        compiler_params=pltpu.CompilerParams(dimension_semantics=("arbitrary",),
                                             vmem_limit_bytes=VMEM_LIMIT_BYTES),
        name="conv_prompt",
    )(xp, wi, wo, w_dw, *vecs)

    rows = xs.shape[0]
    ys, cs = pl.pallas_call(
        functools.partial(_conv_sample_kernel, n_batch=n_batch, n_new=n_new),
        in_specs=[_vmem_spec()] * 10,
        out_specs=[_vmem_spec()] * 2,
        out_shape=[jax.ShapeDtypeStruct((rows, D_MODEL), F32),
                   jax.ShapeDtypeStruct((n_batch, hist, D_MODEL), F32)],
        scratch_shapes=[pltpu.VMEM((D_MODEL // LANES, n_batch * (CONV_HIST + n_new), LANES), F32),
                        pltpu.VMEM((rows, D_MODEL), F32)],
        compiler_params=pltpu.CompilerParams(vmem_limit_bytes=VMEM_LIMIT_BYTES),
        name="conv_sample",
    )(xs, state, wi, wo, w_dw, *vecs)
    return yp, ys, cp, cs


def kernel(x_prompt, x_sample, cache_k, cache_v, state_conv, attn_w_in, attn_sink, attn_w_out,
           conv_w_in, conv_w_dw, conv_b_dw, conv_ln_g, conv_ln_b, conv_w_out, post_ln_g, post_ln_b):
    batch, seq, _ = x_prompt.shape
    dec_batch, dec_seq, _ = x_sample.shape
    wc = cache_k.shape[2]
    assert batch == 1 and wc == WIN_ROWS and seq % ROW_BLOCK == 0
    xp = x_prompt.reshape(seq, D_MODEL)
    xs = x_sample.reshape(dec_batch * dec_seq, D_MODEL)

    def feature_major(c):
        return jnp.transpose(c, (0, 2, 3, 1)).reshape(c.shape[0], KV_WIDTH, wc)

    def frame_major(c_t, n):
        return jnp.transpose(c_t.reshape(n, N_KV_HEADS, HEAD_DIM, wc), (0, 3, 1, 2))[None]

    xp, xs, kp, vp, ksn, vsn, conv_wi, conv_wo = _attn_layer(
        xp, xs, feature_major(cache_k[0]), feature_major(cache_v[0]),
        attn_w_in[0], attn_sink[0], attn_w_out[0], post_ln_g[0], post_ln_b[0],
        conv_w_in[0], conv_w_out[0])
    xp, xs, cp, cs = _conv_layer(
        xp, xs, state_conv[0], conv_wi, conv_w_dw[0], conv_b_dw[0], conv_ln_g[0], conv_ln_b[0],
        conv_wo, post_ln_g[1], post_ln_b[1])

    return (xp.reshape(batch, seq, D_MODEL), xs.reshape(dec_batch, dec_seq, D_MODEL),
            frame_major(kp, 1), frame_major(vp, 1),
            frame_major(ksn, dec_batch), frame_major(vsn, dec_batch),
            cp[None, None], cs[None])
```

```python
import functools

import jax
import jax.numpy as jnp
from jax import lax
from jax.experimental import pallas as pl
from jax.experimental.pallas import tpu as pltpu

F32 = jnp.float32
BF16 = jnp.bfloat16

D_MODEL = 2048
DEPTH = 2
PAST_LEN = 1024
CHUNK = 64
HEAD_DIM = 64
N_HEADS = D_MODEL // HEAD_DIM
N_KV_HEADS = 8
GROUP = N_HEADS // N_KV_HEADS
ATTN_WIDTH = N_HEADS * HEAD_DIM
KV_WIDTH = N_KV_HEADS * HEAD_DIM
WIN_CHUNKS = 2
WIN_ROWS = WIN_CHUNKS * CHUNK
ROT_DIM = HEAD_DIM // 4
ROPE_THETA = 500000.0
CONV_WIDTH = 31
LN_EPS = 1e-5
DEEPNORM_ALPHA = (2.0 * DEPTH) ** 0.25
ATTN_SCALE = HEAD_DIM ** -0.5
LOG2E = 1.4426950408889634
Q_SCALE = ATTN_SCALE * LOG2E

LANES = 128
SUBLANES = 8
VMEM_LIMIT_BYTES = 60 * 1024 * 1024
ROW_BLOCK = 256
CONV_HIST = 32
CONV_ROW_TILE = 32
PAIR_ROWS = 2 * CHUNK
PAIR_KEYS = WIN_ROWS + PAIR_ROWS


def _layer_norm(r, g, b):
    mu = jnp.mean(r, axis=-1, keepdims=True)
    d = r - mu
    var = jnp.mean(d * d, axis=-1, keepdims=True)
    return d * lax.rsqrt(var + LN_EPS) * g + b


def _silu(t):
    return t * jax.nn.sigmoid(t)


def _rope_tables(pos, inv_ref, sgn_ref):
    ang = pos * inv_ref[...]
    sin = jnp.sin(ang)
    return jnp.cos(ang), sin if sgn_ref is None else sin * sgn_ref[...]


def _rope(t, cosv, sinv):
    lane = lax.broadcasted_iota(jnp.int32, (1, LANES), 1) % HEAD_DIM
    partner = jnp.where(lane < ROT_DIM // 2,
                        pltpu.roll(t, LANES - ROT_DIM // 2, 1),
                        pltpu.roll(t, ROT_DIM // 2, 1))
    return t * cosv + partner * sinv


def _lo_hi(col):
    low = lax.broadcasted_iota(jnp.int32, (1, LANES), 1) < HEAD_DIM
    swapped = pltpu.roll(col, HEAD_DIM, 1)
    zero = jnp.zeros_like(col)
    even_lo = jnp.where(low, col, zero).astype(BF16)
    even_hi = jnp.where(low, zero, swapped).astype(BF16)
    odd_lo = jnp.where(low, swapped, zero).astype(BF16)
    odd_hi = jnp.where(low, zero, col).astype(BF16)
    return even_lo, even_hi, odd_lo, odd_hi


def _scores(q2, k_lo, k_hi):
    qst = jnp.concatenate([q2[:, :LANES], q2[:, LANES:]], axis=0)
    dims = (((1,), (1,)), ((), ()))
    return (lax.dot_general(qst, k_lo, dims, preferred_element_type=F32),
            lax.dot_general(qst, k_hi, dims, preferred_element_type=F32))


def _sink_softmax(s, sk, bias):
    if bias is not None:
        s = s + bias
    m = jnp.maximum(jnp.max(s, axis=-1, keepdims=True), sk)
    p = jnp.exp2(s - m)
    denom = jnp.sum(p, axis=-1, keepdims=True) + jnp.exp2(sk - m)
    return p.astype(BF16), 1.0 / denom


def _weighted_values(p_a, inv_a, p_b, inv_b, v_lo, v_hi):
    o_a = jnp.dot(p_a, v_lo, preferred_element_type=F32)
    o_b = jnp.dot(p_b, v_hi, preferred_element_type=F32)
    return o_a * inv_a + o_b * inv_b


def _attend_heads(q_of, k_of, v_of, sink_ref, q_rows, bias):
    scores = [_scores(q_of(kv), *k_of(kv)) for kv in range(N_KV_HEADS)]
    probs = []
    for kv, (s_a, s_b) in enumerate(scores):
        sink_a, sink_b = _sink_cols(sink_ref, kv, q_rows)
        probs.append(_sink_softmax(s_a, sink_a, bias) + _sink_softmax(s_b, sink_b, bias))
    return [_weighted_values(*probs[kv], *v_of(kv)) for kv in range(N_KV_HEADS)]


def _sink_cols(sink_ref, kv, q_rows):
    row = lax.broadcasted_iota(jnp.int32, (2 * q_rows, 1), 0)
    first = row < q_rows
    h = kv * GROUP
    return (jnp.where(first, sink_ref[h + 0] * LOG2E, sink_ref[h + 2] * LOG2E),
            jnp.where(first, sink_ref[h + 1] * LOG2E, sink_ref[h + 3] * LOG2E))


def _split_attn_w_in(w_ref):
    k0, v0, g0 = ATTN_WIDTH, ATTN_WIDTH + KV_WIDTH, ATTN_WIDTH + 2 * KV_WIDTH
    return w_ref.at[:, 0:k0], w_ref.at[:, k0:v0], w_ref.at[:, v0:g0], w_ref.at[:, g0:g0 + ATTN_WIDTH]


def _attn_prompt_kernel(x_ref, w_ref, wo_ref, inv_ref, sgn_ref,
                        sink_ref, lng_ref, lnb_ref, next_wi_ref, next_wo_ref,
                        y_ref, knew_ref, vnew_ref, next_wi_bf_ref, next_wo_bf_ref,
                        qs, klo, khi, vlo, vhi, os_, rope_tab):
    i = pl.program_id(0)
    last = pl.num_programs(0) - 1
    tm = x_ref.shape[0]
    next_wi_bf_ref[...] = next_wi_ref[...].astype(BF16)
    next_wo_bf_ref[...] = next_wo_ref[...].astype(BF16)
    wq_ref, wk_ref, wv_ref, wg_ref = _split_attn_w_in(w_ref)

    @pl.when(i == 0)
    def _():
        for ref in (klo, khi, vlo, vhi):
            ref[0:WIN_ROWS, :] = jnp.zeros((WIN_ROWS, ref.shape[1]), BF16)
        off = lax.broadcasted_iota(jnp.int32, (tm, 1), 0).astype(F32)
        cos_r, sin_r = _rope_tables(off, inv_ref, None)
        rope_tab[0] = cos_r
        rope_tab[1] = sin_r
        rope_tab[2] = cos_r * sgn_ref[...]
        rope_tab[3] = sin_r * sgn_ref[...]

    x = x_ref[...]
    xb = x.astype(BF16)
    start = jnp.full((1, 1), i * tm, jnp.int32).astype(F32)
    cos_s, sin_s = _rope_tables(start, inv_ref, None)
    cosv = cos_s * rope_tab[0] - sin_s * rope_tab[1]
    sinv = sin_s * rope_tab[2] + cos_s * rope_tab[3]

    q = jnp.dot(xb, wq_ref[...], preferred_element_type=F32)
    for j in range(ATTN_WIDTH // LANES):
        sl = slice(j * LANES, (j + 1) * LANES)
        qs[:, sl] = (_rope(q[:, sl], cosv, sinv) * Q_SCALE).astype(BF16)

    k = jnp.dot(xb, wk_ref[...], preferred_element_type=F32)
    v = jnp.dot(xb, wv_ref[...], preferred_element_type=F32)
    new_rows = slice(WIN_ROWS, WIN_ROWS + tm)
    for c in range(KV_WIDTH // LANES):
        sl = slice(c * LANES, (c + 1) * LANES)
        kc = _rope(k[:, sl], cosv, sinv)
        vc = v[:, sl]

        @pl.when(i == last)
        def _():
            knew_ref[sl, :] = kc[tm - WIN_ROWS:, :].T
            vnew_ref[sl, :] = vc[tm - WIN_ROWS:, :].T

        ev = slice(2 * c * LANES, (2 * c + 1) * LANES)
        od = slice((2 * c + 1) * LANES, (2 * c + 2) * LANES)
        klo[new_rows, ev], khi[new_rows, ev], klo[new_rows, od], khi[new_rows, od] = _lo_hi(kc)
        vlo[new_rows, ev], vhi[new_rows, ev], vlo[new_rows, od], vhi[new_rows, od] = _lo_hi(vc)

    row = lax.broadcasted_iota(jnp.int32, (2 * PAIR_ROWS, 1), 0)
    col = lax.broadcasted_iota(jnp.int32, (1, PAIR_KEYS), 1)
    first_col = jnp.where(row % PAIR_ROWS >= CHUNK, CHUNK, 0)
    in_window = (col >= first_col) & (col < first_col + (PAIR_KEYS - CHUNK))
    window_bias = jnp.where(in_window, 0.0, -jnp.inf)

    def pair_body(p, carry):
        r0 = pl.multiple_of(p * PAIR_ROWS, PAIR_ROWS)
        first_key_chunk = i * (tm // CHUNK) + 2 * p - WIN_CHUNKS
        started = col >= jnp.maximum(0, -first_key_chunk) * CHUNK
        bias = window_bias + jnp.where(started, 0.0, -jnp.inf)
        keys = pl.ds(r0, PAIR_KEYS)

        def head_lanes(kv):
            return slice(kv * LANES, (kv + 1) * LANES)

        outs = _attend_heads(
            lambda kv: qs[pl.ds(r0, PAIR_ROWS), kv * 2 * LANES:(kv + 1) * 2 * LANES],
            lambda kv: (klo[keys, head_lanes(kv)], khi[keys, head_lanes(kv)]),
            lambda kv: (vlo[keys, head_lanes(kv)], vhi[keys, head_lanes(kv)]),
            sink_ref, PAIR_ROWS, bias)
        for kv, o in enumerate(outs):
            os_[pl.ds(r0, PAIR_ROWS), kv * 2 * LANES:kv * 2 * LANES + LANES] = o[:PAIR_ROWS]
            os_[pl.ds(r0, PAIR_ROWS), kv * 2 * LANES + LANES:(kv + 1) * 2 * LANES] = o[PAIR_ROWS:]
        return carry

    lax.fori_loop(0, tm // PAIR_ROWS, pair_body, 0)

    for ref in (klo, khi, vlo, vhi):
        ref[0:WIN_ROWS, :] = ref[tm:tm + WIN_ROWS, :]

    g = jnp.dot(xb, wg_ref[...], preferred_element_type=F32)
    z = (os_[...] * _silu(g)).astype(BF16)
    y = jnp.dot(z, wo_ref[...], preferred_element_type=F32)
    y_ref[...] = _layer_norm(DEEPNORM_ALPHA * x + y, lng_ref[...], lnb_ref[...])


def _attn_sample_proj_kernel(xk_ref, w_ref, wo_ref, h_ref, w_bf_ref, wo_bf_ref):
    kstep = pl.program_id(0)
    w_bf = w_ref[...].astype(BF16)
    w_bf_ref[...] = w_bf
    wo_bf_ref[...] = wo_ref[...].astype(BF16)
    part = jnp.dot(xk_ref[...].astype(BF16), w_bf, preferred_element_type=F32)

    @pl.when(kstep == 0)
    def _():
        h_ref[...] = part

    @pl.when(kstep > 0)
    def _():
        h_ref[...] += part


def _attn_sample_kernel(x_ref, ck_ref, cv_ref, h_ref, wo_ref, inv_ref, sgn_ref, sink_ref, lng_ref, lnb_ref,
                        y_ref, knew_ref, vnew_ref, qs, ks, vs, os_, *, n_batch, n_new, n_cache):
    rows = n_batch * n_new
    hq_ref, hk_ref, hv_ref, hg_ref = _split_attn_w_in(h_ref)
    x = x_ref[...]
    pos = (PAST_LEN + lax.broadcasted_iota(jnp.int32, (rows, 1), 0) % n_new).astype(F32)
    cosv, sinv = _rope_tables(pos, inv_ref, sgn_ref)

    for j in range(ATTN_WIDTH // LANES):
        sl = slice(j * LANES, (j + 1) * LANES)
        qs[:, sl] = (_rope(hq_ref[:, sl], cosv, sinv) * Q_SCALE).astype(BF16)
    k = hk_ref[...]
    v = hv_ref[...]
    k_t = jnp.concatenate([_rope(k[:, c * LANES:(c + 1) * LANES], cosv, sinv)
                           for c in range(KV_WIDTH // LANES)], axis=1).T
    v_t = v.T
    for j in range(rows // LANES):
        ks[j] = k_t[:, j * LANES:(j + 1) * LANES]
        vs[j] = v_t[:, j * LANES:(j + 1) * LANES]

    per_tile = LANES // n_new
    lane = lax.broadcasted_iota(jnp.int32, (1, LANES), 1)
    zeros = jnp.zeros((HEAD_DIM, 2 * LANES), BF16)
    dims_nt = (((1,), (1,)), ((), ()))

    def batch_body(b, carry):
        r0 = pl.multiple_of(b * n_new, n_new)
        off = (b % per_tile) * n_new
        k_old, v_old = ck_ref[b], cv_ref[b]
        k_new, v_new = ks[b // per_tile], vs[b // per_tile]
        mine = (lane >= off) & (lane < off + n_new)

        to_end = n_cache - n_new - off
        to_end = jnp.where(to_end < 0, to_end + LANES, to_end)
        for old, new, out_ref in ((k_old, k_new, knew_ref), (v_old, v_new, vnew_ref)):
            out_ref[b] = jnp.where(lane < n_cache - n_new,
                                   pltpu.roll(old, n_cache - n_new, 1), pltpu.roll(new, to_end, 1))

        bias = jnp.concatenate([jnp.zeros((1, n_cache), F32), jnp.where(mine, 0.0, -jnp.inf)], axis=1)
        k_all = jnp.concatenate([k_old, k_new], axis=1).astype(BF16)
        v_all = jnp.concatenate([v_old, jnp.where(mine, v_new, 0.0)], axis=1).astype(BF16)

        scores = []
        for kv in range(N_KV_HEADS):
            q2 = qs[pl.ds(r0, n_new), kv * 2 * LANES:(kv + 1) * 2 * LANES]
            qst = jnp.concatenate([q2[:, :LANES], q2[:, LANES:]], axis=0)
            k_kv = k_all[kv * HEAD_DIM:(kv + 1) * HEAD_DIM, :]
            scores.append((jnp.dot(qst, jnp.concatenate([k_kv, zeros], axis=0), preferred_element_type=F32),
                           jnp.dot(qst, jnp.concatenate([zeros, k_kv], axis=0), preferred_element_type=F32)))
        probs = []
        for kv, (s_a, s_b) in enumerate(scores):
            sink_a, sink_b = _sink_cols(sink_ref, kv, n_new)
            probs.append(_sink_softmax(s_a, sink_a, bias) + _sink_softmax(s_b, sink_b, bias))
        for kv, (p_a, inv_a, p_b, inv_b) in enumerate(probs):
            v_kv = v_all[kv * HEAD_DIM:(kv + 1) * HEAD_DIM, :]
            o_a = lax.dot_general(p_a, jnp.concatenate([v_kv, zeros], axis=0), dims_nt,
                                  preferred_element_type=F32)
            o_b = lax.dot_general(p_b, jnp.concatenate([zeros, v_kv], axis=0), dims_nt,
                                  preferred_element_type=F32)
            o = o_a * inv_a + o_b * inv_b
            os_[pl.ds(r0, n_new), kv * 2 * LANES:kv * 2 * LANES + LANES] = o[:n_new]
            os_[pl.ds(r0, n_new), kv * 2 * LANES + LANES:(kv + 1) * 2 * LANES] = o[n_new:]
        return carry

    lax.fori_loop(0, n_batch, batch_body, 0)

    z = (os_[...] * _silu(hg_ref[...])).astype(BF16)
    y = jnp.dot(z, wo_ref[...], preferred_element_type=F32)
    y_ref[...] = _layer_norm(DEEPNORM_ALPHA * x + y, lng_ref[...], lnb_ref[...])


def _depthwise(u_scr, c_scr, wdw_ref, bdw_ref, base, out_base, n_rows):
    off = CONV_HIST - (CONV_WIDTH - 1)
    for l in range(D_MODEL // LANES):
        sl = slice(l * LANES, (l + 1) * LANES)
        acc = jnp.zeros((n_rows, LANES), F32)
        for j in range(CONV_WIDTH):
            acc = acc + wdw_ref[j:j + 1, sl] * u_scr[l, pl.ds(base + j + off, n_rows, stride=1), :]
        c_scr[pl.ds(out_base, n_rows), sl] = acc + bdw_ref[:, sl]


def _conv_gate(c, g, cg_ref, cb_ref):
    cn = _layer_norm(c, cg_ref[...], cb_ref[...])
    return (_silu(cn) * _silu(g)).astype(BF16)


def _split_conv_w_in(w_ref):
    return tuple(w_ref.at[:, n * D_MODEL:(n + 1) * D_MODEL] for n in range(3))


def _conv_prompt_kernel(x_ref, w_ref, wo_ref, wdw_ref, bdw_ref, cg_ref, cb_ref,
                        lng_ref, lnb_ref,
                        y_ref, state_ref,
                        u_scr, c_scr):
    i = pl.program_id(0)
    last = pl.num_programs(0) - 1
    tm = x_ref.shape[0]
    wa_ref, wb_ref, wg_ref = _split_conv_w_in(w_ref)
    n_tiles = D_MODEL // LANES

    @pl.when(i == 0)
    def _():
        u_scr[:, 0:CONV_HIST, :] = jnp.zeros((n_tiles, CONV_HIST, LANES), F32)

    x = x_ref[...]
    xb = x.astype(BF16)
    a = jnp.dot(xb, wa_ref[...], preferred_element_type=F32)
    b = jnp.dot(xb, wb_ref[...], preferred_element_type=F32)
    u = a * jax.nn.sigmoid(b)
    for l in range(n_tiles):
        u_scr[l, CONV_HIST:CONV_HIST + tm, :] = u[:, l * LANES:(l + 1) * LANES]

    def row_body(rt, carry):
        base = pl.multiple_of(rt * CONV_ROW_TILE, CONV_ROW_TILE)
        _depthwise(u_scr, c_scr, wdw_ref, bdw_ref, base, base, CONV_ROW_TILE)
        return carry

    lax.fori_loop(0, tm // CONV_ROW_TILE, row_body, 0)

    @pl.when(i == last)
    def _():
        for l in range(n_tiles):
            state_ref[:, l * LANES:(l + 1) * LANES] = (
                u_scr[l, CONV_HIST + tm - (CONV_WIDTH - 1):CONV_HIST + tm, :])

    u_scr[:, 0:CONV_HIST, :] = u_scr[:, tm:tm + CONV_HIST, :]

    g = jnp.dot(xb, wg_ref[...], preferred_element_type=F32)
    z = _conv_gate(c_scr[...], g, cg_ref, cb_ref)
    y = jnp.dot(z, wo_ref[...], preferred_element_type=F32)
    y_ref[...] = _layer_norm(DEEPNORM_ALPHA * x + y, lng_ref[...], lnb_ref[...])


def _conv_sample_kernel(x_ref, st_ref, w_ref, wo_ref, wdw_ref, bdw_ref, cg_ref, cb_ref,
                        lng_ref, lnb_ref,
                        y_ref, state_ref,
                        u_scr, c_slab, c_scr, *, n_batch, n_new):
    hist = CONV_WIDTH - 1
    pad = CONV_HIST - hist
    rows = n_batch * n_new
    wa_ref, wb_ref, wg_ref = _split_conv_w_in(w_ref)
    x = x_ref[...]
    xb = x.astype(BF16)
    a = jnp.dot(xb, wa_ref[...], preferred_element_type=F32)
    b = jnp.dot(xb, wb_ref[...], preferred_element_type=F32)
    u = a * jax.nn.sigmoid(b)
    first_new = (pad + hist) * n_batch
    for l in range(D_MODEL // LANES):
        sl = slice(l * LANES, (l + 1) * LANES)
        u_scr[l, 0:pad * n_batch, :] = jnp.zeros((pad * n_batch, LANES), F32)
        u_scr[l, pad * n_batch:first_new, :] = st_ref[:, sl]
        for bi in range(n_batch):
            u_scr[l, pl.ds(first_new + bi, n_new, stride=n_batch), :] = u[bi * n_new:(bi + 1) * n_new, sl]
        state_ref[:, sl] = u_scr[l, (pad + n_new) * n_batch:(pad + n_new + hist) * n_batch, :]

    def row_body(rt, carry):
        base = pl.multiple_of(rt * CONV_ROW_TILE, CONV_ROW_TILE)
        for l in range(D_MODEL // LANES):
            sl = slice(l * LANES, (l + 1) * LANES)
            acc = jnp.zeros((CONV_ROW_TILE, LANES), F32)
            for j in range(CONV_WIDTH):
                acc = acc + wdw_ref[j:j + 1, sl] * u_scr[l, pl.ds(base + (j + pad) * n_batch, CONV_ROW_TILE), :]
            c_slab[l, pl.ds(base, CONV_ROW_TILE), :] = acc + bdw_ref[:, sl]
        return carry

    lax.fori_loop(0, rows // CONV_ROW_TILE, row_body, 0)

    for l in range(D_MODEL // LANES):
        for bi in range(n_batch):
            c_scr[bi * n_new:(bi + 1) * n_new, l * LANES:(l + 1) * LANES] = (
                c_slab[l, pl.ds(bi, n_new, stride=n_batch), :])

    g = jnp.dot(xb, wg_ref[...], preferred_element_type=F32)
    z = _conv_gate(c_scr[...], g, cg_ref, cb_ref)
    y = jnp.dot(z, wo_ref[...], preferred_element_type=F32)
    y_ref[...] = _layer_norm(DEEPNORM_ALPHA * x + y, lng_ref[...], lnb_ref[...])


def _const_spec(shape):
    return pl.BlockSpec(shape, lambda i: (0,) * len(shape), pipeline_mode=pl.Buffered(1))


def _const_out_spec(shape):
    return pl.BlockSpec(shape, lambda i: (0,) * len(shape), pipeline_mode=pl.Buffered(1))


def _vmem_spec():
    return pl.BlockSpec(memory_space=pltpu.VMEM)


def _smem_spec():
    return pl.BlockSpec(memory_space=pltpu.SMEM)


def _rope_lane_consts():
    half = ROT_DIM // 2
    inv = ROPE_THETA ** (-jnp.arange(half, dtype=F32) * (2.0 / ROT_DIM))
    zeros = jnp.zeros((HEAD_DIM - ROT_DIM,), F32)
    inv_head = jnp.concatenate([inv, inv, zeros])
    sgn_head = jnp.concatenate([-jnp.ones((half,), F32), jnp.ones((half,), F32), zeros])
    reps = LANES // HEAD_DIM
    return jnp.tile(inv_head, reps)[None, :], jnp.tile(sgn_head, reps)[None, :]


def _attn_layer(xp, xs, ck, cv, w_in, sink, w_out, ln_g, ln_b, next_w_in, next_w_out):
    seq = xp.shape[0]
    n_batch, n_cache = ck.shape[0], ck.shape[2]
    n_new = xs.shape[0] // n_batch
    assert n_cache == LANES and LANES % n_new == 0 and xs.shape[0] % LANES == 0
    inv, sgn = _rope_lane_consts()
    lng, lnb = ln_g[None, :], ln_b[None, :]
    params = pltpu.CompilerParams(dimension_semantics=("arbitrary",), vmem_limit_bytes=VMEM_LIMIT_BYTES)

    rows = xs.shape[0]
    k_rows = 2 * LANES
    h, wi, wo = pl.pallas_call(
        _attn_sample_proj_kernel,
        grid=(D_MODEL // k_rows,),
        in_specs=[pl.BlockSpec((rows, k_rows), lambda k: (0, k)),
                  pl.BlockSpec((k_rows, w_in.shape[1]), lambda k: (k, 0)),
                  pl.BlockSpec((k_rows, w_out.shape[1]), lambda k: (k, 0))],
        out_specs=[_const_out_spec((rows, w_in.shape[1])),
                   pl.BlockSpec((k_rows, w_in.shape[1]), lambda k: (k, 0)),
                   pl.BlockSpec((k_rows, w_out.shape[1]), lambda k: (k, 0))],
        out_shape=[jax.ShapeDtypeStruct((rows, w_in.shape[1]), F32),
                   jax.ShapeDtypeStruct(w_in.shape, BF16),
                   jax.ShapeDtypeStruct(w_out.shape, BF16)],
        compiler_params=params,
        name="attn_sample_proj",
    )(xs, w_in, w_out)

    new_t = pltpu.VMEM((rows // LANES, KV_WIDTH, LANES), F32)
    ys, ksn, vsn = pl.pallas_call(
        functools.partial(_attn_sample_kernel, n_batch=n_batch, n_new=n_new, n_cache=n_cache),
        in_specs=[_vmem_spec()] * 7 + [_smem_spec(), _vmem_spec(), _vmem_spec()],
        out_specs=[_vmem_spec()] * 3,
        out_shape=[jax.ShapeDtypeStruct((rows, D_MODEL), F32),
                   jax.ShapeDtypeStruct(ck.shape, F32),
                   jax.ShapeDtypeStruct(cv.shape, F32)],
        scratch_shapes=[pltpu.VMEM((rows, ATTN_WIDTH), BF16), new_t, new_t,
                        pltpu.VMEM((rows, ATTN_WIDTH), F32)],
        compiler_params=pltpu.CompilerParams(vmem_limit_bytes=VMEM_LIMIT_BYTES),
        name="attn_sample",
    )(xs, ck, cv, h, wo, inv, sgn, sink, lng, lnb)

    tm = ROW_BLOCK
    n_steps = seq // tm
    w_rows = next_w_in.shape[0] // n_steps
    assert next_w_in.shape[0] == next_w_out.shape[0] == w_rows * n_steps and w_rows % (2 * SUBLANES) == 0
    kv_scr = pltpu.VMEM((WIN_ROWS + tm, N_KV_HEADS * LANES), BF16)
    yp, kp, vp, next_wi, next_wo = pl.pallas_call(
        _attn_prompt_kernel,
        grid=(n_steps,),
        in_specs=[pl.BlockSpec((tm, D_MODEL), lambda i: (i, 0)),
                  _const_spec(wi.shape), _const_spec(wo.shape),
                  _const_spec(inv.shape), _const_spec(sgn.shape), _smem_spec(),
                  _const_spec(lng.shape), _const_spec(lnb.shape),
                  pl.BlockSpec((w_rows, next_w_in.shape[1]), lambda i: (i, 0)),
                  pl.BlockSpec((w_rows, next_w_out.shape[1]), lambda i: (i, 0))],
        out_specs=[pl.BlockSpec((tm, D_MODEL), lambda i: (i, 0)),
                   pl.BlockSpec((KV_WIDTH, WIN_ROWS), lambda i: (0, 0)),
                   pl.BlockSpec((KV_WIDTH, WIN_ROWS), lambda i: (0, 0)),
                   pl.BlockSpec((w_rows, next_w_in.shape[1]), lambda i: (i, 0)),
                   pl.BlockSpec((w_rows, next_w_out.shape[1]), lambda i: (i, 0))],
        out_shape=[jax.ShapeDtypeStruct((seq, D_MODEL), F32),
                   jax.ShapeDtypeStruct((KV_WIDTH, WIN_ROWS), F32),
                   jax.ShapeDtypeStruct((KV_WIDTH, WIN_ROWS), F32),
                   jax.ShapeDtypeStruct(next_w_in.shape, BF16),
                   jax.ShapeDtypeStruct(next_w_out.shape, BF16)],
        scratch_shapes=[pltpu.VMEM((tm, ATTN_WIDTH), BF16), kv_scr, kv_scr, kv_scr, kv_scr,
                        pltpu.VMEM((tm, ATTN_WIDTH), F32), pltpu.VMEM((4, tm, LANES), F32)],
        compiler_params=params,
        name="attn_prompt",
    )(xp, wi, wo, inv, sgn, sink, lng, lnb, next_w_in, next_w_out)

    return yp, ys, kp, vp, ksn, vsn, next_wi, next_wo


def _conv_layer(xp, xs, state, n_batch, wi, w_dw, b_dw, cg, cb, wo, ln_g, ln_b):
    seq = xp.shape[0]
    n_new = xs.shape[0] // n_batch
    hist = CONV_WIDTH - 1
    assert state.shape[0] == hist * n_batch and xs.shape[0] % CONV_ROW_TILE == 0
    vecs = [b_dw[None, :], cg[None, :], cb[None, :], ln_g[None, :], ln_b[None, :]]

    tm = ROW_BLOCK
    yp, cp = pl.pallas_call(
        _conv_prompt_kernel,
        grid=(seq // tm,),
        in_specs=[pl.BlockSpec((tm, D_MODEL), lambda i: (i, 0)),
                  _const_spec(wi.shape), _const_spec(wo.shape), _const_spec(w_dw.shape)]
                 + [_const_spec(t.shape) for t in vecs],
        out_specs=[pl.BlockSpec((tm, D_MODEL), lambda i: (i, 0)),
                   pl.BlockSpec((hist, D_MODEL), lambda i: (0, 0))],
        out_shape=[jax.ShapeDtypeStruct((seq, D_MODEL), F32),
                   jax.ShapeDtypeStruct((hist, D_MODEL), F32)],
        scratch_shapes=[pltpu.VMEM((D_MODEL // LANES, CONV_HIST + tm, LANES), F32),
                        pltpu.VMEM((tm, D_MODEL), F32)],
        compiler_params=pltpu.CompilerParams(dimension_semantics=("arbitrary",),
                                             vmem_limit_bytes=VMEM_LIMIT_BYTES),
        name="conv_prompt",
    )(xp, wi, wo, w_dw, *vecs)

    rows = xs.shape[0]
    ys, cs = pl.pallas_call(
        functools.partial(_conv_sample_kernel, n_batch=n_batch, n_new=n_new),
        in_specs=[_vmem_spec()] * 10,
        out_specs=[_vmem_spec()] * 2,
        out_shape=[jax.ShapeDtypeStruct((rows, D_MODEL), F32),
                   jax.ShapeDtypeStruct(state.shape, F32)],
        scratch_shapes=[pltpu.VMEM((D_MODEL // LANES, n_batch * (CONV_HIST + n_new), LANES), F32),
                        pltpu.VMEM((D_MODEL // LANES, rows, LANES), F32),
                        pltpu.VMEM((rows, D_MODEL), F32)],
        compiler_params=pltpu.CompilerParams(vmem_limit_bytes=VMEM_LIMIT_BYTES),
        name="conv_sample",
    )(xs, state, wi, wo, w_dw, *vecs)
    return yp, ys, cp, cs


def kernel(x_prompt, x_sample, cache_k, cache_v, state_conv, attn_w_in, attn_sink, attn_w_out,
           conv_w_in, conv_w_dw, conv_b_dw, conv_ln_g, conv_ln_b, conv_w_out, post_ln_g, post_ln_b):
    batch, seq, _ = x_prompt.shape
    dec_batch, dec_seq, _ = x_sample.shape
    wc = cache_k.shape[2]
    assert batch == 1 and wc == WIN_ROWS and seq % ROW_BLOCK == 0
    xp = x_prompt.reshape(seq, D_MODEL)
    xs = x_sample.reshape(dec_batch * dec_seq, D_MODEL)

    def feature_major(c):
        return jnp.transpose(c, (0, 2, 3, 1)).reshape(c.shape[0], KV_WIDTH, wc)

    def frame_major(c_t, n):
        return jnp.transpose(c_t.reshape(n, N_KV_HEADS, HEAD_DIM, wc), (0, 3, 1, 2))[None]

    xp, xs, kp, vp, ksn, vsn, conv_wi, conv_wo = _attn_layer(
        xp, xs, feature_major(cache_k[0]), feature_major(cache_v[0]),
        attn_w_in[0], attn_sink[0], attn_w_out[0], post_ln_g[0], post_ln_b[0],
        conv_w_in[0], conv_w_out[0])
    hist = state_conv.shape[2]
    state = jnp.transpose(state_conv[0], (1, 0, 2)).reshape(hist * dec_batch, D_MODEL)
    xp, xs, cp, cs = _conv_layer(
        xp, xs, state, dec_batch, conv_wi, conv_w_dw[0], conv_b_dw[0], conv_ln_g[0], conv_ln_b[0],
        conv_wo, post_ln_g[1], post_ln_b[1])
    cs = jnp.transpose(cs.reshape(hist, dec_batch, D_MODEL), (1, 0, 2))

    return (xp.reshape(batch, seq, D_MODEL), xs.reshape(dec_batch, dec_seq, D_MODEL),
            frame_major(kp, 1), frame_major(vp, 1),
            frame_major(ksn, dec_batch), frame_major(vsn, dec_batch),
            cp[None, None], cs[None])
```

```python
import functools

import jax
import jax.numpy as jnp
from jax import lax
from jax.experimental import pallas as pl
from jax.experimental.pallas import tpu as pltpu

F32 = jnp.float32
BF16 = jnp.bfloat16

D_MODEL = 2048
DEPTH = 2
PAST_LEN = 1024
CHUNK = 64
HEAD_DIM = 64
N_HEADS = D_MODEL // HEAD_DIM
N_KV_HEADS = 8
GROUP = N_HEADS // N_KV_HEADS
ATTN_WIDTH = N_HEADS * HEAD_DIM
KV_WIDTH = N_KV_HEADS * HEAD_DIM
WIN_CHUNKS = 2
WIN_ROWS = WIN_CHUNKS * CHUNK
ROT_DIM = HEAD_DIM // 4
ROPE_THETA = 500000.0
CONV_WIDTH = 31
LN_EPS = 1e-5
DEEPNORM_ALPHA = (2.0 * DEPTH) ** 0.25
ATTN_SCALE = HEAD_DIM ** -0.5
LOG2E = 1.4426950408889634
Q_SCALE = ATTN_SCALE * LOG2E

LANES = 128
SUBLANES = 8
VMEM_LIMIT_BYTES = 60 * 1024 * 1024
ROW_BLOCK = 256
CONV_HIST = 32
CONV_ROW_TILE = 32
PAIR_ROWS = 2 * CHUNK
PAIR_KEYS = WIN_ROWS + PAIR_ROWS


def _layer_norm(r, g, b):
    mu = jnp.mean(r, axis=-1, keepdims=True)
    d = r - mu
    var = jnp.mean(d * d, axis=-1, keepdims=True)
    return d * lax.rsqrt(var + LN_EPS) * g + b


def _silu(t):
    return t * jax.nn.sigmoid(t)


def _rope_tables(pos, inv_ref, sgn_ref):
    ang = pos * inv_ref[...]
    sin = jnp.sin(ang)
    return jnp.cos(ang), sin if sgn_ref is None else sin * sgn_ref[...]


def _rope(t, cosv, sinv):
    lane = lax.broadcasted_iota(jnp.int32, (1, LANES), 1) % HEAD_DIM
    partner = jnp.where(lane < ROT_DIM // 2,
                        pltpu.roll(t, LANES - ROT_DIM // 2, 1),
                        pltpu.roll(t, ROT_DIM // 2, 1))
    return t * cosv + partner * sinv


def _lo_hi(col):
    low = lax.broadcasted_iota(jnp.int32, (1, LANES), 1) < HEAD_DIM
    swapped = pltpu.roll(col, HEAD_DIM, 1)
    zero = jnp.zeros_like(col)
    even_lo = jnp.where(low, col, zero).astype(BF16)
    even_hi = jnp.where(low, zero, swapped).astype(BF16)
    odd_lo = jnp.where(low, swapped, zero).astype(BF16)
    odd_hi = jnp.where(low, zero, col).astype(BF16)
    return even_lo, even_hi, odd_lo, odd_hi


def _scores(q2, k_lo, k_hi):
    qst = jnp.concatenate([q2[:, :LANES], q2[:, LANES:]], axis=0)
    dims = (((1,), (1,)), ((), ()))
    return (lax.dot_general(qst, k_lo, dims, preferred_element_type=F32),
            lax.dot_general(qst, k_hi, dims, preferred_element_type=F32))


def _sink_softmax(s, sk, bias):
    if bias is not None:
        s = s + bias
    m = jnp.maximum(jnp.max(s, axis=-1, keepdims=True), sk)
    p = jnp.exp2(s - m)
    denom = jnp.sum(p, axis=-1, keepdims=True) + jnp.exp2(sk - m)
    return p.astype(BF16), 1.0 / denom


def _weighted_values(p_a, inv_a, p_b, inv_b, v_lo, v_hi):
    o_a = jnp.dot(p_a, v_lo, preferred_element_type=F32)
    o_b = jnp.dot(p_b, v_hi, preferred_element_type=F32)
    return o_a * inv_a + o_b * inv_b


def _attend_heads(q_of, k_of, v_of, sink_ref, q_rows, bias):
    scores = [_scores(q_of(kv), *k_of(kv)) for kv in range(N_KV_HEADS)]
    probs = []
    for kv, (s_a, s_b) in enumerate(scores):
        sink_a, sink_b = _sink_cols(sink_ref, kv, q_rows)
        probs.append(_sink_softmax(s_a, sink_a, bias) + _sink_softmax(s_b, sink_b, bias))
    return [_weighted_values(*probs[kv], *v_of(kv)) for kv in range(N_KV_HEADS)]


def _sink_cols(sink_ref, kv, q_rows):
    row = lax.broadcasted_iota(jnp.int32, (2 * q_rows, 1), 0)
    first = row < q_rows
    h = kv * GROUP
    return (jnp.where(first, sink_ref[h + 0] * LOG2E, sink_ref[h + 2] * LOG2E),
            jnp.where(first, sink_ref[h + 1] * LOG2E, sink_ref[h + 3] * LOG2E))


def _split_attn_w_in(w_ref):
    k0, v0, g0 = ATTN_WIDTH, ATTN_WIDTH + KV_WIDTH, ATTN_WIDTH + 2 * KV_WIDTH
    return w_ref.at[:, 0:k0], w_ref.at[:, k0:v0], w_ref.at[:, v0:g0], w_ref.at[:, g0:g0 + ATTN_WIDTH]


def _attn_prompt_kernel(x_ref, w_ref, wo_ref, inv_ref, sgn_ref,
                        sink_ref, lng_ref, lnb_ref, next_wi_ref, next_wo_ref,
                        y_ref, knew_ref, vnew_ref, next_wi_bf_ref, next_wo_bf_ref,
                        qs, klo, khi, vlo, vhi, os_, rope_tab):
    i = pl.program_id(0)
    last = pl.num_programs(0) - 1
    tm = x_ref.shape[0]
    next_wi_bf_ref[...] = next_wi_ref[...].astype(BF16)
    next_wo_bf_ref[...] = next_wo_ref[...].astype(BF16)
    wq_ref, wk_ref, wv_ref, wg_ref = _split_attn_w_in(w_ref)

    @pl.when(i == 0)
    def _():
        for ref in (klo, khi, vlo, vhi):
            ref[0:WIN_ROWS, :] = jnp.zeros((WIN_ROWS, ref.shape[1]), BF16)
        off = lax.broadcasted_iota(jnp.int32, (tm, 1), 0).astype(F32)
        cos_r, sin_r = _rope_tables(off, inv_ref, None)
        rope_tab[0] = cos_r
        rope_tab[1] = sin_r
        rope_tab[2] = cos_r * sgn_ref[...]
        rope_tab[3] = sin_r * sgn_ref[...]

    x = x_ref[...]
    xb = x.astype(BF16)
    start = jnp.full((1, 1), i * tm, jnp.int32).astype(F32)
    cos_s, sin_s = _rope_tables(start, inv_ref, None)
    cosv = cos_s * rope_tab[0] - sin_s * rope_tab[1]
    sinv = sin_s * rope_tab[2] + cos_s * rope_tab[3]

    q = jnp.dot(xb, wq_ref[...], preferred_element_type=F32)
    cos_q, sin_q = cosv * Q_SCALE, sinv * Q_SCALE
    for j in range(ATTN_WIDTH // LANES):
        sl = slice(j * LANES, (j + 1) * LANES)
        qs[:, sl] = _rope(q[:, sl], cos_q, sin_q).astype(BF16)

    k = jnp.dot(xb, wk_ref[...], preferred_element_type=F32)
    v = jnp.dot(xb, wv_ref[...], preferred_element_type=F32)
    new_rows = slice(WIN_ROWS, WIN_ROWS + tm)
    for c in range(KV_WIDTH // LANES):
        sl = slice(c * LANES, (c + 1) * LANES)
        kc = _rope(k[:, sl], cosv, sinv)
        vc = v[:, sl]

        @pl.when(i == last)
        def _():
            knew_ref[sl, :] = kc[tm - WIN_ROWS:, :].T
            vnew_ref[sl, :] = vc[tm - WIN_ROWS:, :].T

        ev = slice(2 * c * LANES, (2 * c + 1) * LANES)
        od = slice((2 * c + 1) * LANES, (2 * c + 2) * LANES)
        klo[new_rows, ev], khi[new_rows, ev], klo[new_rows, od], khi[new_rows, od] = _lo_hi(kc)
        vlo[new_rows, ev], vhi[new_rows, ev], vlo[new_rows, od], vhi[new_rows, od] = _lo_hi(vc)

    row = lax.broadcasted_iota(jnp.int32, (2 * PAIR_ROWS, 1), 0)
    col = lax.broadcasted_iota(jnp.int32, (1, PAIR_KEYS), 1)
    first_col = jnp.where(row % PAIR_ROWS >= CHUNK, CHUNK, 0)
    in_window = (col >= first_col) & (col < first_col + (PAIR_KEYS - CHUNK))
    window_bias = jnp.where(in_window, 0.0, -jnp.inf)

    def pair_body(p, carry):
        r0 = pl.multiple_of(p * PAIR_ROWS, PAIR_ROWS)
        first_key_chunk = i * (tm // CHUNK) + 2 * p - WIN_CHUNKS
        started = col >= jnp.maximum(0, -first_key_chunk) * CHUNK
        bias = window_bias + jnp.where(started, 0.0, -jnp.inf)
        keys = pl.ds(r0, PAIR_KEYS)

        def head_lanes(kv):
            return slice(kv * LANES, (kv + 1) * LANES)

        outs = _attend_heads(
            lambda kv: qs[pl.ds(r0, PAIR_ROWS), kv * 2 * LANES:(kv + 1) * 2 * LANES],
            lambda kv: (klo[keys, head_lanes(kv)], khi[keys, head_lanes(kv)]),
            lambda kv: (vlo[keys, head_lanes(kv)], vhi[keys, head_lanes(kv)]),
            sink_ref, PAIR_ROWS, bias)
        for kv, o in enumerate(outs):
            os_[pl.ds(r0, PAIR_ROWS), kv * 2 * LANES:kv * 2 * LANES + LANES] = o[:PAIR_ROWS]
            os_[pl.ds(r0, PAIR_ROWS), kv * 2 * LANES + LANES:(kv + 1) * 2 * LANES] = o[PAIR_ROWS:]
        return carry

    lax.fori_loop(0, tm // PAIR_ROWS, pair_body, 0)

    for ref in (klo, khi, vlo, vhi):
        ref[0:WIN_ROWS, :] = ref[tm:tm + WIN_ROWS, :]

    g = jnp.dot(xb, wg_ref[...], preferred_element_type=F32)
    z = (os_[...] * _silu(g)).astype(BF16)
    y = jnp.dot(z, wo_ref[...], preferred_element_type=F32)
    y_ref[...] = _layer_norm(DEEPNORM_ALPHA * x + y, lng_ref[...], lnb_ref[...])


def _attn_sample_proj_kernel(xk_ref, w_ref, wo_ref, h_ref, w_bf_ref, wo_bf_ref):
    kstep = pl.program_id(0)
    w_bf = w_ref[...].astype(BF16)
    w_bf_ref[...] = w_bf
    wo_bf_ref[...] = wo_ref[...].astype(BF16)
    part = jnp.dot(xk_ref[...].astype(BF16), w_bf, preferred_element_type=F32)

    @pl.when(kstep == 0)
    def _():
        h_ref[...] = part

    @pl.when(kstep > 0)
    def _():
        h_ref[...] += part


def _attn_sample_kernel(x_ref, ck_ref, cv_ref, h_ref, wo_ref, inv_ref, sgn_ref, sink_ref, lng_ref, lnb_ref,
                        y_ref, knew_ref, vnew_ref, qs, ks, vs, os_, *, n_batch, n_new, n_cache):
    rows = n_batch * n_new
    hq_ref, hk_ref, hv_ref, hg_ref = _split_attn_w_in(h_ref)
    x = x_ref[...]
    pos = (PAST_LEN + lax.broadcasted_iota(jnp.int32, (rows, 1), 0) % n_new).astype(F32)
    cosv, sinv = _rope_tables(pos, inv_ref, sgn_ref)

    for j in range(ATTN_WIDTH // LANES):
        sl = slice(j * LANES, (j + 1) * LANES)
        qs[:, sl] = (_rope(hq_ref[:, sl], cosv, sinv) * Q_SCALE).astype(BF16)
    k = hk_ref[...]
    v = hv_ref[...]
    k_t = jnp.concatenate([_rope(k[:, c * LANES:(c + 1) * LANES], cosv, sinv)
                           for c in range(KV_WIDTH // LANES)], axis=1).T
    v_t = v.T
    for j in range(rows // LANES):
        ks[j] = k_t[:, j * LANES:(j + 1) * LANES]
        vs[j] = v_t[:, j * LANES:(j + 1) * LANES]

    per_tile = LANES // n_new
    lane = lax.broadcasted_iota(jnp.int32, (1, LANES), 1)
    zeros = jnp.zeros((HEAD_DIM, 2 * LANES), BF16)
    dims_nt = (((1,), (1,)), ((), ()))

    def batch_body(b, carry):
        r0 = pl.multiple_of(b * n_new, n_new)
        off = (b % per_tile) * n_new
        k_old, v_old = ck_ref[b], cv_ref[b]
        k_new, v_new = ks[b // per_tile], vs[b // per_tile]
        mine = (lane >= off) & (lane < off + n_new)

        to_end = n_cache - n_new - off
        to_end = jnp.where(to_end < 0, to_end + LANES, to_end)
        for old, new, out_ref in ((k_old, k_new, knew_ref), (v_old, v_new, vnew_ref)):
            out_ref[b] = jnp.where(lane < n_cache - n_new,
                                   pltpu.roll(old, n_cache - n_new, 1), pltpu.roll(new, to_end, 1))

        bias = jnp.concatenate([jnp.zeros((1, n_cache), F32), jnp.where(mine, 0.0, -jnp.inf)], axis=1)
        k_all = jnp.concatenate([k_old, k_new], axis=1).astype(BF16)
        v_all = jnp.concatenate([v_old, jnp.where(mine, v_new, 0.0)], axis=1).astype(BF16)

        scores = []
        for kv in range(N_KV_HEADS):
            q2 = qs[pl.ds(r0, n_new), kv * 2 * LANES:(kv + 1) * 2 * LANES]
            qst = jnp.concatenate([q2[:, :LANES], q2[:, LANES:]], axis=0)
            k_kv = k_all[kv * HEAD_DIM:(kv + 1) * HEAD_DIM, :]
            scores.append((jnp.dot(qst, jnp.concatenate([k_kv, zeros], axis=0), preferred_element_type=F32),
                           jnp.dot(qst, jnp.concatenate([zeros, k_kv], axis=0), preferred_element_type=F32)))
        probs = []
        for kv, (s_a, s_b) in enumerate(scores):
            sink_a, sink_b = _sink_cols(sink_ref, kv, n_new)
            probs.append(_sink_softmax(s_a, sink_a, bias) + _sink_softmax(s_b, sink_b, bias))
        for kv, (p_a, inv_a, p_b, inv_b) in enumerate(probs):
            v_kv = v_all[kv * HEAD_DIM:(kv + 1) * HEAD_DIM, :]
            o_a = lax.dot_general(p_a, jnp.concatenate([v_kv, zeros], axis=0), dims_nt,
                                  preferred_element_type=F32)
            o_b = lax.dot_general(p_b, jnp.concatenate([zeros, v_kv], axis=0), dims_nt,
                                  preferred_element_type=F32)
            o = o_a * inv_a + o_b * inv_b
            os_[pl.ds(r0, n_new), kv * 2 * LANES:kv * 2 * LANES + LANES] = o[:n_new]
            os_[pl.ds(r0, n_new), kv * 2 * LANES + LANES:(kv + 1) * 2 * LANES] = o[n_new:]
        return carry

    lax.fori_loop(0, n_batch, batch_body, 0)

    z = (os_[...] * _silu(hg_ref[...])).astype(BF16)
    y = jnp.dot(z, wo_ref[...], preferred_element_type=F32)
    y_ref[...] = _layer_norm(DEEPNORM_ALPHA * x + y, lng_ref[...], lnb_ref[...])


def _depthwise(u_scr, c_scr, wdw_ref, bdw_ref, base, out_base, n_rows):
    off = CONV_HIST - (CONV_WIDTH - 1)
    for l in range(D_MODEL // LANES):
        sl = slice(l * LANES, (l + 1) * LANES)
        acc = jnp.zeros((n_rows, LANES), F32)
        for j in range(CONV_WIDTH):
            acc = acc + wdw_ref[j:j + 1, sl] * u_scr[l, pl.ds(base + j + off, n_rows, stride=1), :]
        c_scr[pl.ds(out_base, n_rows), sl] = acc + bdw_ref[:, sl]


def _conv_gate(c, g, cg_ref, cb_ref):
    cn = _layer_norm(c, cg_ref[...], cb_ref[...])
    return (cn * g / ((1.0 + jnp.exp(-cn)) * (1.0 + jnp.exp(-g)))).astype(BF16)


def _split_conv_w_in(w_ref):
    return tuple(w_ref.at[:, n * D_MODEL:(n + 1) * D_MODEL] for n in range(3))


def _conv_prompt_kernel(x_ref, w_ref, wo_ref, wdw_ref, bdw_ref, cg_ref, cb_ref,
                        lng_ref, lnb_ref,
                        y_ref, state_ref,
                        u_scr, c_scr):
    i = pl.program_id(0)
    last = pl.num_programs(0) - 1
    tm = x_ref.shape[0]
    wa_ref, wb_ref, wg_ref = _split_conv_w_in(w_ref)
    n_tiles = D_MODEL // LANES

    @pl.when(i == 0)
    def _():
        u_scr[:, 0:CONV_HIST, :] = jnp.zeros((n_tiles, CONV_HIST, LANES), F32)

    x = x_ref[...]
    xb = x.astype(BF16)
    a = jnp.dot(xb, wa_ref[...], preferred_element_type=F32)
    b = jnp.dot(xb, wb_ref[...], preferred_element_type=F32)
    u = a * jax.nn.sigmoid(b)
    for l in range(n_tiles):
        u_scr[l, CONV_HIST:CONV_HIST + tm, :] = u[:, l * LANES:(l + 1) * LANES]

    def row_body(rt, carry):
        base = pl.multiple_of(rt * CONV_ROW_TILE, CONV_ROW_TILE)
        _depthwise(u_scr, c_scr, wdw_ref, bdw_ref, base, base, CONV_ROW_TILE)
        return carry

    lax.fori_loop(0, tm // CONV_ROW_TILE, row_body, 0)

    @pl.when(i == last)
    def _():
        for l in range(n_tiles):
            state_ref[:, l * LANES:(l + 1) * LANES] = (
                u_scr[l, CONV_HIST + tm - (CONV_WIDTH - 1):CONV_HIST + tm, :])

    u_scr[:, 0:CONV_HIST, :] = u_scr[:, tm:tm + CONV_HIST, :]

    g = jnp.dot(xb, wg_ref[...], preferred_element_type=F32)
    z = _conv_gate(c_scr[...], g, cg_ref, cb_ref)
    y = jnp.dot(z, wo_ref[...], preferred_element_type=F32)
    y_ref[...] = _layer_norm(DEEPNORM_ALPHA * x + y, lng_ref[...], lnb_ref[...])


def _conv_sample_kernel(x_ref, st_ref, w_ref, wo_ref, wdw_ref, bdw_ref, cg_ref, cb_ref,
                        lng_ref, lnb_ref,
                        y_ref, state_ref,
                        u_scr, c_slab, c_scr, *, n_batch, n_new):
    hist = CONV_WIDTH - 1
    pad = CONV_HIST - hist
    rows = n_batch * n_new
    wa_ref, wb_ref, wg_ref = _split_conv_w_in(w_ref)
    x = x_ref[...]
    xb = x.astype(BF16)
    a = jnp.dot(xb, wa_ref[...], preferred_element_type=F32)
    b = jnp.dot(xb, wb_ref[...], preferred_element_type=F32)
    u = a * jax.nn.sigmoid(b)
    first_new = (pad + hist) * n_batch
    for l in range(D_MODEL // LANES):
        sl = slice(l * LANES, (l + 1) * LANES)
        u_scr[l, 0:pad * n_batch, :] = jnp.zeros((pad * n_batch, LANES), F32)
        u_scr[l, pad * n_batch:first_new, :] = st_ref[:, sl]
        for bi in range(n_batch):
            u_scr[l, pl.ds(first_new + bi, n_new, stride=n_batch), :] = u[bi * n_new:(bi + 1) * n_new, sl]
        state_ref[:, sl] = u_scr[l, (pad + n_new) * n_batch:(pad + n_new + hist) * n_batch, :]

    def row_body(rt, carry):
        base = pl.multiple_of(rt * CONV_ROW_TILE, CONV_ROW_TILE)
        for l in range(D_MODEL // LANES):
            sl = slice(l * LANES, (l + 1) * LANES)
            acc = jnp.zeros((CONV_ROW_TILE, LANES), F32)
            for j in range(CONV_WIDTH):
                acc = acc + wdw_ref[j:j + 1, sl] * u_scr[l, pl.ds(base + (j + pad) * n_batch, CONV_ROW_TILE), :]
            c_slab[l, pl.ds(base, CONV_ROW_TILE), :] = acc + bdw_ref[:, sl]
        return carry

    lax.fori_loop(0, rows // CONV_ROW_TILE, row_body, 0)

    for l in range(D_MODEL // LANES):
        for bi in range(n_batch):
            c_scr[bi * n_new:(bi + 1) * n_new, l * LANES:(l + 1) * LANES] = (
                c_slab[l, pl.ds(bi, n_new, stride=n_batch), :])

    g = jnp.dot(xb, wg_ref[...], preferred_element_type=F32)
    z = _conv_gate(c_scr[...], g, cg_ref, cb_ref)
    y = jnp.dot(z, wo_ref[...], preferred_element_type=F32)
    y_ref[...] = _layer_norm(DEEPNORM_ALPHA * x + y, lng_ref[...], lnb_ref[...])


def _const_spec(shape):
    return pl.BlockSpec(shape, lambda i: (0,) * len(shape), pipeline_mode=pl.Buffered(1))


def _const_out_spec(shape):
    return pl.BlockSpec(shape, lambda i: (0,) * len(shape), pipeline_mode=pl.Buffered(1))


def _vmem_spec():
    return pl.BlockSpec(memory_space=pltpu.VMEM)


def _smem_spec():
    return pl.BlockSpec(memory_space=pltpu.SMEM)


def _rope_lane_consts():
    half = ROT_DIM // 2
    inv = ROPE_THETA ** (-jnp.arange(half, dtype=F32) * (2.0 / ROT_DIM))
    zeros = jnp.zeros((HEAD_DIM - ROT_DIM,), F32)
    inv_head = jnp.concatenate([inv, inv, zeros])
    sgn_head = jnp.concatenate([-jnp.ones((half,), F32), jnp.ones((half,), F32), zeros])
    reps = LANES // HEAD_DIM
    return jnp.tile(inv_head, reps)[None, :], jnp.tile(sgn_head, reps)[None, :]


def _attn_layer(xp, xs, ck, cv, w_in, sink, w_out, ln_g, ln_b, next_w_in, next_w_out):
    seq = xp.shape[0]
    n_batch, n_cache = ck.shape[0], ck.shape[2]
    n_new = xs.shape[0] // n_batch
    assert n_cache == LANES and LANES % n_new == 0 and xs.shape[0] % LANES == 0
    inv, sgn = _rope_lane_consts()
    lng, lnb = ln_g[None, :], ln_b[None, :]
    params = pltpu.CompilerParams(dimension_semantics=("arbitrary",), vmem_limit_bytes=VMEM_LIMIT_BYTES)

    rows = xs.shape[0]
    k_rows = 4 * LANES
    h, wi, wo = pl.pallas_call(
        _attn_sample_proj_kernel,
        grid=(D_MODEL // k_rows,),
        in_specs=[pl.BlockSpec((rows, k_rows), lambda k: (0, k)),
                  pl.BlockSpec((k_rows, w_in.shape[1]), lambda k: (k, 0)),
                  pl.BlockSpec((k_rows, w_out.shape[1]), lambda k: (k, 0))],
        out_specs=[_const_out_spec((rows, w_in.shape[1])),
                   pl.BlockSpec((k_rows, w_in.shape[1]), lambda k: (k, 0)),
                   pl.BlockSpec((k_rows, w_out.shape[1]), lambda k: (k, 0))],
        out_shape=[jax.ShapeDtypeStruct((rows, w_in.shape[1]), F32),
                   jax.ShapeDtypeStruct(w_in.shape, BF16),
                   jax.ShapeDtypeStruct(w_out.shape, BF16)],
        compiler_params=params,
        name="attn_sample_proj",
    )(xs, w_in, w_out)

    new_t = pltpu.VMEM((rows // LANES, KV_WIDTH, LANES), F32)
    ys, ksn, vsn = pl.pallas_call(
        functools.partial(_attn_sample_kernel, n_batch=n_batch, n_new=n_new, n_cache=n_cache),
        in_specs=[_vmem_spec()] * 7 + [_smem_spec(), _vmem_spec(), _vmem_spec()],
        out_specs=[_vmem_spec()] * 3,
        out_shape=[jax.ShapeDtypeStruct((rows, D_MODEL), F32),
                   jax.ShapeDtypeStruct(ck.shape, F32),
                   jax.ShapeDtypeStruct(cv.shape, F32)],
        scratch_shapes=[pltpu.VMEM((rows, ATTN_WIDTH), BF16), new_t, new_t,
                        pltpu.VMEM((rows, ATTN_WIDTH), F32)],
        compiler_params=pltpu.CompilerParams(vmem_limit_bytes=VMEM_LIMIT_BYTES),
        name="attn_sample",
    )(xs, ck, cv, h, wo, inv, sgn, sink, lng, lnb)

    tm = ROW_BLOCK
    n_steps = seq // tm
    w_rows = next_w_in.shape[0] // n_steps
    assert next_w_in.shape[0] == next_w_out.shape[0] == w_rows * n_steps and w_rows % (2 * SUBLANES) == 0
    kv_scr = pltpu.VMEM((WIN_ROWS + tm, N_KV_HEADS * LANES), BF16)
    yp, kp, vp, next_wi, next_wo = pl.pallas_call(
        _attn_prompt_kernel,
        grid=(n_steps,),
        in_specs=[pl.BlockSpec((tm, D_MODEL), lambda i: (i, 0)),
                  _const_spec(wi.shape), _const_spec(wo.shape),
                  _const_spec(inv.shape), _const_spec(sgn.shape), _smem_spec(),
                  _const_spec(lng.shape), _const_spec(lnb.shape),
                  pl.BlockSpec((w_rows, next_w_in.shape[1]), lambda i: (i, 0)),
                  pl.BlockSpec((w_rows, next_w_out.shape[1]), lambda i: (i, 0))],
        out_specs=[pl.BlockSpec((tm, D_MODEL), lambda i: (i, 0)),
                   pl.BlockSpec((KV_WIDTH, WIN_ROWS), lambda i: (0, 0)),
                   pl.BlockSpec((KV_WIDTH, WIN_ROWS), lambda i: (0, 0)),
                   pl.BlockSpec((w_rows, next_w_in.shape[1]), lambda i: (i, 0)),
                   pl.BlockSpec((w_rows, next_w_out.shape[1]), lambda i: (i, 0))],
        out_shape=[jax.ShapeDtypeStruct((seq, D_MODEL), F32),
                   jax.ShapeDtypeStruct((KV_WIDTH, WIN_ROWS), F32),
                   jax.ShapeDtypeStruct((KV_WIDTH, WIN_ROWS), F32),
                   jax.ShapeDtypeStruct(next_w_in.shape, BF16),
                   jax.ShapeDtypeStruct(next_w_out.shape, BF16)],
        scratch_shapes=[pltpu.VMEM((tm, ATTN_WIDTH), BF16), kv_scr, kv_scr, kv_scr, kv_scr,
                        pltpu.VMEM((tm, ATTN_WIDTH), F32), pltpu.VMEM((4, tm, LANES), F32)],
        compiler_params=params,
        name="attn_prompt",
    )(xp, wi, wo, inv, sgn, sink, lng, lnb, next_w_in, next_w_out)

    return yp, ys, kp, vp, ksn, vsn, next_wi, next_wo


def _conv_layer(xp, xs, state, n_batch, wi, w_dw, b_dw, cg, cb, wo, ln_g, ln_b):
    seq = xp.shape[0]
    n_new = xs.shape[0] // n_batch
    hist = CONV_WIDTH - 1
    assert state.shape[0] == hist * n_batch and xs.shape[0] % CONV_ROW_TILE == 0
    vecs = [b_dw[None, :], cg[None, :], cb[None, :], ln_g[None, :], ln_b[None, :]]

    tm = ROW_BLOCK
    yp, cp = pl.pallas_call(
        _conv_prompt_kernel,
        grid=(seq // tm,),
        in_specs=[pl.BlockSpec((tm, D_MODEL), lambda i: (i, 0)),
                  _const_spec(wi.shape), _const_spec(wo.shape), _const_spec(w_dw.shape)]
                 + [_const_spec(t.shape) for t in vecs],
        out_specs=[pl.BlockSpec((tm, D_MODEL), lambda i: (i, 0)),
                   pl.BlockSpec((hist, D_MODEL), lambda i: (0, 0))],
        out_shape=[jax.ShapeDtypeStruct((seq, D_MODEL), F32),
                   jax.ShapeDtypeStruct((hist, D_MODEL), F32)],
        scratch_shapes=[pltpu.VMEM((D_MODEL // LANES, CONV_HIST + tm, LANES), F32),
                        pltpu.VMEM((tm, D_MODEL), F32)],
        compiler_params=pltpu.CompilerParams(dimension_semantics=("arbitrary",),
                                             vmem_limit_bytes=VMEM_LIMIT_BYTES),
        name="conv_prompt",
    )(xp, wi, wo, w_dw, *vecs)

    rows = xs.shape[0]
    ys, cs = pl.pallas_call(
        functools.partial(_conv_sample_kernel, n_batch=n_batch, n_new=n_new),
        in_specs=[_vmem_spec()] * 10,
        out_specs=[_vmem_spec()] * 2,
        out_shape=[jax.ShapeDtypeStruct((rows, D_MODEL), F32),
                   jax.ShapeDtypeStruct(state.shape, F32)],
        scratch_shapes=[pltpu.VMEM((D_MODEL // LANES, n_batch * (CONV_HIST + n_new), LANES), F32),
                        pltpu.VMEM((D_MODEL // LANES, rows, LANES), F32),
                        pltpu.VMEM((rows, D_MODEL), F32)],
        compiler_params=pltpu.CompilerParams(vmem_limit_bytes=VMEM_LIMIT_BYTES),
        name="conv_sample",
    )(xs, state, wi, wo, w_dw, *vecs)
    return yp, ys, cp, cs


def kernel(x_prompt, x_sample, cache_k, cache_v, state_conv, attn_w_in, attn_sink, attn_w_out,
           conv_w_in, conv_w_dw, conv_b_dw, conv_ln_g, conv_ln_b, conv_w_out, post_ln_g, post_ln_b):
    batch, seq, _ = x_prompt.shape
    dec_batch, dec_seq, _ = x_sample.shape
    wc = cache_k.shape[2]
    assert batch == 1 and wc == WIN_ROWS and seq % ROW_BLOCK == 0
    xp = x_prompt.reshape(seq, D_MODEL)
    xs = x_sample.reshape(dec_batch * dec_seq, D_MODEL)

    def feature_major(c):
        return jnp.transpose(c, (0, 2, 3, 1)).reshape(c.shape[0], KV_WIDTH, wc)

    def frame_major(c_t, n):
        return jnp.transpose(c_t.reshape(n, N_KV_HEADS, HEAD_DIM, wc), (0, 3, 1, 2))[None]

    xp, xs, kp, vp, ksn, vsn, conv_wi, conv_wo = _attn_layer(
        xp, xs, feature_major(cache_k[0]), feature_major(cache_v[0]),
        attn_w_in[0], attn_sink[0], attn_w_out[0], post_ln_g[0], post_ln_b[0],
        conv_w_in[0], conv_w_out[0])
    hist = state_conv.shape[2]
    state = jnp.transpose(state_conv[0], (1, 0, 2)).reshape(hist * dec_batch, D_MODEL)
    xp, xs, cp, cs = _conv_layer(
        xp, xs, state, dec_batch, conv_wi, conv_w_dw[0], conv_b_dw[0], conv_ln_g[0], conv_ln_b[0],
        conv_wo, post_ln_g[1], post_ln_b[1])
    cs = jnp.transpose(cs.reshape(hist, dec_batch, D_MODEL), (1, 0, 2))

    return (xp.reshape(batch, seq, D_MODEL), xs.reshape(dec_batch, dec_seq, D_MODEL),
            frame_major(kp, 1), frame_major(vp, 1),
            frame_major(ksn, dec_batch), frame_major(vsn, dec_batch),
            cp[None, None], cs[None])
```

```python
import functools

import jax
import jax.numpy as jnp
from jax import lax
from jax.experimental import pallas as pl
from jax.experimental.pallas import tpu as pltpu

F32 = jnp.float32
BF16 = jnp.bfloat16

D_MODEL = 2048
DEPTH = 2
PAST_LEN = 1024
CHUNK = 64
HEAD_DIM = 64
N_HEADS = D_MODEL // HEAD_DIM
N_KV_HEADS = 8
GROUP = N_HEADS // N_KV_HEADS
ATTN_WIDTH = N_HEADS * HEAD_DIM
KV_WIDTH = N_KV_HEADS * HEAD_DIM
WIN_CHUNKS = 2
WIN_ROWS = WIN_CHUNKS * CHUNK
ROT_DIM = HEAD_DIM // 4
ROPE_THETA = 500000.0
CONV_WIDTH = 31
LN_EPS = 1e-5
DEEPNORM_ALPHA = (2.0 * DEPTH) ** 0.25
ATTN_SCALE = HEAD_DIM ** -0.5
LOG2E = 1.4426950408889634
Q_SCALE = ATTN_SCALE * LOG2E

LANES = 128
SUBLANES = 8
VMEM_LIMIT_BYTES = 60 * 1024 * 1024
ROW_BLOCK = 256
CONV_HIST = 32
CONV_ROW_TILE = 32
PAIR_ROWS = 2 * CHUNK
PAIR_KEYS = WIN_ROWS + PAIR_ROWS


def _layer_norm(r, g, b):
    mu = jnp.mean(r, axis=-1, keepdims=True)
    d = r - mu
    var = jnp.mean(d * d, axis=-1, keepdims=True)
    return d * lax.rsqrt(var + LN_EPS) * g + b


def _silu(t):
    return t * jax.nn.sigmoid(t)


def _rope_tables(pos, inv_ref, sgn_ref):
    ang = pos * inv_ref[...]
    sin = jnp.sin(ang)
    return jnp.cos(ang), sin if sgn_ref is None else sin * sgn_ref[...]


def _rope(t, cosv, sinv):
    lane = lax.broadcasted_iota(jnp.int32, (1, LANES), 1) % HEAD_DIM
    partner = jnp.where(lane < ROT_DIM // 2,
                        pltpu.roll(t, LANES - ROT_DIM // 2, 1),
                        pltpu.roll(t, ROT_DIM // 2, 1))
    return t * cosv + partner * sinv


def _lo_hi(col):
    low = lax.broadcasted_iota(jnp.int32, (1, LANES), 1) < HEAD_DIM
    swapped = pltpu.roll(col, HEAD_DIM, 1)
    zero = jnp.zeros_like(col)
    even_lo = jnp.where(low, col, zero).astype(BF16)
    even_hi = jnp.where(low, zero, swapped).astype(BF16)
    odd_lo = jnp.where(low, swapped, zero).astype(BF16)
    odd_hi = jnp.where(low, zero, col).astype(BF16)
    return even_lo, even_hi, odd_lo, odd_hi


def _scores(q2, k_lo, k_hi):
    qst = jnp.concatenate([q2[:, :LANES], q2[:, LANES:]], axis=0)
    dims = (((1,), (1,)), ((), ()))
    return (lax.dot_general(qst, k_lo, dims, preferred_element_type=F32),
            lax.dot_general(qst, k_hi, dims, preferred_element_type=F32))


def _sink_softmax(s, sk, bias):
    if bias is not None:
        s = s + bias
    m = jnp.maximum(jnp.max(s, axis=-1, keepdims=True), sk)
    p = jnp.exp2(s - m)
    denom = jnp.sum(p, axis=-1, keepdims=True) + jnp.exp2(sk - m)
    return p.astype(BF16), 1.0 / denom


def _weighted_values(p_a, inv_a, p_b, inv_b, v_lo, v_hi):
    o_a = jnp.dot(p_a, v_lo, preferred_element_type=F32)
    o_b = jnp.dot(p_b, v_hi, preferred_element_type=F32)
    return o_a * inv_a + o_b * inv_b


def _attend_heads(q_of, k_of, v_of, sink_ref, q_rows, bias):
    scores = [_scores(q_of(kv), *k_of(kv)) for kv in range(N_KV_HEADS)]
    probs = []
    for kv, (s_a, s_b) in enumerate(scores):
        sink_a, sink_b = _sink_cols(sink_ref, kv, q_rows)
        probs.append(_sink_softmax(s_a, sink_a, bias) + _sink_softmax(s_b, sink_b, bias))
    return [_weighted_values(*probs[kv], *v_of(kv)) for kv in range(N_KV_HEADS)]


def _sink_cols(sink_ref, kv, q_rows):
    row = lax.broadcasted_iota(jnp.int32, (2 * q_rows, 1), 0)
    first = row < q_rows
    h = kv * GROUP
    return (jnp.where(first, sink_ref[h + 0] * LOG2E, sink_ref[h + 2] * LOG2E),
            jnp.where(first, sink_ref[h + 1] * LOG2E, sink_ref[h + 3] * LOG2E))


def _split_attn_w_in(w_ref):
    k0, v0, g0 = ATTN_WIDTH, ATTN_WIDTH + KV_WIDTH, ATTN_WIDTH + 2 * KV_WIDTH
    return w_ref.at[:, 0:k0], w_ref.at[:, k0:v0], w_ref.at[:, v0:g0], w_ref.at[:, g0:g0 + ATTN_WIDTH]


def _attn_prompt_kernel(x_ref, w_ref, wo_ref, inv_ref, sgn_ref,
                        sink_ref, lng_ref, lnb_ref, next_wi_ref, next_wo_ref,
                        y_ref, knew_ref, vnew_ref, next_wi_bf_ref, next_wo_bf_ref,
                        qs, klo, khi, vlo, vhi, os_, rope_tab):
    i = pl.program_id(0)
    last = pl.num_programs(0) - 1
    tm = x_ref.shape[0]
    next_wi_bf_ref[...] = next_wi_ref[...].astype(BF16)
    next_wo_bf_ref[...] = next_wo_ref[...].astype(BF16)
    wq_ref, wk_ref, wv_ref, wg_ref = _split_attn_w_in(w_ref)

    @pl.when(i == 0)
    def _():
        for ref in (klo, khi, vlo, vhi):
            ref[0:WIN_ROWS, :] = jnp.zeros((WIN_ROWS, ref.shape[1]), BF16)
        off = lax.broadcasted_iota(jnp.int32, (tm, 1), 0).astype(F32)
        cos_r, sin_r = _rope_tables(off, inv_ref, None)
        rope_tab[0] = cos_r
        rope_tab[1] = sin_r
        rope_tab[2] = cos_r * sgn_ref[...]
        rope_tab[3] = sin_r * sgn_ref[...]

    x = x_ref[...]
    xb = x.astype(BF16)
    start = jnp.full((1, 1), i * tm, jnp.int32).astype(F32)
    cos_s, sin_s = _rope_tables(start, inv_ref, None)
    cosv = cos_s * rope_tab[0] - sin_s * rope_tab[1]
    sinv = sin_s * rope_tab[2] + cos_s * rope_tab[3]

    q = jnp.dot(xb, wq_ref[...], preferred_element_type=F32)
    for j in range(ATTN_WIDTH // LANES):
        sl = slice(j * LANES, (j + 1) * LANES)
        qs[:, sl] = (_rope(q[:, sl], cosv, sinv) * Q_SCALE).astype(BF16)

    k = jnp.dot(xb, wk_ref[...], preferred_element_type=F32)
    v = jnp.dot(xb, wv_ref[...], preferred_element_type=F32)
    new_rows = slice(WIN_ROWS, WIN_ROWS + tm)
    for c in range(KV_WIDTH // LANES):
        sl = slice(c * LANES, (c + 1) * LANES)
        kc = _rope(k[:, sl], cosv, sinv)
        vc = v[:, sl]

        @pl.when(i == last)
        def _():
            knew_ref[sl, :] = kc[tm - WIN_ROWS:, :].T
            vnew_ref[sl, :] = vc[tm - WIN_ROWS:, :].T

        ev = slice(2 * c * LANES, (2 * c + 1) * LANES)
        od = slice((2 * c + 1) * LANES, (2 * c + 2) * LANES)
        klo[new_rows, ev], khi[new_rows, ev], klo[new_rows, od], khi[new_rows, od] = _lo_hi(kc)
        vlo[new_rows, ev], vhi[new_rows, ev], vlo[new_rows, od], vhi[new_rows, od] = _lo_hi(vc)

    row = lax.broadcasted_iota(jnp.int32, (2 * PAIR_ROWS, 1), 0)
    col = lax.broadcasted_iota(jnp.int32, (1, PAIR_KEYS), 1)
    first_col = jnp.where(row % PAIR_ROWS >= CHUNK, CHUNK, 0)
    in_window = (col >= first_col) & (col < first_col + (PAIR_KEYS - CHUNK))
    window_bias = jnp.where(in_window, 0.0, -jnp.inf)

    def pair_body(p, carry):
        r0 = pl.multiple_of(p * PAIR_ROWS, PAIR_ROWS)
        first_key_chunk = i * (tm // CHUNK) + 2 * p - WIN_CHUNKS
        started = col >= jnp.maximum(0, -first_key_chunk) * CHUNK
        bias = window_bias + jnp.where(started, 0.0, -jnp.inf)
        keys = pl.ds(r0, PAIR_KEYS)

        def head_lanes(kv):
            return slice(kv * LANES, (kv + 1) * LANES)

        outs = _attend_heads(
            lambda kv: qs[pl.ds(r0, PAIR_ROWS), kv * 2 * LANES:(kv + 1) * 2 * LANES],
            lambda kv: (klo[keys, head_lanes(kv)], khi[keys, head_lanes(kv)]),
            lambda kv: (vlo[keys, head_lanes(kv)], vhi[keys, head_lanes(kv)]),
            sink_ref, PAIR_ROWS, bias)
        for kv, o in enumerate(outs):
            os_[pl.ds(r0, PAIR_ROWS), kv * 2 * LANES:kv * 2 * LANES + LANES] = o[:PAIR_ROWS]
            os_[pl.ds(r0, PAIR_ROWS), kv * 2 * LANES + LANES:(kv + 1) * 2 * LANES] = o[PAIR_ROWS:]
        return carry

    lax.fori_loop(0, tm // PAIR_ROWS, pair_body, 0)

    for ref in (klo, khi, vlo, vhi):
        ref[0:WIN_ROWS, :] = ref[tm:tm + WIN_ROWS, :]

    g = jnp.dot(xb, wg_ref[...], preferred_element_type=F32)
    z = (os_[...] * _silu(g)).astype(BF16)
    y = jnp.dot(z, wo_ref[...], preferred_element_type=F32)
    y_ref[...] = _layer_norm(DEEPNORM_ALPHA * x + y, lng_ref[...], lnb_ref[...])


def _attn_sample_proj_kernel(xk_ref, w_ref, wo_ref, h_ref, w_bf_ref, wo_bf_ref):
    kstep = pl.program_id(0)
    w_bf = w_ref[...].astype(BF16)
    w_bf_ref[...] = w_bf
    wo_bf_ref[...] = wo_ref[...].astype(BF16)
    part = jnp.dot(xk_ref[...].astype(BF16), w_bf, preferred_element_type=F32)

    @pl.when(kstep == 0)
    def _():
        h_ref[...] = part

    @pl.when(kstep > 0)
    def _():
        h_ref[...] += part


def _attn_sample_kernel(x_ref, ck_ref, cv_ref, h_ref, wo_ref, inv_ref, sgn_ref, sink_ref, lng_ref, lnb_ref,
                        y_ref, knew_ref, vnew_ref, qs, ks, vs, os_, *, n_batch, n_new, n_cache):
    rows = n_batch * n_new
    hq_ref, hk_ref, hv_ref, hg_ref = _split_attn_w_in(h_ref)
    x = x_ref[...]
    pos = (PAST_LEN + lax.broadcasted_iota(jnp.int32, (rows, 1), 0) % n_new).astype(F32)
    cosv, sinv = _rope_tables(pos, inv_ref, sgn_ref)

    for j in range(ATTN_WIDTH // LANES):
        sl = slice(j * LANES, (j + 1) * LANES)
        qs[:, sl] = (_rope(hq_ref[:, sl], cosv, sinv) * Q_SCALE).astype(BF16)
    k = hk_ref[...]
    v = hv_ref[...]
    k_t = jnp.concatenate([_rope(k[:, c * LANES:(c + 1) * LANES], cosv, sinv)
                           for c in range(KV_WIDTH // LANES)], axis=1).T
    v_t = v.T
    for j in range(rows // LANES):
        ks[j] = k_t[:, j * LANES:(j + 1) * LANES]
        vs[j] = v_t[:, j * LANES:(j + 1) * LANES]

    per_tile = LANES // n_new
    lane = lax.broadcasted_iota(jnp.int32, (1, LANES), 1)
    zeros = jnp.zeros((HEAD_DIM, 2 * LANES), BF16)
    dims_nt = (((1,), (1,)), ((), ()))

    def batch_body(b, carry):
        r0 = pl.multiple_of(b * n_new, n_new)
        off = (b % per_tile) * n_new
        k_old, v_old = ck_ref[b], cv_ref[b]
        k_new, v_new = ks[b // per_tile], vs[b // per_tile]
        mine = (lane >= off) & (lane < off + n_new)

        to_end = n_cache - n_new - off
        to_end = jnp.where(to_end < 0, to_end + LANES, to_end)
        for old, new, out_ref in ((k_old, k_new, knew_ref), (v_old, v_new, vnew_ref)):
            out_ref[b] = jnp.where(lane < n_cache - n_new,
                                   pltpu.roll(old, n_cache - n_new, 1), pltpu.roll(new, to_end, 1))

        bias = jnp.concatenate([jnp.zeros((1, n_cache), F32), jnp.where(mine, 0.0, -jnp.inf)], axis=1)
        k_all = jnp.concatenate([k_old, k_new], axis=1).astype(BF16)
        v_all = jnp.concatenate([v_old, jnp.where(mine, v_new, 0.0)], axis=1).astype(BF16)

        scores = []
        for kv in range(N_KV_HEADS):
            q2 = qs[pl.ds(r0, n_new), kv * 2 * LANES:(kv + 1) * 2 * LANES]
            qst = jnp.concatenate([q2[:, :LANES], q2[:, LANES:]], axis=0)
            k_kv = k_all[kv * HEAD_DIM:(kv + 1) * HEAD_DIM, :]
            scores.append((jnp.dot(qst, jnp.concatenate([k_kv, zeros], axis=0), preferred_element_type=F32),
                           jnp.dot(qst, jnp.concatenate([zeros, k_kv], axis=0), preferred_element_type=F32)))
        probs = []
        for kv, (s_a, s_b) in enumerate(scores):
            sink_a, sink_b = _sink_cols(sink_ref, kv, n_new)
            probs.append(_sink_softmax(s_a, sink_a, bias) + _sink_softmax(s_b, sink_b, bias))
        for kv, (p_a, inv_a, p_b, inv_b) in enumerate(probs):
            v_kv = v_all[kv * HEAD_DIM:(kv + 1) * HEAD_DIM, :]
            o_a = lax.dot_general(p_a, jnp.concatenate([v_kv, zeros], axis=0), dims_nt,
                                  preferred_element_type=F32)
            o_b = lax.dot_general(p_b, jnp.concatenate([zeros, v_kv], axis=0), dims_nt,
                                  preferred_element_type=F32)
            o = o_a * inv_a + o_b * inv_b
            os_[pl.ds(r0, n_new), kv * 2 * LANES:kv * 2 * LANES + LANES] = o[:n_new]
            os_[pl.ds(r0, n_new), kv * 2 * LANES + LANES:(kv + 1) * 2 * LANES] = o[n_new:]
        return carry

    lax.fori_loop(0, n_batch, batch_body, 0)

    z = (os_[...] * _silu(hg_ref[...])).astype(BF16)
    y = jnp.dot(z, wo_ref[...], preferred_element_type=F32)
    y_ref[...] = _layer_norm(DEEPNORM_ALPHA * x + y, lng_ref[...], lnb_ref[...])


def _depthwise(u_scr, c_scr, wdw_ref, bdw_ref, base, out_base, n_rows):
    off = CONV_HIST - (CONV_WIDTH - 1)
    for l in range(D_MODEL // LANES):
        sl = slice(l * LANES, (l + 1) * LANES)
        acc = jnp.zeros((n_rows, LANES), F32)
        for j in range(CONV_WIDTH):
            acc = acc + wdw_ref[j:j + 1, sl] * u_scr[l, pl.ds(base + j + off, n_rows, stride=1), :]
        c_scr[pl.ds(out_base, n_rows), sl] = acc + bdw_ref[:, sl]


def _conv_gate(c, g, cg_ref, cb_ref):
    cn = _layer_norm(c, cg_ref[...], cb_ref[...])
    return (_silu(cn) * _silu(g)).astype(BF16)


def _split_conv_w_in(w_ref):
    return tuple(w_ref.at[:, n * D_MODEL:(n + 1) * D_MODEL] for n in range(3))


def _conv_prompt_kernel(x_ref, w_ref, wo_ref, wdw_ref, bdw_ref, cg_ref, cb_ref,
                        lng_ref, lnb_ref,
                        y_ref, state_ref,
                        u_scr, c_scr):
    i = pl.program_id(0)
    last = pl.num_programs(0) - 1
    tm = x_ref.shape[0]
    wa_ref, wb_ref, wg_ref = _split_conv_w_in(w_ref)
    n_tiles = D_MODEL // LANES

    @pl.when(i == 0)
    def _():
        u_scr[:, 0:CONV_HIST, :] = jnp.zeros((n_tiles, CONV_HIST, LANES), F32)

    x = x_ref[...]
    xb = x.astype(BF16)
    a = jnp.dot(xb, wa_ref[...], preferred_element_type=F32)
    b = jnp.dot(xb, wb_ref[...], preferred_element_type=F32)
    u = a * jax.nn.sigmoid(b)
    for l in range(n_tiles):
        u_scr[l, CONV_HIST:CONV_HIST + tm, :] = u[:, l * LANES:(l + 1) * LANES]

    def row_body(rt, carry):
        base = pl.multiple_of(rt * CONV_ROW_TILE, CONV_ROW_TILE)
        _depthwise(u_scr, c_scr, wdw_ref, bdw_ref, base, base, CONV_ROW_TILE)
        return carry

    lax.fori_loop(0, tm // CONV_ROW_TILE, row_body, 0)

    @pl.when(i == last)
    def _():
        for l in range(n_tiles):
            state_ref[:, l * LANES:(l + 1) * LANES] = (
                u_scr[l, CONV_HIST + tm - (CONV_WIDTH - 1):CONV_HIST + tm, :])

    u_scr[:, 0:CONV_HIST, :] = u_scr[:, tm:tm + CONV_HIST, :]

    g = jnp.dot(xb, wg_ref[...], preferred_element_type=F32)
    z = _conv_gate(c_scr[...], g, cg_ref, cb_ref)
    y = jnp.dot(z, wo_ref[...], preferred_element_type=F32)
    y_ref[...] = _layer_norm(DEEPNORM_ALPHA * x + y, lng_ref[...], lnb_ref[...])


def _conv_sample_kernel(x_ref, st_ref, w_ref, wo_ref, wdw_ref, bdw_ref, cg_ref, cb_ref,
                        lng_ref, lnb_ref,
                        y_ref, state_ref,
                        u_scr, c_slab, c_scr, *, n_batch, n_new):
    hist = CONV_WIDTH - 1
    pad = CONV_HIST - hist
    rows = n_batch * n_new
    wa_ref, wb_ref, wg_ref = _split_conv_w_in(w_ref)
    x = x_ref[...]
    xb = x.astype(BF16)
    a = jnp.dot(xb, wa_ref[...], preferred_element_type=F32)
    b = jnp.dot(xb, wb_ref[...], preferred_element_type=F32)
    u = a * jax.nn.sigmoid(b)
    first_new = (pad + hist) * n_batch
    for l in range(D_MODEL // LANES):
        sl = slice(l * LANES, (l + 1) * LANES)
        u_scr[l, 0:pad * n_batch, :] = jnp.zeros((pad * n_batch, LANES), F32)
        u_scr[l, pad * n_batch:first_new, :] = st_ref[:, sl]
        for bi in range(n_batch):
            u_scr[l, pl.ds(first_new + bi, n_new, stride=n_batch), :] = u[bi * n_new:(bi + 1) * n_new, sl]
        state_ref[:, sl] = u_scr[l, (pad + n_new) * n_batch:(pad + n_new + hist) * n_batch, :]

    def row_body(rt, carry):
        base = pl.multiple_of(rt * CONV_ROW_TILE, CONV_ROW_TILE)
        for l in range(D_MODEL // LANES):
            sl = slice(l * LANES, (l + 1) * LANES)
            acc = jnp.zeros((CONV_ROW_TILE, LANES), F32)
            for j in range(CONV_WIDTH):
                acc = acc + wdw_ref[j:j + 1, sl] * u_scr[l, pl.ds(base + (j + pad) * n_batch, CONV_ROW_TILE), :]
            c_slab[l, pl.ds(base, CONV_ROW_TILE), :] = acc + bdw_ref[:, sl]
        return carry

    lax.fori_loop(0, rows // CONV_ROW_TILE, row_body, 0)

    for l in range(D_MODEL // LANES):
        for bi in range(n_batch):
            c_scr[bi * n_new:(bi + 1) * n_new, l * LANES:(l + 1) * LANES] = (
                c_slab[l, pl.ds(bi, n_new, stride=n_batch), :])

    g = jnp.dot(xb, wg_ref[...], preferred_element_type=F32)
    z = _conv_gate(c_scr[...], g, cg_ref, cb_ref)
    y = jnp.dot(z, wo_ref[...], preferred_element_type=F32)
    y_ref[...] = _layer_norm(DEEPNORM_ALPHA * x + y, lng_ref[...], lnb_ref[...])


def _const_spec(shape):
    return pl.BlockSpec(shape, lambda i: (0,) * len(shape), pipeline_mode=pl.Buffered(1))


def _const_out_spec(shape):
    return pl.BlockSpec(shape, lambda i: (0,) * len(shape), pipeline_mode=pl.Buffered(1))


def _vmem_spec():
    return pl.BlockSpec(memory_space=pltpu.VMEM)


def _smem_spec():
    return pl.BlockSpec(memory_space=pltpu.SMEM)


def _rope_lane_consts():
    half = ROT_DIM // 2
    inv = ROPE_THETA ** (-jnp.arange(half, dtype=F32) * (2.0 / ROT_DIM))
    zeros = jnp.zeros((HEAD_DIM - ROT_DIM,), F32)
    inv_head = jnp.concatenate([inv, inv, zeros])
    sgn_head = jnp.concatenate([-jnp.ones((half,), F32), jnp.ones((half,), F32), zeros])
    reps = LANES // HEAD_DIM
    return jnp.tile(inv_head, reps)[None, :], jnp.tile(sgn_head, reps)[None, :]


def _attn_layer(xp, xs, ck, cv, w_in, sink, w_out, ln_g, ln_b, next_w_in, next_w_out):
    seq = xp.shape[0]
    n_batch, n_cache = ck.shape[0], ck.shape[2]
    n_new = xs.shape[0] // n_batch
    assert n_cache == LANES and LANES % n_new == 0 and xs.shape[0] % LANES == 0
    inv, sgn = _rope_lane_consts()
    lng, lnb = ln_g[None, :], ln_b[None, :]
    params = pltpu.CompilerParams(dimension_semantics=("arbitrary",), vmem_limit_bytes=VMEM_LIMIT_BYTES)

    rows = xs.shape[0]
    k_rows = 2 * LANES
    h, wi, wo = pl.pallas_call(
        _attn_sample_proj_kernel,
        grid=(D_MODEL // k_rows,),
        in_specs=[pl.BlockSpec((rows, k_rows), lambda k: (0, k)),
                  pl.BlockSpec((k_rows, w_in.shape[1]), lambda k: (k, 0)),
                  pl.BlockSpec((k_rows, w_out.shape[1]), lambda k: (k, 0))],
        out_specs=[_const_out_spec((rows, w_in.shape[1])),
                   pl.BlockSpec((k_rows, w_in.shape[1]), lambda k: (k, 0)),
                   pl.BlockSpec((k_rows, w_out.shape[1]), lambda k: (k, 0))],
        out_shape=[jax.ShapeDtypeStruct((rows, w_in.shape[1]), F32),
                   jax.ShapeDtypeStruct(w_in.shape, BF16),
                   jax.ShapeDtypeStruct(w_out.shape, BF16)],
        compiler_params=params,
        name="attn_sample_proj",
    )(xs, w_in, w_out)

    new_t = pltpu.VMEM((rows // LANES, KV_WIDTH, LANES), F32)
    ys, ksn, vsn = pl.pallas_call(
        functools.partial(_attn_sample_kernel, n_batch=n_batch, n_new=n_new, n_cache=n_cache),
        in_specs=[_vmem_spec()] * 7 + [_smem_spec(), _vmem_spec(), _vmem_spec()],
        out_specs=[_vmem_spec()] * 3,
        out_shape=[jax.ShapeDtypeStruct((rows, D_MODEL), F32),
                   jax.ShapeDtypeStruct(ck.shape, F32),
                   jax.ShapeDtypeStruct(cv.shape, F32)],
        scratch_shapes=[pltpu.VMEM((rows, ATTN_WIDTH), BF16), new_t, new_t,
                        pltpu.VMEM((rows, ATTN_WIDTH), F32)],
        compiler_params=pltpu.CompilerParams(vmem_limit_bytes=VMEM_LIMIT_BYTES),
        name="attn_sample",
    )(xs, ck, cv, h, wo, inv, sgn, sink, lng, lnb)

    tm = ROW_BLOCK
    n_steps = seq // tm
    w_rows = next_w_in.shape[0] // n_steps
    assert next_w_in.shape[0] == next_w_out.shape[0] == w_rows * n_steps and w_rows % (2 * SUBLANES) == 0
    kv_scr = pltpu.VMEM((WIN_ROWS + tm, N_KV_HEADS * LANES), BF16)
    yp, kp, vp, next_wi, next_wo = pl.pallas_call(
        _attn_prompt_kernel,
        grid=(n_steps,),
        in_specs=[pl.BlockSpec((tm, D_MODEL), lambda i: (i, 0)),
                  _const_spec(wi.shape), _const_spec(wo.shape),
                  _const_spec(inv.shape), _const_spec(sgn.shape), _smem_spec(),
                  _const_spec(lng.shape), _const_spec(lnb.shape),
                  pl.BlockSpec((w_rows, next_w_in.shape[1]), lambda i: (i, 0)),
                  pl.BlockSpec((w_rows, next_w_out.shape[1]), lambda i: (i, 0))],
        out_specs=[pl.BlockSpec((tm, D_MODEL), lambda i: (i, 0)),
                   pl.BlockSpec((KV_WIDTH, WIN_ROWS), lambda i: (0, 0)),
                   pl.BlockSpec((KV_WIDTH, WIN_ROWS), lambda i: (0, 0)),
                   pl.BlockSpec((w_rows, next_w_in.shape[1]), lambda i: (i, 0)),
                   pl.BlockSpec((w_rows, next_w_out.shape[1]), lambda i: (i, 0))],
        out_shape=[jax.ShapeDtypeStruct((seq, D_MODEL), F32),
                   jax.ShapeDtypeStruct((KV_WIDTH, WIN_ROWS), F32),
                   jax.ShapeDtypeStruct((KV_WIDTH, WIN_ROWS), F32),
                   jax.ShapeDtypeStruct(next_w_in.shape, BF16),
                   jax.ShapeDtypeStruct(next_w_out.shape, BF16)],
        scratch_shapes=[pltpu.VMEM((tm, ATTN_WIDTH), BF16), kv_scr, kv_scr, kv_scr, kv_scr,
                        pltpu.VMEM((tm, ATTN_WIDTH), F32), pltpu.VMEM((4, tm, LANES), F32)],
        compiler_params=params,
        name="attn_prompt",
    )(xp, wi, wo, inv, sgn, sink, lng, lnb, next_w_in, next_w_out)

    return yp, ys, kp, vp, ksn, vsn, next_wi, next_wo


def _conv_layer(xp, xs, state, n_batch, wi, w_dw, b_dw, cg, cb, wo, ln_g, ln_b):
    seq = xp.shape[0]
    n_new = xs.shape[0] // n_batch
    hist = CONV_WIDTH - 1
    assert state.shape[0] == hist * n_batch and xs.shape[0] % CONV_ROW_TILE == 0
    vecs = [b_dw[None, :], cg[None, :], cb[None, :], ln_g[None, :], ln_b[None, :]]

    tm = ROW_BLOCK
    yp, cp = pl.pallas_call(
        _conv_prompt_kernel,
        grid=(seq // tm,),
        in_specs=[pl.BlockSpec((tm, D_MODEL), lambda i: (i, 0)),
                  _const_spec(wi.shape), _const_spec(wo.shape), _const_spec(w_dw.shape)]
                 + [_const_spec(t.shape) for t in vecs],
        out_specs=[pl.BlockSpec((tm, D_MODEL), lambda i: (i, 0)),
                   pl.BlockSpec((hist, D_MODEL), lambda i: (0, 0))],
        out_shape=[jax.ShapeDtypeStruct((seq, D_MODEL), F32),
                   jax.ShapeDtypeStruct((hist, D_MODEL), F32)],
        scratch_shapes=[pltpu.VMEM((D_MODEL // LANES, CONV_HIST + tm, LANES), F32),
                        pltpu.VMEM((tm, D_MODEL), F32)],
        compiler_params=pltpu.CompilerParams(dimension_semantics=("arbitrary",),
                                             vmem_limit_bytes=VMEM_LIMIT_BYTES),
        name="conv_prompt",
    )(xp, wi, wo, w_dw, *vecs)

    rows = xs.shape[0]
    ys, cs = pl.pallas_call(
        functools.partial(_conv_sample_kernel, n_batch=n_batch, n_new=n_new),
        in_specs=[_vmem_spec()] * 10,
        out_specs=[_vmem_spec()] * 2,
        out_shape=[jax.ShapeDtypeStruct((rows, D_MODEL), F32),
                   jax.ShapeDtypeStruct(state.shape, F32)],
        scratch_shapes=[pltpu.VMEM((D_MODEL // LANES, n_batch * (CONV_HIST + n_new), LANES), F32),
                        pltpu.VMEM((D_MODEL // LANES, rows, LANES), F32),
                        pltpu.VMEM((rows, D_MODEL), F32)],
        compiler_params=pltpu.CompilerParams(vmem_limit_bytes=VMEM_LIMIT_BYTES),
        name="conv_sample",
    )(xs, state, wi, wo, w_dw, *vecs)
    return yp, ys, cp, cs


def kernel(x_prompt, x_sample, cache_k, cache_v, state_conv, attn_w_in, attn_sink, attn_w_out,
           conv_w_in, conv_w_dw, conv_b_dw, conv_ln_g, conv_ln_b, conv_w_out, post_ln_g, post_ln_b):
    batch, seq, _ = x_prompt.shape
    dec_batch, dec_seq, _ = x_sample.shape
    wc = cache_k.shape[2]
    assert batch == 1 and wc == WIN_ROWS and seq % ROW_BLOCK == 0
    xp = x_prompt.reshape(seq, D_MODEL)
    xs = x_sample.reshape(dec_batch * dec_seq, D_MODEL)

    def feature_major(c):
        return jnp.transpose(c, (0, 2, 3, 1)).reshape(c.shape[0], KV_WIDTH, wc)

    def frame_major(c_t, n):
        return jnp.transpose(c_t.reshape(n, N_KV_HEADS, HEAD_DIM, wc), (0, 3, 1, 2))[None]

    xp, xs, kp, vp, ksn, vsn, conv_wi, conv_wo = _attn_layer(
        xp, xs, feature_major(cache_k[0]), feature_major(cache_v[0]),
        attn_w_in[0], attn_sink[0], attn_w_out[0], post_ln_g[0], post_ln_b[0],
        conv_w_in[0], conv_w_out[0])
    hist = state_conv.shape[2]
    state = jnp.transpose(state_conv[0], (1, 0, 2)).reshape(hist * dec_batch, D_MODEL)
    xp, xs, cp, cs = _conv_layer(
        xp, xs, state, dec_batch, conv_wi, conv_w_dw[0], conv_b_dw[0], conv_ln_g[0], conv_ln_b[0],
        conv_wo, post_ln_g[1], post_ln_b[1])
    cs = jnp.transpose(cs.reshape(hist, dec_batch, D_MODEL), (1, 0, 2))

    return (xp.reshape(batch, seq, D_MODEL), xs.reshape(dec_batch, dec_seq, D_MODEL),
            frame_major(kp, 1), frame_major(vp, 1),
            frame_major(ksn, dec_batch), frame_major(vsn, dec_batch),
            cp[None, None], cs[None])
```

```python
import functools

import jax
import jax.numpy as jnp
from jax import lax
from jax.experimental import pallas as pl
from jax.experimental.pallas import tpu as pltpu

F32 = jnp.float32
BF16 = jnp.bfloat16

D_MODEL = 2048
DEPTH = 2
PAST_LEN = 1024
CHUNK = 64
HEAD_DIM = 64
N_HEADS = D_MODEL // HEAD_DIM
N_KV_HEADS = 8
GROUP = N_HEADS // N_KV_HEADS
ATTN_WIDTH = N_HEADS * HEAD_DIM
KV_WIDTH = N_KV_HEADS * HEAD_DIM
WIN_CHUNKS = 2
WIN_ROWS = WIN_CHUNKS * CHUNK
ROT_DIM = HEAD_DIM // 4
ROPE_THETA = 500000.0
CONV_WIDTH = 31
LN_EPS = 1e-5
DEEPNORM_ALPHA = (2.0 * DEPTH) ** 0.25
ATTN_SCALE = HEAD_DIM ** -0.5
LOG2E = 1.4426950408889634
Q_SCALE = ATTN_SCALE * LOG2E

LANES = 128
SUBLANES = 8
VMEM_CAPACITY_BYTES = 64 * 1024 * 1024
VMEM_LIMIT_BYTES = 60 * 1024 * 1024
ROW_BLOCK = 256
CONV_HIST = 32
CONV_ROW_TILE = 32
PAIR_ROWS = 2 * CHUNK
PAIR_KEYS = WIN_ROWS + PAIR_ROWS


def _layer_norm(r, g, b):
    mu = jnp.mean(r, axis=-1, keepdims=True)
    d = r - mu
    var = jnp.mean(d * d, axis=-1, keepdims=True)
    return d * lax.rsqrt(var + LN_EPS) * g + b


def _silu(t):
    return t * jax.nn.sigmoid(t)


def _rope_tables(pos, inv_ref, sgn_ref):
    ang = pos * inv_ref[...]
    sin = jnp.sin(ang)
    return jnp.cos(ang), sin if sgn_ref is None else sin * sgn_ref[...]


def _rope(t, cosv, sinv):
    lane = lax.broadcasted_iota(jnp.int32, (1, LANES), 1) % HEAD_DIM
    partner = jnp.where(lane < ROT_DIM // 2,
                        pltpu.roll(t, LANES - ROT_DIM // 2, 1),
                        pltpu.roll(t, ROT_DIM // 2, 1))
    return t * cosv + partner * sinv


def _lo_hi(col):
    low = lax.broadcasted_iota(jnp.int32, (1, LANES), 1) < HEAD_DIM
    swapped = pltpu.roll(col, HEAD_DIM, 1)
    zero = jnp.zeros_like(col)
    even_lo = jnp.where(low, col, zero).astype(BF16)
    even_hi = jnp.where(low, zero, swapped).astype(BF16)
    odd_lo = jnp.where(low, swapped, zero).astype(BF16)
    odd_hi = jnp.where(low, zero, col).astype(BF16)
    return even_lo, even_hi, odd_lo, odd_hi


def _scores(q2, k_lo, k_hi):
    qst = jnp.concatenate([q2[:, :LANES], q2[:, LANES:]], axis=0)
    dims = (((1,), (1,)), ((), ()))
    return (lax.dot_general(qst, k_lo, dims, preferred_element_type=F32),
            lax.dot_general(qst, k_hi, dims, preferred_element_type=F32))


def _sink_softmax(s, sk, bias):
    if bias is not None:
        s = s + bias
    m = jnp.maximum(jnp.max(s, axis=-1, keepdims=True), sk)
    p = jnp.exp2(s - m)
    denom = jnp.sum(p, axis=-1, keepdims=True) + jnp.exp2(sk - m)
    return p.astype(BF16), 1.0 / denom


def _weighted_values(p_a, inv_a, p_b, inv_b, v_lo, v_hi):
    o_a = jnp.dot(p_a, v_lo, preferred_element_type=F32)
    o_b = jnp.dot(p_b, v_hi, preferred_element_type=F32)
    return o_a * inv_a + o_b * inv_b


def _attend_heads(q_of, k_of, v_of, sink_ref, q_rows, bias):
    scores = [_scores(q_of(kv), *k_of(kv)) for kv in range(N_KV_HEADS)]
    probs = []
    for kv, (s_a, s_b) in enumerate(scores):
        sink_a, sink_b = _sink_cols(sink_ref, kv, q_rows)
        probs.append(_sink_softmax(s_a, sink_a, bias) + _sink_softmax(s_b, sink_b, bias))
    return [_weighted_values(*probs[kv], *v_of(kv)) for kv in range(N_KV_HEADS)]


def _sink_cols(sink_ref, kv, q_rows):
    row = lax.broadcasted_iota(jnp.int32, (2 * q_rows, 1), 0)
    first = row < q_rows
    h = kv * GROUP
    return (jnp.where(first, sink_ref[h + 0] * LOG2E, sink_ref[h + 2] * LOG2E),
            jnp.where(first, sink_ref[h + 1] * LOG2E, sink_ref[h + 3] * LOG2E))


def _split_attn_w_in(w_ref):
    k0, v0, g0 = ATTN_WIDTH, ATTN_WIDTH + KV_WIDTH, ATTN_WIDTH + 2 * KV_WIDTH
    return w_ref.at[:, 0:k0], w_ref.at[:, k0:v0], w_ref.at[:, v0:g0], w_ref.at[:, g0:g0 + ATTN_WIDTH]


def _attn_prompt_kernel(x_ref, w_ref, wo_ref, inv_ref, sgn_ref,
                        sink_ref, lng_ref, lnb_ref, next_wi_ref, next_wo_ref,
                        y_ref, knew_ref, vnew_ref, next_wi_bf_ref, next_wo_bf_ref,
                        qs, klo, khi, vlo, vhi, os_, rope_tab):
    i = pl.program_id(0)
    last = pl.num_programs(0) - 1
    tm = x_ref.shape[0]
    next_wi_bf_ref[...] = next_wi_ref[...].astype(BF16)
    next_wo_bf_ref[...] = next_wo_ref[...].astype(BF16)
    wq_ref, wk_ref, wv_ref, wg_ref = _split_attn_w_in(w_ref)

    @pl.when(i == 0)
    def _():
        for ref in (klo, khi, vlo, vhi):
            ref[0:WIN_ROWS, :] = jnp.zeros((WIN_ROWS, ref.shape[1]), BF16)
        off = lax.broadcasted_iota(jnp.int32, (tm, 1), 0).astype(F32)
        cos_r, sin_r = _rope_tables(off, inv_ref, None)
        rope_tab[0] = cos_r
        rope_tab[1] = sin_r
        rope_tab[2] = cos_r * sgn_ref[...]
        rope_tab[3] = sin_r * sgn_ref[...]

    x = x_ref[...]
    xb = x.astype(BF16)
    start = jnp.full((1, 1), i * tm, jnp.int32).astype(F32)
    cos_s, sin_s = _rope_tables(start, inv_ref, None)
    cosv = cos_s * rope_tab[0] - sin_s * rope_tab[1]
    sinv = sin_s * rope_tab[2] + cos_s * rope_tab[3]

    q = jnp.dot(xb, wq_ref[...], preferred_element_type=F32)
    for j in range(ATTN_WIDTH // LANES):
        sl = slice(j * LANES, (j + 1) * LANES)
        qs[:, sl] = (_rope(q[:, sl], cosv, sinv) * Q_SCALE).astype(BF16)

    k = jnp.dot(xb, wk_ref[...], preferred_element_type=F32)
    v = jnp.dot(xb, wv_ref[...], preferred_element_type=F32)
    new_rows = slice(WIN_ROWS, WIN_ROWS + tm)
    for c in range(KV_WIDTH // LANES):
        sl = slice(c * LANES, (c + 1) * LANES)
        kc = _rope(k[:, sl], cosv, sinv)
        vc = v[:, sl]

        @pl.when(i == last)
        def _():
            knew_ref[sl, :] = kc[tm - WIN_ROWS:, :].T
            vnew_ref[sl, :] = vc[tm - WIN_ROWS:, :].T

        ev = slice(2 * c * LANES, (2 * c + 1) * LANES)
        od = slice((2 * c + 1) * LANES, (2 * c + 2) * LANES)
        klo[new_rows, ev], khi[new_rows, ev], klo[new_rows, od], khi[new_rows, od] = _lo_hi(kc)
        vlo[new_rows, ev], vhi[new_rows, ev], vlo[new_rows, od], vhi[new_rows, od] = _lo_hi(vc)

    row = lax.broadcasted_iota(jnp.int32, (2 * PAIR_ROWS, 1), 0)
    col = lax.broadcasted_iota(jnp.int32, (1, PAIR_KEYS), 1)
    first_col = jnp.where(row % PAIR_ROWS >= CHUNK, CHUNK, 0)
    in_window = (col >= first_col) & (col < first_col + (PAIR_KEYS - CHUNK))
    window_bias = jnp.where(in_window, 0.0, -jnp.inf)

    def pair_body(p, carry):
        r0 = p * PAIR_ROWS
        first_key_chunk = i * (tm // CHUNK) + 2 * p - WIN_CHUNKS
        started = col >= jnp.maximum(0, -first_key_chunk) * CHUNK
        bias = window_bias + jnp.where(started, 0.0, -jnp.inf)
        keys = pl.ds(r0, PAIR_KEYS)

        def head_lanes(kv):
            return slice(kv * LANES, (kv + 1) * LANES)

        outs = _attend_heads(
            lambda kv: qs[pl.ds(r0, PAIR_ROWS), kv * 2 * LANES:(kv + 1) * 2 * LANES],
            lambda kv: (klo[keys, head_lanes(kv)], khi[keys, head_lanes(kv)]),
            lambda kv: (vlo[keys, head_lanes(kv)], vhi[keys, head_lanes(kv)]),
            sink_ref, PAIR_ROWS, bias)
        for kv, o in enumerate(outs):
            os_[pl.ds(r0, PAIR_ROWS), kv * 2 * LANES:kv * 2 * LANES + LANES] = o[:PAIR_ROWS]
            os_[pl.ds(r0, PAIR_ROWS), kv * 2 * LANES + LANES:(kv + 1) * 2 * LANES] = o[PAIR_ROWS:]
        return carry

    for p in range(tm // PAIR_ROWS):
        pair_body(p, 0)

    for ref in (klo, khi, vlo, vhi):
        ref[0:WIN_ROWS, :] = ref[tm:tm + WIN_ROWS, :]

    g = jnp.dot(xb, wg_ref[...], preferred_element_type=F32)
    z = (os_[...] * _silu(g)).astype(BF16)
    y = jnp.dot(z, wo_ref[...], preferred_element_type=F32)
    y_ref[...] = _layer_norm(DEEPNORM_ALPHA * x + y, lng_ref[...], lnb_ref[...])


def _attn_sample_proj_kernel(xk_ref, w_ref, wo_ref, h_ref, w_bf_ref, wo_bf_ref):
    kstep = pl.program_id(0)
    w_bf = w_ref[...].astype(BF16)
    w_bf_ref[...] = w_bf
    wo_bf_ref[...] = wo_ref[...].astype(BF16)
    part = jnp.dot(xk_ref[...].astype(BF16), w_bf, preferred_element_type=F32)

    @pl.when(kstep == 0)
    def _():
        h_ref[...] = part

    @pl.when(kstep > 0)
    def _():
        h_ref[...] += part


def _attn_sample_kernel(x_ref, ck_ref, cv_ref, h_ref, wo_ref, inv_ref, sgn_ref, sink_ref, lng_ref, lnb_ref,
                        y_ref, knew_ref, vnew_ref, qs, ks, vs, os_, *, n_batch, n_new, n_cache):
    rows = n_batch * n_new
    hq_ref, hk_ref, hv_ref, hg_ref = _split_attn_w_in(h_ref)
    x = x_ref[...]
    pos = (PAST_LEN + lax.broadcasted_iota(jnp.int32, (rows, 1), 0) % n_new).astype(F32)
    cosv, sinv = _rope_tables(pos, inv_ref, sgn_ref)

    for j in range(ATTN_WIDTH // LANES):
        sl = slice(j * LANES, (j + 1) * LANES)
        qs[:, sl] = (_rope(hq_ref[:, sl], cosv, sinv) * Q_SCALE).astype(BF16)
    k = hk_ref[...]
    v = hv_ref[...]
    k_t = jnp.concatenate([_rope(k[:, c * LANES:(c + 1) * LANES], cosv, sinv)
                           for c in range(KV_WIDTH // LANES)], axis=1).T
    v_t = v.T
    for j in range(rows // LANES):
        ks[j] = k_t[:, j * LANES:(j + 1) * LANES]
        vs[j] = v_t[:, j * LANES:(j + 1) * LANES]

    per_tile = LANES // n_new
    lane = lax.broadcasted_iota(jnp.int32, (1, LANES), 1)
    zeros = jnp.zeros((HEAD_DIM, 2 * LANES), BF16)
    dims_nt = (((1,), (1,)), ((), ()))

    def batch_body(b, carry):
        r0 = pl.multiple_of(b * n_new, n_new)
        off = (b % per_tile) * n_new
        k_old, v_old = ck_ref[b], cv_ref[b]
        k_new, v_new = ks[b // per_tile], vs[b // per_tile]
        mine = (lane >= off) & (lane < off + n_new)

        to_end = n_cache - n_new - off
        to_end = jnp.where(to_end < 0, to_end + LANES, to_end)
        for old, new, out_ref in ((k_old, k_new, knew_ref), (v_old, v_new, vnew_ref)):
            out_ref[b] = jnp.where(lane < n_cache - n_new,
                                   pltpu.roll(old, n_cache - n_new, 1), pltpu.roll(new, to_end, 1))

        bias = jnp.concatenate([jnp.zeros((1, n_cache), F32), jnp.where(mine, 0.0, -jnp.inf)], axis=1)
        k_all = jnp.concatenate([k_old, k_new], axis=1).astype(BF16)
        v_all = jnp.concatenate([v_old, jnp.where(mine, v_new, 0.0)], axis=1).astype(BF16)

        scores = []
        for kv in range(N_KV_HEADS):
            q2 = qs[pl.ds(r0, n_new), kv * 2 * LANES:(kv + 1) * 2 * LANES]
            qst = jnp.concatenate([q2[:, :LANES], q2[:, LANES:]], axis=0)
            k_kv = k_all[kv * HEAD_DIM:(kv + 1) * HEAD_DIM, :]
            scores.append((jnp.dot(qst, jnp.concatenate([k_kv, zeros], axis=0), preferred_element_type=F32),
                           jnp.dot(qst, jnp.concatenate([zeros, k_kv], axis=0), preferred_element_type=F32)))
        probs = []
        for kv, (s_a, s_b) in enumerate(scores):
            sink_a, sink_b = _sink_cols(sink_ref, kv, n_new)
            probs.append(_sink_softmax(s_a, sink_a, bias) + _sink_softmax(s_b, sink_b, bias))
        for kv, (p_a, inv_a, p_b, inv_b) in enumerate(probs):
            v_kv = v_all[kv * HEAD_DIM:(kv + 1) * HEAD_DIM, :]
            o_a = lax.dot_general(p_a, jnp.concatenate([v_kv, zeros], axis=0), dims_nt,
                                  preferred_element_type=F32)
            o_b = lax.dot_general(p_b, jnp.concatenate([zeros, v_kv], axis=0), dims_nt,
                                  preferred_element_type=F32)
            o = o_a * inv_a + o_b * inv_b
            os_[pl.ds(r0, n_new), kv * 2 * LANES:kv * 2 * LANES + LANES] = o[:n_new]
            os_[pl.ds(r0, n_new), kv * 2 * LANES + LANES:(kv + 1) * 2 * LANES] = o[n_new:]
        return carry

    lax.fori_loop(0, n_batch, batch_body, 0)

    z = (os_[...] * _silu(hg_ref[...])).astype(BF16)
    y = jnp.dot(z, wo_ref[...], preferred_element_type=F32)
    y_ref[...] = _layer_norm(DEEPNORM_ALPHA * x + y, lng_ref[...], lnb_ref[...])


def _depthwise(u_scr, c_scr, wdw_ref, bdw_ref, base, out_base, n_rows):
    off = CONV_HIST - (CONV_WIDTH - 1)
    for l in range(D_MODEL // LANES):
        sl = slice(l * LANES, (l + 1) * LANES)
        acc = jnp.zeros((n_rows, LANES), F32)
        for j in range(CONV_WIDTH):
            acc = acc + wdw_ref[j:j + 1, sl] * u_scr[l, pl.ds(base + j + off, n_rows, stride=1), :]
        c_scr[pl.ds(out_base, n_rows), sl] = acc + bdw_ref[:, sl]


def _conv_gate(c, g, cg_ref, cb_ref):
    cn = _layer_norm(c, cg_ref[...], cb_ref[...])
    return (_silu(cn) * _silu(g)).astype(BF16)


def _split_conv_w_in(w_ref):
    return tuple(w_ref.at[:, n * D_MODEL:(n + 1) * D_MODEL] for n in range(3))


def _conv_prompt_kernel(x_ref, w_ref, wo_ref, wdw_ref, bdw_ref, cg_ref, cb_ref,
                        lng_ref, lnb_ref,
                        y_ref, state_ref,
                        u_scr, c_scr):
    i = pl.program_id(0)
    last = pl.num_programs(0) - 1
    tm = x_ref.shape[0]
    wa_ref, wb_ref, wg_ref = _split_conv_w_in(w_ref)
    n_tiles = D_MODEL // LANES

    @pl.when(i == 0)
    def _():
        u_scr[:, 0:CONV_HIST, :] = jnp.zeros((n_tiles, CONV_HIST, LANES), F32)

    x = x_ref[...]
    xb = x.astype(BF16)
    a = jnp.dot(xb, wa_ref[...], preferred_element_type=F32)
    b = jnp.dot(xb, wb_ref[...], preferred_element_type=F32)
    u = a * jax.nn.sigmoid(b)
    for l in range(n_tiles):
        u_scr[l, CONV_HIST:CONV_HIST + tm, :] = u[:, l * LANES:(l + 1) * LANES]

    def row_body(rt, carry):
        base = pl.multiple_of(rt * CONV_ROW_TILE, CONV_ROW_TILE)
        _depthwise(u_scr, c_scr, wdw_ref, bdw_ref, base, base, CONV_ROW_TILE)
        return carry

    lax.fori_loop(0, tm // CONV_ROW_TILE, row_body, 0)

    @pl.when(i == last)
    def _():
        for l in range(n_tiles):
            state_ref[:, l * LANES:(l + 1) * LANES] = (
                u_scr[l, CONV_HIST + tm - (CONV_WIDTH - 1):CONV_HIST + tm, :])

    u_scr[:, 0:CONV_HIST, :] = u_scr[:, tm:tm + CONV_HIST, :]

    g = jnp.dot(xb, wg_ref[...], preferred_element_type=F32)
    z = _conv_gate(c_scr[...], g, cg_ref, cb_ref)
    y = jnp.dot(z, wo_ref[...], preferred_element_type=F32)
    y_ref[...] = _layer_norm(DEEPNORM_ALPHA * x + y, lng_ref[...], lnb_ref[...])


def _conv_sample_kernel(x_ref, st_ref, w_ref, wo_ref, wdw_ref, bdw_ref, cg_ref, cb_ref,
                        lng_ref, lnb_ref,
                        y_ref, state_ref,
                        u_scr, c_slab, c_scr, *, n_batch, n_new):
    hist = CONV_WIDTH - 1
    pad = CONV_HIST - hist
    rows = n_batch * n_new
    wa_ref, wb_ref, wg_ref = _split_conv_w_in(w_ref)
    x = x_ref[...]
    xb = x.astype(BF16)
    a = jnp.dot(xb, wa_ref[...], preferred_element_type=F32)
    b = jnp.dot(xb, wb_ref[...], preferred_element_type=F32)
    u = a * jax.nn.sigmoid(b)
    first_new = (pad + hist) * n_batch
    for l in range(D_MODEL // LANES):
        sl = slice(l * LANES, (l + 1) * LANES)
        u_scr[l, 0:pad * n_batch, :] = jnp.zeros((pad * n_batch, LANES), F32)
        u_scr[l, pad * n_batch:first_new, :] = st_ref[:, sl]
        for bi in range(n_batch):
            u_scr[l, pl.ds(first_new + bi, n_new, stride=n_batch), :] = u[bi * n_new:(bi + 1) * n_new, sl]
        state_ref[:, sl] = u_scr[l, (pad + n_new) * n_batch:(pad + n_new + hist) * n_batch, :]

    def row_body(rt, carry):
        base = pl.multiple_of(rt * CONV_ROW_TILE, CONV_ROW_TILE)
        for l in range(D_MODEL // LANES):
            sl = slice(l * LANES, (l + 1) * LANES)
            acc = jnp.zeros((CONV_ROW_TILE, LANES), F32)
            for j in range(CONV_WIDTH):
                acc = acc + wdw_ref[j:j + 1, sl] * u_scr[l, pl.ds(base + (j + pad) * n_batch, CONV_ROW_TILE), :]
            c_slab[l, pl.ds(base, CONV_ROW_TILE), :] = acc + bdw_ref[:, sl]
        return carry

    lax.fori_loop(0, rows // CONV_ROW_TILE, row_body, 0)

    for l in range(D_MODEL // LANES):
        for bi in range(n_batch):
            c_scr[bi * n_new:(bi + 1) * n_new, l * LANES:(l + 1) * LANES] = (
                c_slab[l, pl.ds(bi, n_new, stride=n_batch), :])

    g = jnp.dot(xb, wg_ref[...], preferred_element_type=F32)
    z = _conv_gate(c_scr[...], g, cg_ref, cb_ref)
    y = jnp.dot(z, wo_ref[...], preferred_element_type=F32)
    y_ref[...] = _layer_norm(DEEPNORM_ALPHA * x + y, lng_ref[...], lnb_ref[...])


def _const_spec(shape):
    return pl.BlockSpec(shape, lambda i: (0,) * len(shape), pipeline_mode=pl.Buffered(1))


def _const_out_spec(shape):
    return pl.BlockSpec(shape, lambda i: (0,) * len(shape), pipeline_mode=pl.Buffered(1))


def _vmem_spec():
    return pl.BlockSpec(memory_space=pltpu.VMEM)


def _smem_spec():
    return pl.BlockSpec(memory_space=pltpu.SMEM)


def _rope_lane_consts():
    half = ROT_DIM // 2
    inv = ROPE_THETA ** (-jnp.arange(half, dtype=F32) * (2.0 / ROT_DIM))
    zeros = jnp.zeros((HEAD_DIM - ROT_DIM,), F32)
    inv_head = jnp.concatenate([inv, inv, zeros])
    sgn_head = jnp.concatenate([-jnp.ones((half,), F32), jnp.ones((half,), F32), zeros])
    reps = LANES // HEAD_DIM
    return jnp.tile(inv_head, reps)[None, :], jnp.tile(sgn_head, reps)[None, :]


def _attn_layer(xp, xs, ck, cv, w_in, sink, w_out, ln_g, ln_b, next_w_in, next_w_out):
    seq = xp.shape[0]
    n_batch, n_cache = ck.shape[0], ck.shape[2]
    n_new = xs.shape[0] // n_batch
    assert n_cache == LANES and LANES % n_new == 0 and xs.shape[0] % LANES == 0
    inv, sgn = _rope_lane_consts()
    lng, lnb = ln_g[None, :], ln_b[None, :]
    params = pltpu.CompilerParams(dimension_semantics=("arbitrary",), vmem_limit_bytes=VMEM_LIMIT_BYTES)

    rows = xs.shape[0]
    k_rows = 2 * LANES
    h, wi, wo = pl.pallas_call(
        _attn_sample_proj_kernel,
        grid=(D_MODEL // k_rows,),
        in_specs=[pl.BlockSpec((rows, k_rows), lambda k: (0, k)),
                  pl.BlockSpec((k_rows, w_in.shape[1]), lambda k: (k, 0)),
                  pl.BlockSpec((k_rows, w_out.shape[1]), lambda k: (k, 0))],
        out_specs=[_const_out_spec((rows, w_in.shape[1])),
                   pl.BlockSpec((k_rows, w_in.shape[1]), lambda k: (k, 0)),
                   pl.BlockSpec((k_rows, w_out.shape[1]), lambda k: (k, 0))],
        out_shape=[jax.ShapeDtypeStruct((rows, w_in.shape[1]), F32),
                   jax.ShapeDtypeStruct(w_in.shape, BF16),
                   jax.ShapeDtypeStruct(w_out.shape, BF16)],
        compiler_params=params,
        name="attn_sample_proj",
    )(xs, w_in, w_out)

    new_t = pltpu.VMEM((rows // LANES, KV_WIDTH, LANES), F32)
    ys, ksn, vsn = pl.pallas_call(
        functools.partial(_attn_sample_kernel, n_batch=n_batch, n_new=n_new, n_cache=n_cache),
        in_specs=[_vmem_spec()] * 7 + [_smem_spec(), _vmem_spec(), _vmem_spec()],
        out_specs=[_vmem_spec()] * 3,
        out_shape=[jax.ShapeDtypeStruct((rows, D_MODEL), F32),
                   jax.ShapeDtypeStruct(ck.shape, F32),
                   jax.ShapeDtypeStruct(cv.shape, F32)],
        scratch_shapes=[pltpu.VMEM((rows, ATTN_WIDTH), BF16), new_t, new_t,
                        pltpu.VMEM((rows, ATTN_WIDTH), F32)],
        compiler_params=pltpu.CompilerParams(vmem_limit_bytes=VMEM_LIMIT_BYTES),
        name="attn_sample",
    )(xs, ck, cv, h, wo, inv, sgn, sink, lng, lnb)

    tm = ROW_BLOCK
    n_steps = seq // tm
    w_rows = next_w_in.shape[0] // n_steps
    assert next_w_in.shape[0] == next_w_out.shape[0] == w_rows * n_steps and w_rows % (2 * SUBLANES) == 0
    kv_scr = pltpu.VMEM((WIN_ROWS + tm, N_KV_HEADS * LANES), BF16)
    yp, kp, vp, next_wi, next_wo = pl.pallas_call(
        _attn_prompt_kernel,
        grid=(n_steps,),
        in_specs=[pl.BlockSpec((tm, D_MODEL), lambda i: (i, 0)),
                  _const_spec(wi.shape), _const_spec(wo.shape),
                  _const_spec(inv.shape), _const_spec(sgn.shape), _smem_spec(),
                  _const_spec(lng.shape), _const_spec(lnb.shape),
                  pl.BlockSpec((w_rows, next_w_in.shape[1]), lambda i: (i, 0)),
                  pl.BlockSpec((w_rows, next_w_out.shape[1]), lambda i: (i, 0))],
        out_specs=[pl.BlockSpec((tm, D_MODEL), lambda i: (i, 0)),
                   pl.BlockSpec((KV_WIDTH, WIN_ROWS), lambda i: (0, 0)),
                   pl.BlockSpec((KV_WIDTH, WIN_ROWS), lambda i: (0, 0)),
                   pl.BlockSpec((w_rows, next_w_in.shape[1]), lambda i: (i, 0)),
                   pl.BlockSpec((w_rows, next_w_out.shape[1]), lambda i: (i, 0))],
        out_shape=[jax.ShapeDtypeStruct((seq, D_MODEL), F32),
                   jax.ShapeDtypeStruct((KV_WIDTH, WIN_ROWS), F32),
                   jax.ShapeDtypeStruct((KV_WIDTH, WIN_ROWS), F32),
                   jax.ShapeDtypeStruct(next_w_in.shape, BF16),
                   jax.ShapeDtypeStruct(next_w_out.shape, BF16)],
        scratch_shapes=[pltpu.VMEM((tm, ATTN_WIDTH), BF16), kv_scr, kv_scr, kv_scr, kv_scr,
                        pltpu.VMEM((tm, ATTN_WIDTH), F32), pltpu.VMEM((4, tm, LANES), F32)],
        compiler_params=pltpu.CompilerParams(dimension_semantics=("arbitrary",),
                                             vmem_limit_bytes=VMEM_CAPACITY_BYTES),
        name="attn_prompt",
    )(xp, wi, wo, inv, sgn, sink, lng, lnb, next_w_in, next_w_out)

    return yp, ys, kp, vp, ksn, vsn, next_wi, next_wo


def _conv_layer(xp, xs, state, n_batch, wi, w_dw, b_dw, cg, cb, wo, ln_g, ln_b):
    seq = xp.shape[0]
    n_new = xs.shape[0] // n_batch
    hist = CONV_WIDTH - 1
    assert state.shape[0] == hist * n_batch and xs.shape[0] % CONV_ROW_TILE == 0
    vecs = [b_dw[None, :], cg[None, :], cb[None, :], ln_g[None, :], ln_b[None, :]]

    tm = ROW_BLOCK
    yp, cp = pl.pallas_call(
        _conv_prompt_kernel,
        grid=(seq // tm,),
        in_specs=[pl.BlockSpec((tm, D_MODEL), lambda i: (i, 0)),
                  _const_spec(wi.shape), _const_spec(wo.shape), _const_spec(w_dw.shape)]
                 + [_const_spec(t.shape) for t in vecs],
        out_specs=[pl.BlockSpec((tm, D_MODEL), lambda i: (i, 0)),
                   pl.BlockSpec((hist, D_MODEL), lambda i: (0, 0))],
        out_shape=[jax.ShapeDtypeStruct((seq, D_MODEL), F32),
                   jax.ShapeDtypeStruct((hist, D_MODEL), F32)],
        scratch_shapes=[pltpu.VMEM((D_MODEL // LANES, CONV_HIST + tm, LANES), F32),
                        pltpu.VMEM((tm, D_MODEL), F32)],
        compiler_params=pltpu.CompilerParams(dimension_semantics=("arbitrary",),
                                             vmem_limit_bytes=VMEM_LIMIT_BYTES),
        name="conv_prompt",
    )(xp, wi, wo, w_dw, *vecs)

    rows = xs.shape[0]
    ys, cs = pl.pallas_call(
        functools.partial(_conv_sample_kernel, n_batch=n_batch, n_new=n_new),
        in_specs=[_vmem_spec()] * 10,
        out_specs=[_vmem_spec()] * 2,
        out_shape=[jax.ShapeDtypeStruct((rows, D_MODEL), F32),
                   jax.ShapeDtypeStruct(state.shape, F32)],
        scratch_shapes=[pltpu.VMEM((D_MODEL // LANES, n_batch * (CONV_HIST + n_new), LANES), F32),
                        pltpu.VMEM((D_MODEL // LANES, rows, LANES), F32),
                        pltpu.VMEM((rows, D_MODEL), F32)],
        compiler_params=pltpu.CompilerParams(vmem_limit_bytes=VMEM_LIMIT_BYTES),
        name="conv_sample",
    )(xs, state, wi, wo, w_dw, *vecs)
    return yp, ys, cp, cs


def kernel(x_prompt, x_sample, cache_k, cache_v, state_conv, attn_w_in, attn_sink, attn_w_out,
           conv_w_in, conv_w_dw, conv_b_dw, conv_ln_g, conv_ln_b, conv_w_out, post_ln_g, post_ln_b):
    batch, seq, _ = x_prompt.shape
    dec_batch, dec_seq, _ = x_sample.shape
    wc = cache_k.shape[2]
    assert batch == 1 and wc == WIN_ROWS and seq % ROW_BLOCK == 0
    xp = x_prompt.reshape(seq, D_MODEL)
    xs = x_sample.reshape(dec_batch * dec_seq, D_MODEL)

    def feature_major(c):
        return jnp.transpose(c, (0, 2, 3, 1)).reshape(c.shape[0], KV_WIDTH, wc)

    def frame_major(c_t, n):
        return jnp.transpose(c_t.reshape(n, N_KV_HEADS, HEAD_DIM, wc), (0, 3, 1, 2))[None]

    xp, xs, kp, vp, ksn, vsn, conv_wi, conv_wo = _attn_layer(
        xp, xs, feature_major(cache_k[0]), feature_major(cache_v[0]),
        attn_w_in[0], attn_sink[0], attn_w_out[0], post_ln_g[0], post_ln_b[0],
        conv_w_in[0], conv_w_out[0])
    hist = state_conv.shape[2]
    state = jnp.transpose(state_conv[0], (1, 0, 2)).reshape(hist * dec_batch, D_MODEL)
    xp, xs, cp, cs = _conv_layer(
        xp, xs, state, dec_batch, conv_wi, conv_w_dw[0], conv_b_dw[0], conv_ln_g[0], conv_ln_b[0],
        conv_wo, post_ln_g[1], post_ln_b[1])
    cs = jnp.transpose(cs.reshape(hist, dec_batch, D_MODEL), (1, 0, 2))

    return (xp.reshape(batch, seq, D_MODEL), xs.reshape(dec_batch, dec_seq, D_MODEL),
            frame_major(kp, 1), frame_major(vp, 1),
            frame_major(ksn, dec_batch), frame_major(vsn, dec_batch),
            cp[None, None], cs[None])
```

```python
import functools

import jax
import jax.numpy as jnp
from jax import lax
from jax.experimental import pallas as pl
from jax.experimental.pallas import tpu as pltpu

F32 = jnp.float32
BF16 = jnp.bfloat16

D_MODEL = 2048
DEPTH = 2
PAST_LEN = 1024
CHUNK = 64
HEAD_DIM = 64
N_HEADS = D_MODEL // HEAD_DIM
N_KV_HEADS = 8
GROUP = N_HEADS // N_KV_HEADS
ATTN_WIDTH = N_HEADS * HEAD_DIM
KV_WIDTH = N_KV_HEADS * HEAD_DIM
WIN_CHUNKS = 2
WIN_ROWS = WIN_CHUNKS * CHUNK
ROT_DIM = HEAD_DIM // 4
ROPE_THETA = 500000.0
CONV_WIDTH = 31
LN_EPS = 1e-5
DEEPNORM_ALPHA = (2.0 * DEPTH) ** 0.25
ATTN_SCALE = HEAD_DIM ** -0.5
LOG2E = 1.4426950408889634
Q_SCALE = ATTN_SCALE * LOG2E

LANES = 128
SUBLANES = 8
VMEM_CAPACITY_BYTES = 64 * 1024 * 1024
VMEM_LIMIT_BYTES = 60 * 1024 * 1024
ROW_BLOCK = 256
CONV_HIST = 32
CONV_ROW_TILE = 32
PAIR_ROWS = 2 * CHUNK
PAIR_KEYS = WIN_ROWS + PAIR_ROWS


def _layer_norm(r, g, b):
    mu = jnp.mean(r, axis=-1, keepdims=True)
    d = r - mu
    var = jnp.mean(d * d, axis=-1, keepdims=True)
    return d * lax.rsqrt(var + LN_EPS) * g + b


def _silu(t):
    return t * jax.nn.sigmoid(t)


def _rope_tables(pos, inv_ref, sgn_ref):
    ang = pos * inv_ref[...]
    sin = jnp.sin(ang)
    return jnp.cos(ang), sin if sgn_ref is None else sin * sgn_ref[...]


def _rope(t, cosv, sinv):
    lane = lax.broadcasted_iota(jnp.int32, (1, LANES), 1) % HEAD_DIM
    partner = jnp.where(lane < ROT_DIM // 2,
                        pltpu.roll(t, LANES - ROT_DIM // 2, 1),
                        pltpu.roll(t, ROT_DIM // 2, 1))
    return t * cosv + partner * sinv


def _lo_hi(col):
    low = lax.broadcasted_iota(jnp.int32, (1, LANES), 1) < HEAD_DIM
    swapped = pltpu.roll(col, HEAD_DIM, 1)
    zero = jnp.zeros_like(col)
    even_lo = jnp.where(low, col, zero).astype(BF16)
    even_hi = jnp.where(low, zero, swapped).astype(BF16)
    odd_lo = jnp.where(low, swapped, zero).astype(BF16)
    odd_hi = jnp.where(low, zero, col).astype(BF16)
    return even_lo, even_hi, odd_lo, odd_hi


def _scores(q2, k_lo, k_hi):
    qst = jnp.concatenate([q2[:, :LANES], q2[:, LANES:]], axis=0)
    dims = (((1,), (1,)), ((), ()))
    return (lax.dot_general(qst, k_lo, dims, preferred_element_type=F32),
            lax.dot_general(qst, k_hi, dims, preferred_element_type=F32))


def _sink_softmax(s, sk, bias):
    if bias is not None:
        s = s + bias
    m = jnp.maximum(jnp.max(s, axis=-1, keepdims=True), sk)
    p = jnp.exp2(s - m)
    denom = jnp.sum(p, axis=-1, keepdims=True) + jnp.exp2(sk - m)
    return p.astype(BF16), 1.0 / denom


def _weighted_values(p_a, inv_a, p_b, inv_b, v_lo, v_hi):
    o_a = jnp.dot(p_a, v_lo, preferred_element_type=F32)
    o_b = jnp.dot(p_b, v_hi, preferred_element_type=F32)
    return o_a * inv_a + o_b * inv_b


def _attend_heads(q_of, k_of, v_of, sink_ref, q_rows, bias):
    scores = [_scores(q_of(kv), *k_of(kv)) for kv in range(N_KV_HEADS)]
    probs = []
    for kv, (s_a, s_b) in enumerate(scores):
        sink_a, sink_b = _sink_cols(sink_ref, kv, q_rows)
        probs.append(_sink_softmax(s_a, sink_a, bias) + _sink_softmax(s_b, sink_b, bias))
    return [_weighted_values(*probs[kv], *v_of(kv)) for kv in range(N_KV_HEADS)]


def _sink_cols(sink_ref, kv, q_rows):
    row = lax.broadcasted_iota(jnp.int32, (2 * q_rows, 1), 0)
    first = row < q_rows
    h = kv * GROUP
    return (jnp.where(first, sink_ref[h + 0] * LOG2E, sink_ref[h + 2] * LOG2E),
            jnp.where(first, sink_ref[h + 1] * LOG2E, sink_ref[h + 3] * LOG2E))


def _split_attn_w_in(w_ref):
    k0, v0, g0 = ATTN_WIDTH, ATTN_WIDTH + KV_WIDTH, ATTN_WIDTH + 2 * KV_WIDTH
    return w_ref.at[:, 0:k0], w_ref.at[:, k0:v0], w_ref.at[:, v0:g0], w_ref.at[:, g0:g0 + ATTN_WIDTH]


def _attn_prompt_kernel(x_ref, w_ref, wo_ref, inv_ref, sgn_ref,
                        sink_ref, lng_ref, lnb_ref, next_wi_ref, next_wo_ref,
                        y_ref, knew_ref, vnew_ref, next_wi_bf_ref, next_wo_bf_ref,
                        qs, klo, khi, vlo, vhi, os_, rope_tab):
    i = pl.program_id(0)
    last = pl.num_programs(0) - 1
    tm = x_ref.shape[0]
    next_wi_bf_ref[...] = next_wi_ref[...].astype(BF16)
    next_wo_bf_ref[...] = next_wo_ref[...].astype(BF16)
    wq_ref, wk_ref, wv_ref, wg_ref = _split_attn_w_in(w_ref)

    @pl.when(i == 0)
    def _():
        for ref in (klo, khi, vlo, vhi):
            ref[0:WIN_ROWS, :] = jnp.zeros((WIN_ROWS, ref.shape[1]), BF16)
        off = lax.broadcasted_iota(jnp.int32, (tm, 1), 0).astype(F32)
        cos_r, sin_r = _rope_tables(off, inv_ref, None)
        rope_tab[0] = cos_r
        rope_tab[1] = sin_r
        rope_tab[2] = cos_r * sgn_ref[...]
        rope_tab[3] = sin_r * sgn_ref[...]

    x = x_ref[...]
    xb = x.astype(BF16)
    start = jnp.full((1, 1), i * tm, jnp.int32).astype(F32)
    cos_s, sin_s = _rope_tables(start, inv_ref, None)
    cosv = cos_s * rope_tab[0] - sin_s * rope_tab[1]
    sinv = sin_s * rope_tab[2] + cos_s * rope_tab[3]

    q = jnp.dot(xb, wq_ref[...], preferred_element_type=F32)
    for j in range(ATTN_WIDTH // LANES):
        sl = slice(j * LANES, (j + 1) * LANES)
        qs[:, sl] = (_rope(q[:, sl], cosv, sinv) * Q_SCALE).astype(BF16)

    k = jnp.dot(xb, wk_ref[...], preferred_element_type=F32)
    v = jnp.dot(xb, wv_ref[...], preferred_element_type=F32)
    new_rows = slice(WIN_ROWS, WIN_ROWS + tm)
    for c in range(KV_WIDTH // LANES):
        sl = slice(c * LANES, (c + 1) * LANES)
        kc = _rope(k[:, sl], cosv, sinv)
        vc = v[:, sl]

        @pl.when(i == last)
        def _():
            knew_ref[sl, :] = kc[tm - WIN_ROWS:, :].T
            vnew_ref[sl, :] = vc[tm - WIN_ROWS:, :].T

        ev = slice(2 * c * LANES, (2 * c + 1) * LANES)
        od = slice((2 * c + 1) * LANES, (2 * c + 2) * LANES)
        klo[new_rows, ev], khi[new_rows, ev], klo[new_rows, od], khi[new_rows, od] = _lo_hi(kc)
        vlo[new_rows, ev], vhi[new_rows, ev], vlo[new_rows, od], vhi[new_rows, od] = _lo_hi(vc)

    row = lax.broadcasted_iota(jnp.int32, (2 * PAIR_ROWS, 1), 0)
    col = lax.broadcasted_iota(jnp.int32, (1, PAIR_KEYS), 1)
    first_col = jnp.where(row % PAIR_ROWS >= CHUNK, CHUNK, 0)
    in_window = (col >= first_col) & (col < first_col + (PAIR_KEYS - CHUNK))
    window_bias = jnp.where(in_window, 0.0, -jnp.inf)

    def attend_pair(p):
        r0 = p * PAIR_ROWS
        first_key_chunk = i * (tm // CHUNK) + 2 * p - WIN_CHUNKS
        started = col >= jnp.maximum(0, -first_key_chunk) * CHUNK
        bias = window_bias + jnp.where(started, 0.0, -jnp.inf)
        keys = pl.ds(r0, PAIR_KEYS)

        def head_lanes(kv):
            return slice(kv * LANES, (kv + 1) * LANES)

        outs = _attend_heads(
            lambda kv: qs[pl.ds(r0, PAIR_ROWS), kv * 2 * LANES:(kv + 1) * 2 * LANES],
            lambda kv: (klo[keys, head_lanes(kv)], khi[keys, head_lanes(kv)]),
            lambda kv: (vlo[keys, head_lanes(kv)], vhi[keys, head_lanes(kv)]),
            sink_ref, PAIR_ROWS, bias)
        for kv, o in enumerate(outs):
            os_[pl.ds(r0, PAIR_ROWS), kv * 2 * LANES:kv * 2 * LANES + LANES] = o[:PAIR_ROWS]
            os_[pl.ds(r0, PAIR_ROWS), kv * 2 * LANES + LANES:(kv + 1) * 2 * LANES] = o[PAIR_ROWS:]

    for p in range(tm // PAIR_ROWS):
        attend_pair(p)

    for ref in (klo, khi, vlo, vhi):
        ref[0:WIN_ROWS, :] = ref[tm:tm + WIN_ROWS, :]

    g = jnp.dot(xb, wg_ref[...], preferred_element_type=F32)
    z = (os_[...] * _silu(g)).astype(BF16)
    y = jnp.dot(z, wo_ref[...], preferred_element_type=F32)
    y_ref[...] = _layer_norm(DEEPNORM_ALPHA * x + y, lng_ref[...], lnb_ref[...])


def _attn_sample_proj_kernel(xk_ref, w_ref, wo_ref, h_ref, w_bf_ref, wo_bf_ref):
    kstep = pl.program_id(0)
    w_bf = w_ref[...].astype(BF16)
    w_bf_ref[...] = w_bf
    wo_bf_ref[...] = wo_ref[...].astype(BF16)
    part = jnp.dot(xk_ref[...].astype(BF16), w_bf, preferred_element_type=F32)

    @pl.when(kstep == 0)
    def _():
        h_ref[...] = part

    @pl.when(kstep > 0)
    def _():
        h_ref[...] += part


def _attn_sample_kernel(x_ref, ck_ref, cv_ref, h_ref, wo_ref, inv_ref, sgn_ref, sink_ref, lng_ref, lnb_ref,
                        y_ref, knew_ref, vnew_ref, qs, ks, vs, os_, *, n_batch, n_new, n_cache):
    rows = n_batch * n_new
    hq_ref, hk_ref, hv_ref, hg_ref = _split_attn_w_in(h_ref)
    x = x_ref[...]
    pos = (PAST_LEN + lax.broadcasted_iota(jnp.int32, (rows, 1), 0) % n_new).astype(F32)
    cosv, sinv = _rope_tables(pos, inv_ref, sgn_ref)

    for j in range(ATTN_WIDTH // LANES):
        sl = slice(j * LANES, (j + 1) * LANES)
        qs[:, sl] = (_rope(hq_ref[:, sl], cosv, sinv) * Q_SCALE).astype(BF16)
    k = hk_ref[...]
    v = hv_ref[...]
    k_t = jnp.concatenate([_rope(k[:, c * LANES:(c + 1) * LANES], cosv, sinv)
                           for c in range(KV_WIDTH // LANES)], axis=1).T
    v_t = v.T
    for j in range(rows // LANES):
        ks[j] = k_t[:, j * LANES:(j + 1) * LANES]
        vs[j] = v_t[:, j * LANES:(j + 1) * LANES]

    per_tile = LANES // n_new
    lane = lax.broadcasted_iota(jnp.int32, (1, LANES), 1)
    zeros = jnp.zeros((HEAD_DIM, 2 * LANES), BF16)
    dims_nt = (((1,), (1,)), ((), ()))

    def batch_body(b, carry):
        r0 = pl.multiple_of(b * n_new, n_new)
        off = (b % per_tile) * n_new
        k_old, v_old = ck_ref[b], cv_ref[b]
        k_new, v_new = ks[b // per_tile], vs[b // per_tile]
        mine = (lane >= off) & (lane < off + n_new)

        to_end = n_cache - n_new - off
        to_end = jnp.where(to_end < 0, to_end + LANES, to_end)
        for old, new, out_ref in ((k_old, k_new, knew_ref), (v_old, v_new, vnew_ref)):
            out_ref[b] = jnp.where(lane < n_cache - n_new,
                                   pltpu.roll(old, n_cache - n_new, 1), pltpu.roll(new, to_end, 1))

        bias = jnp.concatenate([jnp.zeros((1, n_cache), F32), jnp.where(mine, 0.0, -jnp.inf)], axis=1)
        k_all = jnp.concatenate([k_old, k_new], axis=1).astype(BF16)
        v_all = jnp.concatenate([v_old, jnp.where(mine, v_new, 0.0)], axis=1).astype(BF16)

        scores = []
        for kv in range(N_KV_HEADS):
            q2 = qs[pl.ds(r0, n_new), kv * 2 * LANES:(kv + 1) * 2 * LANES]
            qst = jnp.concatenate([q2[:, :LANES], q2[:, LANES:]], axis=0)
            k_kv = k_all[kv * HEAD_DIM:(kv + 1) * HEAD_DIM, :]
            scores.append((jnp.dot(qst, jnp.concatenate([k_kv, zeros], axis=0), preferred_element_type=F32),
                           jnp.dot(qst, jnp.concatenate([zeros, k_kv], axis=0), preferred_element_type=F32)))
        probs = []
        for kv, (s_a, s_b) in enumerate(scores):
            sink_a, sink_b = _sink_cols(sink_ref, kv, n_new)
            probs.append(_sink_softmax(s_a, sink_a, bias) + _sink_softmax(s_b, sink_b, bias))
        for kv, (p_a, inv_a, p_b, inv_b) in enumerate(probs):
            v_kv = v_all[kv * HEAD_DIM:(kv + 1) * HEAD_DIM, :]
            o_a = lax.dot_general(p_a, jnp.concatenate([v_kv, zeros], axis=0), dims_nt,
                                  preferred_element_type=F32)
            o_b = lax.dot_general(p_b, jnp.concatenate([zeros, v_kv], axis=0), dims_nt,
                                  preferred_element_type=F32)
            o = o_a * inv_a + o_b * inv_b
            os_[pl.ds(r0, n_new), kv * 2 * LANES:kv * 2 * LANES + LANES] = o[:n_new]
            os_[pl.ds(r0, n_new), kv * 2 * LANES + LANES:(kv + 1) * 2 * LANES] = o[n_new:]
        return carry

    lax.fori_loop(0, n_batch, batch_body, 0)

    z = (os_[...] * _silu(hg_ref[...])).astype(BF16)
    y = jnp.dot(z, wo_ref[...], preferred_element_type=F32)
    y_ref[...] = _layer_norm(DEEPNORM_ALPHA * x + y, lng_ref[...], lnb_ref[...])


def _depthwise(u_scr, c_scr, wdw_ref, bdw_ref, base, out_base, n_rows):
    off = CONV_HIST - (CONV_WIDTH - 1)
    for l in range(D_MODEL // LANES):
        sl = slice(l * LANES, (l + 1) * LANES)
        acc = jnp.zeros((n_rows, LANES), F32)
        for j in range(CONV_WIDTH):
            acc = acc + wdw_ref[j:j + 1, sl] * u_scr[l, pl.ds(base + j + off, n_rows, stride=1), :]
        c_scr[pl.ds(out_base, n_rows), sl] = acc + bdw_ref[:, sl]


def _conv_gate(c, g, cg_ref, cb_ref):
    cn = _layer_norm(c, cg_ref[...], cb_ref[...])
    return (_silu(cn) * _silu(g)).astype(BF16)


def _split_conv_w_in(w_ref):
    return tuple(w_ref.at[:, n * D_MODEL:(n + 1) * D_MODEL] for n in range(3))


def _conv_prompt_kernel(x_ref, w_ref, wo_ref, wdw_ref, bdw_ref, cg_ref, cb_ref,
                        lng_ref, lnb_ref,
                        y_ref, state_ref,
                        u_scr, c_scr):
    i = pl.program_id(0)
    last = pl.num_programs(0) - 1
    tm = x_ref.shape[0]
    wa_ref, wb_ref, wg_ref = _split_conv_w_in(w_ref)
    n_tiles = D_MODEL // LANES

    @pl.when(i == 0)
    def _():
        u_scr[:, 0:CONV_HIST, :] = jnp.zeros((n_tiles, CONV_HIST, LANES), F32)

    x = x_ref[...]
    xb = x.astype(BF16)
    a = jnp.dot(xb, wa_ref[...], preferred_element_type=F32)
    b = jnp.dot(xb, wb_ref[...], preferred_element_type=F32)
    u = a * jax.nn.sigmoid(b)
    for l in range(n_tiles):
        u_scr[l, CONV_HIST:CONV_HIST + tm, :] = u[:, l * LANES:(l + 1) * LANES]

    def row_body(rt, carry):
        base = pl.multiple_of(rt * CONV_ROW_TILE, CONV_ROW_TILE)
        _depthwise(u_scr, c_scr, wdw_ref, bdw_ref, base, base, CONV_ROW_TILE)
        return carry

    lax.fori_loop(0, tm // CONV_ROW_TILE, row_body, 0)

    @pl.when(i == last)
    def _():
        for l in range(n_tiles):
            state_ref[:, l * LANES:(l + 1) * LANES] = (
                u_scr[l, CONV_HIST + tm - (CONV_WIDTH - 1):CONV_HIST + tm, :])

    u_scr[:, 0:CONV_HIST, :] = u_scr[:, tm:tm + CONV_HIST, :]

    g = jnp.dot(xb, wg_ref[...], preferred_element_type=F32)
    z = _conv_gate(c_scr[...], g, cg_ref, cb_ref)
    y = jnp.dot(z, wo_ref[...], preferred_element_type=F32)
    y_ref[...] = _layer_norm(DEEPNORM_ALPHA * x + y, lng_ref[...], lnb_ref[...])


def _conv_sample_kernel(x_ref, st_ref, w_ref, wo_ref, wdw_ref, bdw_ref, cg_ref, cb_ref,
                        lng_ref, lnb_ref,
                        y_ref, state_ref,
                        u_scr, c_slab, c_scr, *, n_batch, n_new):
    hist = CONV_WIDTH - 1
    pad = CONV_HIST - hist
    rows = n_batch * n_new
    wa_ref, wb_ref, wg_ref = _split_conv_w_in(w_ref)
    x = x_ref[...]
    xb = x.astype(BF16)
    a = jnp.dot(xb, wa_ref[...], preferred_element_type=F32)
    b = jnp.dot(xb, wb_ref[...], preferred_element_type=F32)
    u = a * jax.nn.sigmoid(b)
    first_new = (pad + hist) * n_batch
    for l in range(D_MODEL // LANES):
        sl = slice(l * LANES, (l + 1) * LANES)
        u_scr[l, 0:pad * n_batch, :] = jnp.zeros((pad * n_batch, LANES), F32)
        u_scr[l, pad * n_batch:first_new, :] = st_ref[:, sl]
        for bi in range(n_batch):
            u_scr[l, pl.ds(first_new + bi, n_new, stride=n_batch), :] = u[bi * n_new:(bi + 1) * n_new, sl]
        state_ref[:, sl] = u_scr[l, (pad + n_new) * n_batch:(pad + n_new + hist) * n_batch, :]

    def row_body(rt, carry):
        base = pl.multiple_of(rt * CONV_ROW_TILE, CONV_ROW_TILE)
        for l in range(D_MODEL // LANES):
            sl = slice(l * LANES, (l + 1) * LANES)
            acc = jnp.zeros((CONV_ROW_TILE, LANES), F32)
            for j in range(CONV_WIDTH):
                acc = acc + wdw_ref[j:j + 1, sl] * u_scr[l, pl.ds(base + (j + pad) * n_batch, CONV_ROW_TILE), :]
            c_slab[l, pl.ds(base, CONV_ROW_TILE), :] = acc + bdw_ref[:, sl]
        return carry

    lax.fori_loop(0, rows // CONV_ROW_TILE, row_body, 0)

    for l in range(D_MODEL // LANES):
        for bi in range(n_batch):
            c_scr[bi * n_new:(bi + 1) * n_new, l * LANES:(l + 1) * LANES] = (
                c_slab[l, pl.ds(bi, n_new, stride=n_batch), :])

    g = jnp.dot(xb, wg_ref[...], preferred_element_type=F32)
    z = _conv_gate(c_scr[...], g, cg_ref, cb_ref)
    y = jnp.dot(z, wo_ref[...], preferred_element_type=F32)
    y_ref[...] = _layer_norm(DEEPNORM_ALPHA * x + y, lng_ref[...], lnb_ref[...])


def _const_spec(shape):
    return pl.BlockSpec(shape, lambda i: (0,) * len(shape), pipeline_mode=pl.Buffered(1))


def _const_out_spec(shape):
    return pl.BlockSpec(shape, lambda i: (0,) * len(shape), pipeline_mode=pl.Buffered(1))


def _vmem_spec():
    return pl.BlockSpec(memory_space=pltpu.VMEM)


def _smem_spec():
    return pl.BlockSpec(memory_space=pltpu.SMEM)


def _rope_lane_consts():
    half = ROT_DIM // 2
    inv = ROPE_THETA ** (-jnp.arange(half, dtype=F32) * (2.0 / ROT_DIM))
    zeros = jnp.zeros((HEAD_DIM - ROT_DIM,), F32)
    inv_head = jnp.concatenate([inv, inv, zeros])
    sgn_head = jnp.concatenate([-jnp.ones((half,), F32), jnp.ones((half,), F32), zeros])
    reps = LANES // HEAD_DIM
    return jnp.tile(inv_head, reps)[None, :], jnp.tile(sgn_head, reps)[None, :]


def _attn_layer(xp, xs, ck, cv, w_in, sink, w_out, ln_g, ln_b, next_w_in, next_w_out):
    seq = xp.shape[0]
    n_batch, n_cache = ck.shape[0], ck.shape[2]
    n_new = xs.shape[0] // n_batch
    assert n_cache == LANES and LANES % n_new == 0 and xs.shape[0] % LANES == 0
    inv, sgn = _rope_lane_consts()
    lng, lnb = ln_g[None, :], ln_b[None, :]
    params = pltpu.CompilerParams(dimension_semantics=("arbitrary",), vmem_limit_bytes=VMEM_LIMIT_BYTES)

    rows = xs.shape[0]
    k_rows = 2 * LANES
    h, wi, wo = pl.pallas_call(
        _attn_sample_proj_kernel,
        grid=(D_MODEL // k_rows,),
        in_specs=[pl.BlockSpec((rows, k_rows), lambda k: (0, k)),
                  pl.BlockSpec((k_rows, w_in.shape[1]), lambda k: (k, 0)),
                  pl.BlockSpec((k_rows, w_out.shape[1]), lambda k: (k, 0))],
        out_specs=[_const_out_spec((rows, w_in.shape[1])),
                   pl.BlockSpec((k_rows, w_in.shape[1]), lambda k: (k, 0)),
                   pl.BlockSpec((k_rows, w_out.shape[1]), lambda k: (k, 0))],
        out_shape=[jax.ShapeDtypeStruct((rows, w_in.shape[1]), F32),
                   jax.ShapeDtypeStruct(w_in.shape, BF16),
                   jax.ShapeDtypeStruct(w_out.shape, BF16)],
        compiler_params=params,
        name="attn_sample_proj",
    )(xs, w_in, w_out)

    new_t = pltpu.VMEM((rows // LANES, KV_WIDTH, LANES), F32)
    ys, ksn, vsn = pl.pallas_call(
        functools.partial(_attn_sample_kernel, n_batch=n_batch, n_new=n_new, n_cache=n_cache),
        in_specs=[_vmem_spec()] * 7 + [_smem_spec(), _vmem_spec(), _vmem_spec()],
        out_specs=[_vmem_spec()] * 3,
        out_shape=[jax.ShapeDtypeStruct((rows, D_MODEL), F32),
                   jax.ShapeDtypeStruct(ck.shape, F32),
                   jax.ShapeDtypeStruct(cv.shape, F32)],
        scratch_shapes=[pltpu.VMEM((rows, ATTN_WIDTH), BF16), new_t, new_t,
                        pltpu.VMEM((rows, ATTN_WIDTH), F32)],
        compiler_params=pltpu.CompilerParams(vmem_limit_bytes=VMEM_LIMIT_BYTES),
        name="attn_sample",
    )(xs, ck, cv, h, wo, inv, sgn, sink, lng, lnb)

    tm = ROW_BLOCK
    n_steps = seq // tm
    w_rows = next_w_in.shape[0] // n_steps
    assert next_w_in.shape[0] == next_w_out.shape[0] == w_rows * n_steps and w_rows % (2 * SUBLANES) == 0
    kv_scr = pltpu.VMEM((WIN_ROWS + tm, N_KV_HEADS * LANES), BF16)
    yp, kp, vp, next_wi, next_wo = pl.pallas_call(
        _attn_prompt_kernel,
        grid=(n_steps,),
        in_specs=[pl.BlockSpec((tm, D_MODEL), lambda i: (i, 0)),
                  _const_spec(wi.shape), _const_spec(wo.shape),
                  _const_spec(inv.shape), _const_spec(sgn.shape), _smem_spec(),
                  _const_spec(lng.shape), _const_spec(lnb.shape),
                  pl.BlockSpec((w_rows, next_w_in.shape[1]), lambda i: (i, 0)),
                  pl.BlockSpec((w_rows, next_w_out.shape[1]), lambda i: (i, 0))],
        out_specs=[pl.BlockSpec((tm, D_MODEL), lambda i: (i, 0)),
                   pl.BlockSpec((KV_WIDTH, WIN_ROWS), lambda i: (0, 0)),
                   pl.BlockSpec((KV_WIDTH, WIN_ROWS), lambda i: (0, 0)),
                   pl.BlockSpec((w_rows, next_w_in.shape[1]), lambda i: (i, 0)),
                   pl.BlockSpec((w_rows, next_w_out.shape[1]), lambda i: (i, 0))],
        out_shape=[jax.ShapeDtypeStruct((seq, D_MODEL), F32),
                   jax.ShapeDtypeStruct((KV_WIDTH, WIN_ROWS), F32),
                   jax.ShapeDtypeStruct((KV_WIDTH, WIN_ROWS), F32),
                   jax.ShapeDtypeStruct(next_w_in.shape, BF16),
                   jax.ShapeDtypeStruct(next_w_out.shape, BF16)],
        scratch_shapes=[pltpu.VMEM((tm, ATTN_WIDTH), BF16), kv_scr, kv_scr, kv_scr, kv_scr,
                        pltpu.VMEM((tm, ATTN_WIDTH), F32), pltpu.VMEM((4, tm, LANES), F32)],
        compiler_params=pltpu.CompilerParams(dimension_semantics=("arbitrary",),
                                             vmem_limit_bytes=VMEM_CAPACITY_BYTES),
        name="attn_prompt",
    )(xp, wi, wo, inv, sgn, sink, lng, lnb, next_w_in, next_w_out)

    return yp, ys, kp, vp, ksn, vsn, next_wi, next_wo


def _conv_layer(xp, xs, state, n_batch, wi, w_dw, b_dw, cg, cb, wo, ln_g, ln_b):
    seq = xp.shape[0]
    n_new = xs.shape[0] // n_batch
    hist = CONV_WIDTH - 1
    assert state.shape[0] == hist * n_batch and xs.shape[0] % CONV_ROW_TILE == 0
    vecs = [b_dw[None, :], cg[None, :], cb[None, :], ln_g[None, :], ln_b[None, :]]

    tm = ROW_BLOCK
    yp, cp = pl.pallas_call(
        _conv_prompt_kernel,
        grid=(seq // tm,),
        in_specs=[pl.BlockSpec((tm, D_MODEL), lambda i: (i, 0)),
                  _const_spec(wi.shape), _const_spec(wo.shape), _const_spec(w_dw.shape)]
                 + [_const_spec(t.shape) for t in vecs],
        out_specs=[pl.BlockSpec((tm, D_MODEL), lambda i: (i, 0)),
                   pl.BlockSpec((hist, D_MODEL), lambda i: (0, 0))],
        out_shape=[jax.ShapeDtypeStruct((seq, D_MODEL), F32),
                   jax.ShapeDtypeStruct((hist, D_MODEL), F32)],
        scratch_shapes=[pltpu.VMEM((D_MODEL // LANES, CONV_HIST + tm, LANES), F32),
                        pltpu.VMEM((tm, D_MODEL), F32)],
        compiler_params=pltpu.CompilerParams(dimension_semantics=("arbitrary",),
                                             vmem_limit_bytes=VMEM_LIMIT_BYTES),
        name="conv_prompt",
    )(xp, wi, wo, w_dw, *vecs)

    rows = xs.shape[0]
    ys, cs = pl.pallas_call(
        functools.partial(_conv_sample_kernel, n_batch=n_batch, n_new=n_new),
        in_specs=[_vmem_spec()] * 10,
        out_specs=[_vmem_spec()] * 2,
        out_shape=[jax.ShapeDtypeStruct((rows, D_MODEL), F32),
                   jax.ShapeDtypeStruct(state.shape, F32)],
        scratch_shapes=[pltpu.VMEM((D_MODEL // LANES, n_batch * (CONV_HIST + n_new), LANES), F32),
                        pltpu.VMEM((D_MODEL // LANES, rows, LANES), F32),
                        pltpu.VMEM((rows, D_MODEL), F32)],
        compiler_params=pltpu.CompilerParams(vmem_limit_bytes=VMEM_LIMIT_BYTES),
        name="conv_sample",
    )(xs, state, wi, wo, w_dw, *vecs)
    return yp, ys, cp, cs


def kernel(x_prompt, x_sample, cache_k, cache_v, state_conv, attn_w_in, attn_sink, attn_w_out,
           conv_w_in, conv_w_dw, conv_b_dw, conv_ln_g, conv_ln_b, conv_w_out, post_ln_g, post_ln_b):
    batch, seq, _ = x_prompt.shape
    dec_batch, dec_seq, _ = x_sample.shape
    wc = cache_k.shape[2]
    assert batch == 1 and wc == WIN_ROWS and seq % ROW_BLOCK == 0
    xp = x_prompt.reshape(seq, D_MODEL)
    xs = x_sample.reshape(dec_batch * dec_seq, D_MODEL)

    def feature_major(c):
        return jnp.transpose(c, (0, 2, 3, 1)).reshape(c.shape[0], KV_WIDTH, wc)

    def frame_major(c_t, n):
        return jnp.transpose(c_t.reshape(n, N_KV_HEADS, HEAD_DIM, wc), (0, 3, 1, 2))[None]

    xp, xs, kp, vp, ksn, vsn, conv_wi, conv_wo = _attn_layer(
        xp, xs, feature_major(cache_k[0]), feature_major(cache_v[0]),
        attn_w_in[0], attn_sink[0], attn_w_out[0], post_ln_g[0], post_ln_b[0],
        conv_w_in[0], conv_w_out[0])
    hist = state_conv.shape[2]
    state = jnp.transpose(state_conv[0], (1, 0, 2)).reshape(hist * dec_batch, D_MODEL)
    xp, xs, cp, cs = _conv_layer(
        xp, xs, state, dec_batch, conv_wi, conv_w_dw[0], conv_b_dw[0], conv_ln_g[0], conv_ln_b[0],
        conv_wo, post_ln_g[1], post_ln_b[1])
    cs = jnp.transpose(cs.reshape(hist, dec_batch, D_MODEL), (1, 0, 2))

    return (xp.reshape(batch, seq, D_MODEL), xs.reshape(dec_batch, dec_seq, D_MODEL),
            frame_major(kp, 1), frame_major(vp, 1),
            frame_major(ksn, dec_batch), frame_major(vsn, dec_batch),
            cp[None, None], cs[None])
```

```python
import functools

import jax
import jax.numpy as jnp
from jax import lax
from jax.experimental import pallas as pl
from jax.experimental.pallas import tpu as pltpu

F32 = jnp.float32
BF16 = jnp.bfloat16

D_MODEL = 2048
DEPTH = 2
PAST_LEN = 1024
CHUNK = 64
HEAD_DIM = 64
N_HEADS = D_MODEL // HEAD_DIM
N_KV_HEADS = 8
GROUP = N_HEADS // N_KV_HEADS
ATTN_WIDTH = N_HEADS * HEAD_DIM
KV_WIDTH = N_KV_HEADS * HEAD_DIM
WIN_CHUNKS = 2
WIN_ROWS = WIN_CHUNKS * CHUNK
ROT_DIM = HEAD_DIM // 4
ROPE_THETA = 500000.0
CONV_WIDTH = 31
LN_EPS = 1e-5
DEEPNORM_ALPHA = (2.0 * DEPTH) ** 0.25
ATTN_SCALE = HEAD_DIM ** -0.5
LOG2E = 1.4426950408889634
Q_SCALE = ATTN_SCALE * LOG2E

LANES = 128
SUBLANES = 8
VMEM_LIMIT_BYTES = 60 * 1024 * 1024
ROW_BLOCK = 256
CONV_HIST = 32
CONV_ROW_TILE = 32
PAIR_ROWS = 2 * CHUNK
PAIR_KEYS = WIN_ROWS + PAIR_ROWS


def _layer_norm(r, g, b):
    mu = jnp.mean(r, axis=-1, keepdims=True)
    d = r - mu
    var = jnp.mean(d * d, axis=-1, keepdims=True)
    return d * lax.rsqrt(var + LN_EPS) * g + b


def _silu(t):
    return t * jax.nn.sigmoid(t)


def _rope_tables(pos, inv_ref, sgn_ref):
    ang = pos * inv_ref[...]
    sin = jnp.sin(ang)
    return jnp.cos(ang), sin if sgn_ref is None else sin * sgn_ref[...]


def _rope(t, cosv, sinv):
    lane = lax.broadcasted_iota(jnp.int32, (1, LANES), 1) % HEAD_DIM
    partner = jnp.where(lane < ROT_DIM // 2,
                        pltpu.roll(t, LANES - ROT_DIM // 2, 1),
                        pltpu.roll(t, ROT_DIM // 2, 1))
    return t * cosv + partner * sinv


def _lo_hi(col):
    low = lax.broadcasted_iota(jnp.int32, (1, LANES), 1) < HEAD_DIM
    swapped = pltpu.roll(col, HEAD_DIM, 1)
    zero = jnp.zeros_like(col)
    even_lo = jnp.where(low, col, zero).astype(BF16)
    even_hi = jnp.where(low, zero, swapped).astype(BF16)
    odd_lo = jnp.where(low, swapped, zero).astype(BF16)
    odd_hi = jnp.where(low, zero, col).astype(BF16)
    return even_lo, even_hi, odd_lo, odd_hi


def _sink_softmax(s, sk, bias):
    if bias is not None:
        s = s + bias
    m = jnp.maximum(jnp.max(s, axis=-1, keepdims=True), sk)
    p = jnp.exp2(s - m)
    denom = jnp.sum(p, axis=-1, keepdims=True) + jnp.exp2(sk - m)
    return p.astype(BF16), 1.0 / denom


def _sink_cols(sink_ref, kv, q_rows):
    row = lax.broadcasted_iota(jnp.int32, (2 * q_rows, 1), 0)
    first = row < q_rows
    h = kv * GROUP
    return (jnp.where(first, sink_ref[h + 0] * LOG2E, sink_ref[h + 2] * LOG2E),
            jnp.where(first, sink_ref[h + 1] * LOG2E, sink_ref[h + 3] * LOG2E))


def _split_attn_w_in(w_ref):
    k0, v0, g0 = ATTN_WIDTH, ATTN_WIDTH + KV_WIDTH, ATTN_WIDTH + 2 * KV_WIDTH
    return w_ref.at[:, 0:k0], w_ref.at[:, k0:v0], w_ref.at[:, v0:g0], w_ref.at[:, g0:g0 + ATTN_WIDTH]


def _attn_prompt_kernel(x_ref, w_ref, wo_ref, inv_ref, sgn_ref,
                        sink_ref, lng_ref, lnb_ref, next_wi_ref, next_wo_ref,
                        y_ref, knew_ref, vnew_ref, next_wi_bf_ref, next_wo_bf_ref,
                        qs, klo, khi, vlo, vhi, os_, rope_tab):
    i = pl.program_id(0)
    last = pl.num_programs(0) - 1
    tm = x_ref.shape[0]
    next_wi_bf_ref[...] = next_wi_ref[...].astype(BF16)
    next_wo_bf_ref[...] = next_wo_ref[...].astype(BF16)
    wq_ref, wk_ref, wv_ref, wg_ref = _split_attn_w_in(w_ref)

    @pl.when(i == 0)
    def _():
        for ref in (klo, khi, vlo, vhi):
            ref[0:WIN_ROWS, :] = jnp.zeros((WIN_ROWS, ref.shape[1]), BF16)
        off = lax.broadcasted_iota(jnp.int32, (tm, 1), 0).astype(F32)
        cos_r, sin_r = _rope_tables(off, inv_ref, None)
        rope_tab[0] = cos_r
        rope_tab[1] = sin_r
        rope_tab[2] = cos_r * sgn_ref[...]
        rope_tab[3] = sin_r * sgn_ref[...]

    x = x_ref[...]
    xb = x.astype(BF16)
    start = jnp.full((1, 1), i * tm, jnp.int32).astype(F32)
    cos_s, sin_s = _rope_tables(start, inv_ref, None)
    cosv = cos_s * rope_tab[0] - sin_s * rope_tab[1]
    sinv = sin_s * rope_tab[2] + cos_s * rope_tab[3]

    q = jnp.dot(xb, wq_ref[...], preferred_element_type=F32)
    for j in range(ATTN_WIDTH // LANES):
        sl = slice(j * LANES, (j + 1) * LANES)
        qs[:, sl] = (_rope(q[:, sl], cosv, sinv) * Q_SCALE).astype(BF16)

    k = jnp.dot(xb, wk_ref[...], preferred_element_type=F32)
    v = jnp.dot(xb, wv_ref[...], preferred_element_type=F32)
    new_rows = slice(WIN_ROWS, WIN_ROWS + tm)
    for c in range(KV_WIDTH // LANES):
        sl = slice(c * LANES, (c + 1) * LANES)
        kc = _rope(k[:, sl], cosv, sinv)
        vc = v[:, sl]

        @pl.when(i == last)
        def _():
            knew_ref[sl, :] = kc[tm - WIN_ROWS:, :].T
            vnew_ref[sl, :] = vc[tm - WIN_ROWS:, :].T

        ev = slice(2 * c * LANES, (2 * c + 1) * LANES)
        od = slice((2 * c + 1) * LANES, (2 * c + 2) * LANES)
        klo[new_rows, ev], khi[new_rows, ev], klo[new_rows, od], khi[new_rows, od] = _lo_hi(kc)
        vlo[new_rows, ev], vhi[new_rows, ev], vlo[new_rows, od], vhi[new_rows, od] = _lo_hi(vc)

    key = lax.broadcasted_iota(jnp.int32, (PAIR_KEYS, 1), 0)
    qcol = lax.broadcasted_iota(jnp.int32, (1, 2 * PAIR_ROWS), 1)
    first_key = jnp.where(qcol % PAIR_ROWS >= CHUNK, CHUNK, 0)
    in_window = (key >= first_key) & (key < first_key + (PAIR_KEYS - CHUNK))
    window_bias = jnp.where(in_window, 0.0, -jnp.inf)
    dims_nt = (((1,), (1,)), ((), ()))
    dims_tn = (((0,), (0,)), ((), ()))

    def softmax_t(s, sk, bias):
        s = s + bias
        m = jnp.maximum(jnp.max(s, axis=0, keepdims=True), sk)
        p = jnp.exp2(s - m)
        denom = jnp.sum(p, axis=0, keepdims=True) + jnp.exp2(sk - m)
        return p.astype(BF16), 1.0 / denom

    def attend_pair(p):
        r0 = p * PAIR_ROWS
        first_key_chunk = i * (tm // CHUNK) + 2 * p - WIN_CHUNKS
        started = key >= jnp.maximum(0, -first_key_chunk) * CHUNK
        bias = window_bias + jnp.where(started, 0.0, -jnp.inf)
        keys = pl.ds(r0, PAIR_KEYS)
        scores = []
        for kv in range(N_KV_HEADS):
            hl = slice(kv * LANES, (kv + 1) * LANES)
            q2 = qs[pl.ds(r0, PAIR_ROWS), kv * 2 * LANES:(kv + 1) * 2 * LANES]
            qst = jnp.concatenate([q2[:, :LANES], q2[:, LANES:]], axis=0)
            scores.append((lax.dot_general(klo[keys, hl], qst, dims_nt, preferred_element_type=F32),
                           lax.dot_general(khi[keys, hl], qst, dims_nt, preferred_element_type=F32)))
        probs = []
        for kv, (s_a, s_b) in enumerate(scores):
            h = kv * GROUP
            first = qcol < PAIR_ROWS
            sk_a = jnp.where(first, sink_ref[h + 0] * LOG2E, sink_ref[h + 2] * LOG2E)
            sk_b = jnp.where(first, sink_ref[h + 1] * LOG2E, sink_ref[h + 3] * LOG2E)
            probs.append(softmax_t(s_a, sk_a, bias) + softmax_t(s_b, sk_b, bias))
        for kv, (p_a, inv_a, p_b, inv_b) in enumerate(probs):
            hl = slice(kv * LANES, (kv + 1) * LANES)
            o_a = lax.dot_general(vlo[keys, hl], p_a, dims_tn, preferred_element_type=F32)
            o_b = lax.dot_general(vhi[keys, hl], p_b, dims_tn, preferred_element_type=F32)
            o = (o_a * inv_a + o_b * inv_b).T
            os_[pl.ds(r0, PAIR_ROWS), kv * 2 * LANES:kv * 2 * LANES + LANES] = o[:PAIR_ROWS]
            os_[pl.ds(r0, PAIR_ROWS), kv * 2 * LANES + LANES:(kv + 1) * 2 * LANES] = o[PAIR_ROWS:]

    for p in range(tm // PAIR_ROWS):
        attend_pair(p)

    for ref in (klo, khi, vlo, vhi):
        ref[0:WIN_ROWS, :] = ref[tm:tm + WIN_ROWS, :]

    g = jnp.dot(xb, wg_ref[...], preferred_element_type=F32)
    z = (os_[...] * _silu(g)).astype(BF16)
    y = jnp.dot(z, wo_ref[...], preferred_element_type=F32)
    y_ref[...] = _layer_norm(DEEPNORM_ALPHA * x + y, lng_ref[...], lnb_ref[...])


def _attn_sample_proj_kernel(xk_ref, w_ref, wo_ref, h_ref, w_bf_ref, wo_bf_ref):
    kstep = pl.program_id(0)
    w_bf = w_ref[...].astype(BF16)
    w_bf_ref[...] = w_bf
    wo_bf_ref[...] = wo_ref[...].astype(BF16)
    part = jnp.dot(xk_ref[...].astype(BF16), w_bf, preferred_element_type=F32)

    @pl.when(kstep == 0)
    def _():
        h_ref[...] = part

    @pl.when(kstep > 0)
    def _():
        h_ref[...] += part


def _attn_sample_kernel(x_ref, ck_ref, cv_ref, h_ref, wo_ref, inv_ref, sgn_ref, sink_ref, lng_ref, lnb_ref,
                        y_ref, knew_ref, vnew_ref, qs, ks, vs, os_, *, n_batch, n_new, n_cache):
    rows = n_batch * n_new
    hq_ref, hk_ref, hv_ref, hg_ref = _split_attn_w_in(h_ref)
    x = x_ref[...]
    pos = (PAST_LEN + lax.broadcasted_iota(jnp.int32, (rows, 1), 0) % n_new).astype(F32)
    cosv, sinv = _rope_tables(pos, inv_ref, sgn_ref)

    for j in range(ATTN_WIDTH // LANES):
        sl = slice(j * LANES, (j + 1) * LANES)
        qs[:, sl] = (_rope(hq_ref[:, sl], cosv, sinv) * Q_SCALE).astype(BF16)
    k = hk_ref[...]
    v = hv_ref[...]
    k_t = jnp.concatenate([_rope(k[:, c * LANES:(c + 1) * LANES], cosv, sinv)
                           for c in range(KV_WIDTH // LANES)], axis=1).T
    v_t = v.T
    for j in range(rows // LANES):
        ks[j] = k_t[:, j * LANES:(j + 1) * LANES]
        vs[j] = v_t[:, j * LANES:(j + 1) * LANES]

    per_tile = LANES // n_new
    lane = lax.broadcasted_iota(jnp.int32, (1, LANES), 1)
    zeros = jnp.zeros((HEAD_DIM, 2 * LANES), BF16)
    dims_nt = (((1,), (1,)), ((), ()))

    def batch_body(b, carry):
        r0 = pl.multiple_of(b * n_new, n_new)
        off = (b % per_tile) * n_new
        k_old, v_old = ck_ref[b], cv_ref[b]
        k_new, v_new = ks[b // per_tile], vs[b // per_tile]
        mine = (lane >= off) & (lane < off + n_new)

        to_end = n_cache - n_new - off
        to_end = jnp.where(to_end < 0, to_end + LANES, to_end)
        for old, new, out_ref in ((k_old, k_new, knew_ref), (v_old, v_new, vnew_ref)):
            out_ref[b] = jnp.where(lane < n_cache - n_new,
                                   pltpu.roll(old, n_cache - n_new, 1), pltpu.roll(new, to_end, 1))

        bias = jnp.concatenate([jnp.zeros((1, n_cache), F32), jnp.where(mine, 0.0, -jnp.inf)], axis=1)
        k_all = jnp.concatenate([k_old, k_new], axis=1).astype(BF16)
        v_all = jnp.concatenate([v_old, jnp.where(mine, v_new, 0.0)], axis=1).astype(BF16)

        scores = []
        for kv in range(N_KV_HEADS):
            q2 = qs[pl.ds(r0, n_new), kv * 2 * LANES:(kv + 1) * 2 * LANES]
            qst = jnp.concatenate([q2[:, :LANES], q2[:, LANES:]], axis=0)
            k_kv = k_all[kv * HEAD_DIM:(kv + 1) * HEAD_DIM, :]
            scores.append((jnp.dot(qst, jnp.concatenate([k_kv, zeros], axis=0), preferred_element_type=F32),
                           jnp.dot(qst, jnp.concatenate([zeros, k_kv], axis=0), preferred_element_type=F32)))
        probs = []
        for kv, (s_a, s_b) in enumerate(scores):
            sink_a, sink_b = _sink_cols(sink_ref, kv, n_new)
            probs.append(_sink_softmax(s_a, sink_a, bias) + _sink_softmax(s_b, sink_b, bias))
        for kv, (p_a, inv_a, p_b, inv_b) in enumerate(probs):
            v_kv = v_all[kv * HEAD_DIM:(kv + 1) * HEAD_DIM, :]
            o_a = lax.dot_general(p_a, jnp.concatenate([v_kv, zeros], axis=0), dims_nt,
                                  preferred_element_type=F32)
            o_b = lax.dot_general(p_b, jnp.concatenate([zeros, v_kv], axis=0), dims_nt,
                                  preferred_element_type=F32)
            o = o_a * inv_a + o_b * inv_b
            os_[pl.ds(r0, n_new), kv * 2 * LANES:kv * 2 * LANES + LANES] = o[:n_new]
            os_[pl.ds(r0, n_new), kv * 2 * LANES + LANES:(kv + 1) * 2 * LANES] = o[n_new:]
        return carry

    lax.fori_loop(0, n_batch, batch_body, 0)

    z = (os_[...] * _silu(hg_ref[...])).astype(BF16)
    y = jnp.dot(z, wo_ref[...], preferred_element_type=F32)
    y_ref[...] = _layer_norm(DEEPNORM_ALPHA * x + y, lng_ref[...], lnb_ref[...])


def _depthwise(u_scr, c_scr, wdw_ref, bdw_ref, base, out_base, n_rows):
    off = CONV_HIST - (CONV_WIDTH - 1)
    for l in range(D_MODEL // LANES):
        sl = slice(l * LANES, (l + 1) * LANES)
        acc = jnp.zeros((n_rows, LANES), F32)
        for j in range(CONV_WIDTH):
            acc = acc + wdw_ref[j:j + 1, sl] * u_scr[l, pl.ds(base + j + off, n_rows, stride=1), :]
        c_scr[pl.ds(out_base, n_rows), sl] = acc + bdw_ref[:, sl]


def _conv_gate(c, g, cg_ref, cb_ref):
    cn = _layer_norm(c, cg_ref[...], cb_ref[...])
    return (_silu(cn) * _silu(g)).astype(BF16)


def _split_conv_w_in(w_ref):
    return tuple(w_ref.at[:, n * D_MODEL:(n + 1) * D_MODEL] for n in range(3))


def _conv_prompt_kernel(x_ref, w_ref, wo_ref, wdw_ref, bdw_ref, cg_ref, cb_ref,
                        lng_ref, lnb_ref,
                        y_ref, state_ref,
                        u_scr, c_scr):
    i = pl.program_id(0)
    last = pl.num_programs(0) - 1
    tm = x_ref.shape[0]
    wa_ref, wb_ref, wg_ref = _split_conv_w_in(w_ref)
    n_tiles = D_MODEL // LANES

    @pl.when(i == 0)
    def _():
        u_scr[:, 0:CONV_HIST, :] = jnp.zeros((n_tiles, CONV_HIST, LANES), F32)

    x = x_ref[...]
    xb = x.astype(BF16)
    a = jnp.dot(xb, wa_ref[...], preferred_element_type=F32)
    b = jnp.dot(xb, wb_ref[...], preferred_element_type=F32)
    u = a * jax.nn.sigmoid(b)
    for l in range(n_tiles):
        u_scr[l, CONV_HIST:CONV_HIST + tm, :] = u[:, l * LANES:(l + 1) * LANES]

    def row_body(rt, carry):
        base = pl.multiple_of(rt * CONV_ROW_TILE, CONV_ROW_TILE)
        _depthwise(u_scr, c_scr, wdw_ref, bdw_ref, base, base, CONV_ROW_TILE)
        return carry

    lax.fori_loop(0, tm // CONV_ROW_TILE, row_body, 0)

    @pl.when(i == last)
    def _():
        for l in range(n_tiles):
            state_ref[:, l * LANES:(l + 1) * LANES] = (
                u_scr[l, CONV_HIST + tm - (CONV_WIDTH - 1):CONV_HIST + tm, :])

    u_scr[:, 0:CONV_HIST, :] = u_scr[:, tm:tm + CONV_HIST, :]

    g = jnp.dot(xb, wg_ref[...], preferred_element_type=F32)
    z = _conv_gate(c_scr[...], g, cg_ref, cb_ref)
    y = jnp.dot(z, wo_ref[...], preferred_element_type=F32)
    y_ref[...] = _layer_norm(DEEPNORM_ALPHA * x + y, lng_ref[...], lnb_ref[...])


def _conv_sample_kernel(x_ref, st_ref, w_ref, wo_ref, wdw_ref, bdw_ref, cg_ref, cb_ref,
                        lng_ref, lnb_ref,
                        y_ref, state_ref,
                        u_scr, c_slab, c_scr, *, n_batch, n_new):
    hist = CONV_WIDTH - 1
    pad = CONV_HIST - hist
    rows = n_batch * n_new
    wa_ref, wb_ref, wg_ref = _split_conv_w_in(w_ref)
    x = x_ref[...]
    xb = x.astype(BF16)
    a = jnp.dot(xb, wa_ref[...], preferred_element_type=F32)
    b = jnp.dot(xb, wb_ref[...], preferred_element_type=F32)
    u = a * jax.nn.sigmoid(b)
    first_new = (pad + hist) * n_batch
    for l in range(D_MODEL // LANES):
        sl = slice(l * LANES, (l + 1) * LANES)
        u_scr[l, 0:pad * n_batch, :] = jnp.zeros((pad * n_batch, LANES), F32)
        u_scr[l, pad * n_batch:first_new, :] = st_ref[:, sl]
        for bi in range(n_batch):
            u_scr[l, pl.ds(first_new + bi, n_new, stride=n_batch), :] = u[bi * n_new:(bi + 1) * n_new, sl]
        state_ref[:, sl] = u_scr[l, (pad + n_new) * n_batch:(pad + n_new + hist) * n_batch, :]

    def row_body(rt, carry):
        base = pl.multiple_of(rt * CONV_ROW_TILE, CONV_ROW_TILE)
        for l in range(D_MODEL // LANES):
            sl = slice(l * LANES, (l + 1) * LANES)
            acc = jnp.zeros((CONV_ROW_TILE, LANES), F32)
            for j in range(CONV_WIDTH):
                acc = acc + wdw_ref[j:j + 1, sl] * u_scr[l, pl.ds(base + (j + pad) * n_batch, CONV_ROW_TILE), :]
            c_slab[l, pl.ds(base, CONV_ROW_TILE), :] = acc + bdw_ref[:, sl]
        return carry

    lax.fori_loop(0, rows // CONV_ROW_TILE, row_body, 0)

    for l in range(D_MODEL // LANES):
        for bi in range(n_batch):
            c_scr[bi * n_new:(bi + 1) * n_new, l * LANES:(l + 1) * LANES] = (
                c_slab[l, pl.ds(bi, n_new, stride=n_batch), :])

    g = jnp.dot(xb, wg_ref[...], preferred_element_type=F32)
    z = _conv_gate(c_scr[...], g, cg_ref, cb_ref)
    y = jnp.dot(z, wo_ref[...], preferred_element_type=F32)
    y_ref[...] = _layer_norm(DEEPNORM_ALPHA * x + y, lng_ref[...], lnb_ref[...])


def _const_spec(shape):
    return pl.BlockSpec(shape, lambda i: (0,) * len(shape), pipeline_mode=pl.Buffered(1))


def _const_out_spec(shape):
    return pl.BlockSpec(shape, lambda i: (0,) * len(shape), pipeline_mode=pl.Buffered(1))


def _vmem_spec():
    return pl.BlockSpec(memory_space=pltpu.VMEM)


def _smem_spec():
    return pl.BlockSpec(memory_space=pltpu.SMEM)


def _rope_lane_consts():
    half = ROT_DIM // 2
    inv = ROPE_THETA ** (-jnp.arange(half, dtype=F32) * (2.0 / ROT_DIM))
    zeros = jnp.zeros((HEAD_DIM - ROT_DIM,), F32)
    inv_head = jnp.concatenate([inv, inv, zeros])
    sgn_head = jnp.concatenate([-jnp.ones((half,), F32), jnp.ones((half,), F32), zeros])
    reps = LANES // HEAD_DIM
    return jnp.tile(inv_head, reps)[None, :], jnp.tile(sgn_head, reps)[None, :]


def _attn_layer(xp, xs, ck, cv, w_in, sink, w_out, ln_g, ln_b, next_w_in, next_w_out):
    seq = xp.shape[0]
    n_batch, n_cache = ck.shape[0], ck.shape[2]
    n_new = xs.shape[0] // n_batch
    assert n_cache == LANES and LANES % n_new == 0 and xs.shape[0] % LANES == 0
    inv, sgn = _rope_lane_consts()
    lng, lnb = ln_g[None, :], ln_b[None, :]
    params = pltpu.CompilerParams(dimension_semantics=("arbitrary",), vmem_limit_bytes=VMEM_LIMIT_BYTES)

    rows = xs.shape[0]
    k_rows = 2 * LANES
    h, wi, wo = pl.pallas_call(
        _attn_sample_proj_kernel,
        grid=(D_MODEL // k_rows,),
        in_specs=[pl.BlockSpec((rows, k_rows), lambda k: (0, k)),
                  pl.BlockSpec((k_rows, w_in.shape[1]), lambda k: (k, 0)),
                  pl.BlockSpec((k_rows, w_out.shape[1]), lambda k: (k, 0))],
        out_specs=[_const_out_spec((rows, w_in.shape[1])),
                   pl.BlockSpec((k_rows, w_in.shape[1]), lambda k: (k, 0)),
                   pl.BlockSpec((k_rows, w_out.shape[1]), lambda k: (k, 0))],
        out_shape=[jax.ShapeDtypeStruct((rows, w_in.shape[1]), F32),
                   jax.ShapeDtypeStruct(w_in.shape, BF16),
                   jax.ShapeDtypeStruct(w_out.shape, BF16)],
        compiler_params=params,
        name="attn_sample_proj",
    )(xs, w_in, w_out)

    new_t = pltpu.VMEM((rows // LANES, KV_WIDTH, LANES), F32)
    ys, ksn, vsn = pl.pallas_call(
        functools.partial(_attn_sample_kernel, n_batch=n_batch, n_new=n_new, n_cache=n_cache),
        in_specs=[_vmem_spec()] * 7 + [_smem_spec(), _vmem_spec(), _vmem_spec()],
        out_specs=[_vmem_spec()] * 3,
        out_shape=[jax.ShapeDtypeStruct((rows, D_MODEL), F32),
                   jax.ShapeDtypeStruct(ck.shape, F32),
                   jax.ShapeDtypeStruct(cv.shape, F32)],
        scratch_shapes=[pltpu.VMEM((rows, ATTN_WIDTH), BF16), new_t, new_t,
                        pltpu.VMEM((rows, ATTN_WIDTH), F32)],
        compiler_params=pltpu.CompilerParams(vmem_limit_bytes=VMEM_LIMIT_BYTES),
        name="attn_sample",
    )(xs, ck, cv, h, wo, inv, sgn, sink, lng, lnb)

    tm = ROW_BLOCK
    n_steps = seq // tm
    w_rows = next_w_in.shape[0] // n_steps
    assert next_w_in.shape[0] == next_w_out.shape[0] == w_rows * n_steps and w_rows % (2 * SUBLANES) == 0
    kv_scr = pltpu.VMEM((WIN_ROWS + tm, N_KV_HEADS * LANES), BF16)
    yp, kp, vp, next_wi, next_wo = pl.pallas_call(
        _attn_prompt_kernel,
        grid=(n_steps,),
        in_specs=[pl.BlockSpec((tm, D_MODEL), lambda i: (i, 0)),
                  _const_spec(wi.shape), _const_spec(wo.shape),
                  _const_spec(inv.shape), _const_spec(sgn.shape), _smem_spec(),
                  _const_spec(lng.shape), _const_spec(lnb.shape),
                  pl.BlockSpec((w_rows, next_w_in.shape[1]), lambda i: (i, 0)),
                  pl.BlockSpec((w_rows, next_w_out.shape[1]), lambda i: (i, 0))],
        out_specs=[pl.BlockSpec((tm, D_MODEL), lambda i: (i, 0)),
                   pl.BlockSpec((KV_WIDTH, WIN_ROWS), lambda i: (0, 0)),
                   pl.BlockSpec((KV_WIDTH, WIN_ROWS), lambda i: (0, 0)),
                   pl.BlockSpec((w_rows, next_w_in.shape[1]), lambda i: (i, 0)),
                   pl.BlockSpec((w_rows, next_w_out.shape[1]), lambda i: (i, 0))],
        out_shape=[jax.ShapeDtypeStruct((seq, D_MODEL), F32),
                   jax.ShapeDtypeStruct((KV_WIDTH, WIN_ROWS), F32),
                   jax.ShapeDtypeStruct((KV_WIDTH, WIN_ROWS), F32),
                   jax.ShapeDtypeStruct(next_w_in.shape, BF16),
                   jax.ShapeDtypeStruct(next_w_out.shape, BF16)],
        scratch_shapes=[pltpu.VMEM((tm, ATTN_WIDTH), BF16), kv_scr, kv_scr, kv_scr, kv_scr,
                        pltpu.VMEM((tm, ATTN_WIDTH), F32), pltpu.VMEM((4, tm, LANES), F32)],
        compiler_params=params,
        name="attn_prompt",
    )(xp, wi, wo, inv, sgn, sink, lng, lnb, next_w_in, next_w_out)

    return yp, ys, kp, vp, ksn, vsn, next_wi, next_wo


def _conv_layer(xp, xs, state, n_batch, wi, w_dw, b_dw, cg, cb, wo, ln_g, ln_b):
    seq = xp.shape[0]
    n_new = xs.shape[0] // n_batch
    hist = CONV_WIDTH - 1
    assert state.shape[0] == hist * n_batch and xs.shape[0] % CONV_ROW_TILE == 0
    vecs = [b_dw[None, :], cg[None, :], cb[None, :], ln_g[None, :], ln_b[None, :]]

    tm = ROW_BLOCK
    yp, cp = pl.pallas_call(
        _conv_prompt_kernel,
        grid=(seq // tm,),
        in_specs=[pl.BlockSpec((tm, D_MODEL), lambda i: (i, 0)),
                  _const_spec(wi.shape), _const_spec(wo.shape), _const_spec(w_dw.shape)]
                 + [_const_spec(t.shape) for t in vecs],
        out_specs=[pl.BlockSpec((tm, D_MODEL), lambda i: (i, 0)),
                   pl.BlockSpec((hist, D_MODEL), lambda i: (0, 0))],
        out_shape=[jax.ShapeDtypeStruct((seq, D_MODEL), F32),
                   jax.ShapeDtypeStruct((hist, D_MODEL), F32)],
        scratch_shapes=[pltpu.VMEM((D_MODEL // LANES, CONV_HIST + tm, LANES), F32),
                        pltpu.VMEM((tm, D_MODEL), F32)],
        compiler_params=pltpu.CompilerParams(dimension_semantics=("arbitrary",),
                                             vmem_limit_bytes=VMEM_LIMIT_BYTES),
        name="conv_prompt",
    )(xp, wi, wo, w_dw, *vecs)

    rows = xs.shape[0]
    ys, cs = pl.pallas_call(
        functools.partial(_conv_sample_kernel, n_batch=n_batch, n_new=n_new),
        in_specs=[_vmem_spec()] * 10,
        out_specs=[_vmem_spec()] * 2,
        out_shape=[jax.ShapeDtypeStruct((rows, D_MODEL), F32),
                   jax.ShapeDtypeStruct(state.shape, F32)],
        scratch_shapes=[pltpu.VMEM((D_MODEL // LANES, n_batch * (CONV_HIST + n_new), LANES), F32),
                        pltpu.VMEM((D_MODEL // LANES, rows, LANES), F32),
                        pltpu.VMEM((rows, D_MODEL), F32)],
        compiler_params=pltpu.CompilerParams(vmem_limit_bytes=VMEM_LIMIT_BYTES),
        name="conv_sample",
    )(xs, state, wi, wo, w_dw, *vecs)
    return yp, ys, cp, cs


def kernel(x_prompt, x_sample, cache_k, cache_v, state_conv, attn_w_in, attn_sink, attn_w_out,
           conv_w_in, conv_w_dw, conv_b_dw, conv_ln_g, conv_ln_b, conv_w_out, post_ln_g, post_ln_b):
    batch, seq, _ = x_prompt.shape
    dec_batch, dec_seq, _ = x_sample.shape
    wc = cache_k.shape[2]
    assert batch == 1 and wc == WIN_ROWS and seq % ROW_BLOCK == 0
    xp = x_prompt.reshape(seq, D_MODEL)
    xs = x_sample.reshape(dec_batch * dec_seq, D_MODEL)

    def feature_major(c):
        return jnp.transpose(c, (0, 2, 3, 1)).reshape(c.shape[0], KV_WIDTH, wc)

    def frame_major(c_t, n):
        return jnp.transpose(c_t.reshape(n, N_KV_HEADS, HEAD_DIM, wc), (0, 3, 1, 2))[None]

    xp, xs, kp, vp, ksn, vsn, conv_wi, conv_wo = _attn_layer(
        xp, xs, feature_major(cache_k[0]), feature_major(cache_v[0]),
        attn_w_in[0], attn_sink[0], attn_w_out[0], post_ln_g[0], post_ln_b[0],
        conv_w_in[0], conv_w_out[0])
    hist = state_conv.shape[2]
    state = jnp.transpose(state_conv[0], (1, 0, 2)).reshape(hist * dec_batch, D_MODEL)
    xp, xs, cp, cs = _conv_layer(
        xp, xs, state, dec_batch, conv_wi, conv_w_dw[0], conv_b_dw[0], conv_ln_g[0], conv_ln_b[0],
        conv_wo, post_ln_g[1], post_ln_b[1])
    cs = jnp.transpose(cs.reshape(hist, dec_batch, D_MODEL), (1, 0, 2))

    return (xp.reshape(batch, seq, D_MODEL), xs.reshape(dec_batch, dec_seq, D_MODEL),
            frame_major(kp, 1), frame_major(vp, 1),
            frame_major(ksn, dec_batch), frame_major(vsn, dec_batch),
            cp[None, None], cs[None])
```

```python
import functools

import jax
import jax.numpy as jnp
from jax import lax
from jax.experimental import pallas as pl
from jax.experimental.pallas import tpu as pltpu

F32 = jnp.float32
BF16 = jnp.bfloat16

D_MODEL = 2048
DEPTH = 2
PAST_LEN = 1024
CHUNK = 64
HEAD_DIM = 64
N_HEADS = D_MODEL // HEAD_DIM
N_KV_HEADS = 8
GROUP = N_HEADS // N_KV_HEADS
ATTN_WIDTH = N_HEADS * HEAD_DIM
KV_WIDTH = N_KV_HEADS * HEAD_DIM
WIN_CHUNKS = 2
WIN_ROWS = WIN_CHUNKS * CHUNK
ROT_DIM = HEAD_DIM // 4
ROPE_THETA = 500000.0
CONV_WIDTH = 31
LN_EPS = 1e-5
DEEPNORM_ALPHA = (2.0 * DEPTH) ** 0.25
ATTN_SCALE = HEAD_DIM ** -0.5
LOG2E = 1.4426950408889634
Q_SCALE = ATTN_SCALE * LOG2E

LANES = 128
SUBLANES = 8
VMEM_LIMIT_BYTES = 60 * 1024 * 1024
ROW_BLOCK = 256
CONV_HIST = 32
CONV_ROW_TILE = 32
PAIR_ROWS = 2 * CHUNK
PAIR_KEYS = WIN_ROWS + PAIR_ROWS


def _layer_norm(r, g, b):
    mu = jnp.mean(r, axis=-1, keepdims=True)
    d = r - mu
    var = jnp.mean(d * d, axis=-1, keepdims=True)
    return d * lax.rsqrt(var + LN_EPS) * g + b


def _silu(t):
    return t * jax.nn.sigmoid(t)


def _rope_tables(pos, inv_ref, sgn_ref):
    ang = pos * inv_ref[...]
    sin = jnp.sin(ang)
    return jnp.cos(ang), sin if sgn_ref is None else sin * sgn_ref[...]


def _rope(t, cosv, sinv):
    lane = lax.broadcasted_iota(jnp.int32, (1, LANES), 1) % HEAD_DIM
    partner = jnp.where(lane < ROT_DIM // 2,
                        pltpu.roll(t, LANES - ROT_DIM // 2, 1),
                        pltpu.roll(t, ROT_DIM // 2, 1))
    return t * cosv + partner * sinv


def _lo_hi(col):
    low = lax.broadcasted_iota(jnp.int32, (1, LANES), 1) < HEAD_DIM
    swapped = pltpu.roll(col, HEAD_DIM, 1)
    zero = jnp.zeros_like(col)
    even_lo = jnp.where(low, col, zero).astype(BF16)
    even_hi = jnp.where(low, zero, swapped).astype(BF16)
    odd_lo = jnp.where(low, swapped, zero).astype(BF16)
    odd_hi = jnp.where(low, zero, col).astype(BF16)
    return even_lo, even_hi, odd_lo, odd_hi


def _sink_softmax(s, sk, bias):
    if bias is not None:
        s = s + bias
    m = jnp.maximum(jnp.max(s, axis=-1, keepdims=True), sk)
    p = jnp.exp2(s - m)
    denom = jnp.sum(p, axis=-1, keepdims=True) + jnp.exp2(sk - m)
    return p.astype(BF16), 1.0 / denom


def _sink_cols(sink_ref, kv, q_rows):
    row = lax.broadcasted_iota(jnp.int32, (2 * q_rows, 1), 0)
    first = row < q_rows
    h = kv * GROUP
    return (jnp.where(first, sink_ref[h + 0] * LOG2E, sink_ref[h + 2] * LOG2E),
            jnp.where(first, sink_ref[h + 1] * LOG2E, sink_ref[h + 3] * LOG2E))


def _split_attn_w_in(w_ref):
    k0, v0, g0 = ATTN_WIDTH, ATTN_WIDTH + KV_WIDTH, ATTN_WIDTH + 2 * KV_WIDTH
    return w_ref.at[:, 0:k0], w_ref.at[:, k0:v0], w_ref.at[:, v0:g0], w_ref.at[:, g0:g0 + ATTN_WIDTH]


def _attn_prompt_kernel(x_ref, w_ref, wo_ref, inv_ref, sgn_ref,
                        sink_ref, lng_ref, lnb_ref, next_wi_ref, next_wo_ref,
                        y_ref, knew_ref, vnew_ref, next_wi_bf_ref, next_wo_bf_ref,
                        qs, klo, khi, vlo, vhi, os_, rope_tab):
    i = pl.program_id(0)
    last = pl.num_programs(0) - 1
    tm = x_ref.shape[0]
    next_wi_bf_ref[...] = next_wi_ref[...].astype(BF16)
    next_wo_bf_ref[...] = next_wo_ref[...].astype(BF16)
    wq_ref, wk_ref, wv_ref, wg_ref = _split_attn_w_in(w_ref)

    @pl.when(i == 0)
    def _():
        for ref in (klo, khi, vlo, vhi):
            ref[0:WIN_ROWS, :] = jnp.zeros((WIN_ROWS, ref.shape[1]), BF16)
        off = lax.broadcasted_iota(jnp.int32, (tm, 1), 0).astype(F32)
        cos_r, sin_r = _rope_tables(off, inv_ref, None)
        rope_tab[0] = cos_r
        rope_tab[1] = sin_r
        rope_tab[2] = cos_r * sgn_ref[...]
        rope_tab[3] = sin_r * sgn_ref[...]

    x = x_ref[...]
    xb = x.astype(BF16)
    start = jnp.full((1, 1), i * tm, jnp.int32).astype(F32)
    cos_s, sin_s = _rope_tables(start, inv_ref, None)
    cosv = cos_s * rope_tab[0] - sin_s * rope_tab[1]
    sinv = sin_s * rope_tab[2] + cos_s * rope_tab[3]

    q = jnp.dot(xb, wq_ref[...], preferred_element_type=F32)
    for j in range(ATTN_WIDTH // LANES):
        sl = slice(j * LANES, (j + 1) * LANES)
        qs[:, sl] = (_rope(q[:, sl], cosv, sinv) * Q_SCALE).astype(BF16)

    k = jnp.dot(xb, wk_ref[...], preferred_element_type=F32)
    v = jnp.dot(xb, wv_ref[...], preferred_element_type=F32)
    new_rows = slice(WIN_ROWS, WIN_ROWS + tm)
    for c in range(KV_WIDTH // LANES):
        sl = slice(c * LANES, (c + 1) * LANES)
        kc = _rope(k[:, sl], cosv, sinv)
        vc = v[:, sl]

        @pl.when(i == last)
        def _():
            knew_ref[sl, :] = kc[tm - WIN_ROWS:, :].T
            vnew_ref[sl, :] = vc[tm - WIN_ROWS:, :].T

        ev = slice(2 * c * LANES, (2 * c + 1) * LANES)
        od = slice((2 * c + 1) * LANES, (2 * c + 2) * LANES)
        klo[new_rows, ev], khi[new_rows, ev], klo[new_rows, od], khi[new_rows, od] = _lo_hi(kc)
        vlo[new_rows, ev], vhi[new_rows, ev], vlo[new_rows, od], vhi[new_rows, od] = _lo_hi(vc)

    key = lax.broadcasted_iota(jnp.int32, (PAIR_KEYS, 1), 0)
    qcol = lax.broadcasted_iota(jnp.int32, (1, 2 * PAIR_ROWS), 1)
    first_key = jnp.where(qcol % PAIR_ROWS >= CHUNK, CHUNK, 0)
    in_window = (key >= first_key) & (key < first_key + (PAIR_KEYS - CHUNK))
    window_bias = jnp.where(in_window, 0.0, -jnp.inf)
    dims_nt = (((1,), (1,)), ((), ()))
    dims_tn = (((0,), (0,)), ((), ()))

    def softmax_t(s, sk, bias):
        s = s + bias
        m = jnp.maximum(jnp.max(s, axis=0, keepdims=True), sk)
        p = jnp.exp2(s - m)
        denom = jnp.sum(p, axis=0, keepdims=True) + jnp.exp2(sk - m)
        return p.astype(BF16), 1.0 / denom

    n_pairs = tm // PAIR_ROWS
    all_scores, all_probs = [], []
    for p in range(n_pairs):
        r0 = p * PAIR_ROWS
        keys = pl.ds(r0, PAIR_KEYS)
        scores = []
        for kv in range(N_KV_HEADS):
            hl = slice(kv * LANES, (kv + 1) * LANES)
            q2 = qs[pl.ds(r0, PAIR_ROWS), kv * 2 * LANES:(kv + 1) * 2 * LANES]
            qst = jnp.concatenate([q2[:, :LANES], q2[:, LANES:]], axis=0)
            scores.append((lax.dot_general(klo[keys, hl], qst, dims_nt, preferred_element_type=F32),
                           lax.dot_general(khi[keys, hl], qst, dims_nt, preferred_element_type=F32)))
        all_scores.append(scores)
    for p in range(n_pairs):
        first_key_chunk = i * (tm // CHUNK) + 2 * p - WIN_CHUNKS
        started = key >= jnp.maximum(0, -first_key_chunk) * CHUNK
        bias = window_bias + jnp.where(started, 0.0, -jnp.inf)
        probs = []
        for kv, (s_a, s_b) in enumerate(all_scores[p]):
            h = kv * GROUP
            first = qcol < PAIR_ROWS
            sk_a = jnp.where(first, sink_ref[h + 0] * LOG2E, sink_ref[h + 2] * LOG2E)
            sk_b = jnp.where(first, sink_ref[h + 1] * LOG2E, sink_ref[h + 3] * LOG2E)
            probs.append(softmax_t(s_a, sk_a, bias) + softmax_t(s_b, sk_b, bias))
        all_probs.append(probs)
    for p in range(n_pairs):
        r0 = p * PAIR_ROWS
        keys = pl.ds(r0, PAIR_KEYS)
        for kv, (p_a, inv_a, p_b, inv_b) in enumerate(all_probs[p]):
            hl = slice(kv * LANES, (kv + 1) * LANES)
            o_a = lax.dot_general(vlo[keys, hl], p_a, dims_tn, preferred_element_type=F32)
            o_b = lax.dot_general(vhi[keys, hl], p_b, dims_tn, preferred_element_type=F32)
            o = (o_a * inv_a + o_b * inv_b).T
            os_[pl.ds(r0, PAIR_ROWS), kv * 2 * LANES:kv * 2 * LANES + LANES] = o[:PAIR_ROWS]
            os_[pl.ds(r0, PAIR_ROWS), kv * 2 * LANES + LANES:(kv + 1) * 2 * LANES] = o[PAIR_ROWS:]

    for ref in (klo, khi, vlo, vhi):
        ref[0:WIN_ROWS, :] = ref[tm:tm + WIN_ROWS, :]

    g = jnp.dot(xb, wg_ref[...], preferred_element_type=F32)
    z = (os_[...] * _silu(g)).astype(BF16)
    y = jnp.dot(z, wo_ref[...], preferred_element_type=F32)
    y_ref[...] = _layer_norm(DEEPNORM_ALPHA * x + y, lng_ref[...], lnb_ref[...])


def _attn_sample_proj_kernel(xk_ref, w_ref, wo_ref, h_ref, w_bf_ref, wo_bf_ref):
    kstep = pl.program_id(0)
    w_bf = w_ref[...].astype(BF16)
    w_bf_ref[...] = w_bf
    wo_bf_ref[...] = wo_ref[...].astype(BF16)
    part = jnp.dot(xk_ref[...].astype(BF16), w_bf, preferred_element_type=F32)

    @pl.when(kstep == 0)
    def _():
        h_ref[...] = part

    @pl.when(kstep > 0)
    def _():
        h_ref[...] += part


def _attn_sample_kernel(x_ref, ck_ref, cv_ref, h_ref, wo_ref, inv_ref, sgn_ref, sink_ref, lng_ref, lnb_ref,
                        y_ref, knew_ref, vnew_ref, qs, ks, vs, os_, *, n_batch, n_new, n_cache):
    rows = n_batch * n_new
    hq_ref, hk_ref, hv_ref, hg_ref = _split_attn_w_in(h_ref)
    x = x_ref[...]
    pos = (PAST_LEN + lax.broadcasted_iota(jnp.int32, (rows, 1), 0) % n_new).astype(F32)
    cosv, sinv = _rope_tables(pos, inv_ref, sgn_ref)

    for j in range(ATTN_WIDTH // LANES):
        sl = slice(j * LANES, (j + 1) * LANES)
        qs[:, sl] = (_rope(hq_ref[:, sl], cosv, sinv) * Q_SCALE).astype(BF16)
    k = hk_ref[...]
    v = hv_ref[...]
    k_t = jnp.concatenate([_rope(k[:, c * LANES:(c + 1) * LANES], cosv, sinv)
                           for c in range(KV_WIDTH // LANES)], axis=1).T
    v_t = v.T
    for j in range(rows // LANES):
        ks[j] = k_t[:, j * LANES:(j + 1) * LANES]
        vs[j] = v_t[:, j * LANES:(j + 1) * LANES]

    per_tile = LANES // n_new
    lane = lax.broadcasted_iota(jnp.int32, (1, LANES), 1)
    zeros = jnp.zeros((HEAD_DIM, 2 * LANES), BF16)
    dims_nt = (((1,), (1,)), ((), ()))

    def batch_body(b, carry):
        r0 = pl.multiple_of(b * n_new, n_new)
        off = (b % per_tile) * n_new
        k_old, v_old = ck_ref[b], cv_ref[b]
        k_new, v_new = ks[b // per_tile], vs[b // per_tile]
        mine = (lane >= off) & (lane < off + n_new)

        to_end = n_cache - n_new - off
        to_end = jnp.where(to_end < 0, to_end + LANES, to_end)
        for old, new, out_ref in ((k_old, k_new, knew_ref), (v_old, v_new, vnew_ref)):
            out_ref[b] = jnp.where(lane < n_cache - n_new,
                                   pltpu.roll(old, n_cache - n_new, 1), pltpu.roll(new, to_end, 1))

        bias = jnp.concatenate([jnp.zeros((1, n_cache), F32), jnp.where(mine, 0.0, -jnp.inf)], axis=1)
        k_all = jnp.concatenate([k_old, k_new], axis=1).astype(BF16)
        v_all = jnp.concatenate([v_old, jnp.where(mine, v_new, 0.0)], axis=1).astype(BF16)

        scores = []
        for kv in range(N_KV_HEADS):
            q2 = qs[pl.ds(r0, n_new), kv * 2 * LANES:(kv + 1) * 2 * LANES]
            qst = jnp.concatenate([q2[:, :LANES], q2[:, LANES:]], axis=0)
            k_kv = k_all[kv * HEAD_DIM:(kv + 1) * HEAD_DIM, :]
            scores.append((jnp.dot(qst, jnp.concatenate([k_kv, zeros], axis=0), preferred_element_type=F32),
                           jnp.dot(qst, jnp.concatenate([zeros, k_kv], axis=0), preferred_element_type=F32)))
        probs = []
        for kv, (s_a, s_b) in enumerate(scores):
            sink_a, sink_b = _sink_cols(sink_ref, kv, n_new)
            probs.append(_sink_softmax(s_a, sink_a, bias) + _sink_softmax(s_b, sink_b, bias))
        for kv, (p_a, inv_a, p_b, inv_b) in enumerate(probs):
            v_kv = v_all[kv * HEAD_DIM:(kv + 1) * HEAD_DIM, :]
            o_a = lax.dot_general(p_a, jnp.concatenate([v_kv, zeros], axis=0), dims_nt,
                                  preferred_element_type=F32)
            o_b = lax.dot_general(p_b, jnp.concatenate([zeros, v_kv], axis=0), dims_nt,
                                  preferred_element_type=F32)
            o = o_a * inv_a + o_b * inv_b
            os_[pl.ds(r0, n_new), kv * 2 * LANES:kv * 2 * LANES + LANES] = o[:n_new]
            os_[pl.ds(r0, n_new), kv * 2 * LANES + LANES:(kv + 1) * 2 * LANES] = o[n_new:]
        return carry

    lax.fori_loop(0, n_batch, batch_body, 0, unroll=2)

    z = (os_[...] * _silu(hg_ref[...])).astype(BF16)
    y = jnp.dot(z, wo_ref[...], preferred_element_type=F32)
    y_ref[...] = _layer_norm(DEEPNORM_ALPHA * x + y, lng_ref[...], lnb_ref[...])


def _depthwise(u_scr, c_scr, wdw_ref, bdw_ref, base, out_base, n_rows):
    off = CONV_HIST - (CONV_WIDTH - 1)
    for l in range(D_MODEL // LANES):
        sl = slice(l * LANES, (l + 1) * LANES)
        acc = jnp.zeros((n_rows, LANES), F32)
        for j in range(CONV_WIDTH):
            acc = acc + wdw_ref[j:j + 1, sl] * u_scr[l, pl.ds(base + j + off, n_rows, stride=1), :]
        c_scr[pl.ds(out_base, n_rows), sl] = acc + bdw_ref[:, sl]


def _conv_gate(c, g, cg_ref, cb_ref):
    cn = _layer_norm(c, cg_ref[...], cb_ref[...])
    return (_silu(cn) * _silu(g)).astype(BF16)


def _split_conv_w_in(w_ref):
    return tuple(w_ref.at[:, n * D_MODEL:(n + 1) * D_MODEL] for n in range(3))


def _conv_prompt_kernel(x_ref, w_ref, wo_ref, wdw_ref, bdw_ref, cg_ref, cb_ref,
                        lng_ref, lnb_ref,
                        y_ref, state_ref,
                        u_scr, c_scr):
    i = pl.program_id(0)
    last = pl.num_programs(0) - 1
    tm = x_ref.shape[0]
    wa_ref, wb_ref, wg_ref = _split_conv_w_in(w_ref)
    n_tiles = D_MODEL // LANES

    @pl.when(i == 0)
    def _():
        u_scr[:, 0:CONV_HIST, :] = jnp.zeros((n_tiles, CONV_HIST, LANES), F32)

    x = x_ref[...]
    xb = x.astype(BF16)
    a = jnp.dot(xb, wa_ref[...], preferred_element_type=F32)
    b = jnp.dot(xb, wb_ref[...], preferred_element_type=F32)
    u = a * jax.nn.sigmoid(b)
    for l in range(n_tiles):
        u_scr[l, CONV_HIST:CONV_HIST + tm, :] = u[:, l * LANES:(l + 1) * LANES]

    def row_body(rt, carry):
        base = pl.multiple_of(rt * CONV_ROW_TILE, CONV_ROW_TILE)
        _depthwise(u_scr, c_scr, wdw_ref, bdw_ref, base, base, CONV_ROW_TILE)
        return carry

    lax.fori_loop(0, tm // CONV_ROW_TILE, row_body, 0)

    @pl.when(i == last)
    def _():
        for l in range(n_tiles):
            state_ref[:, l * LANES:(l + 1) * LANES] = (
                u_scr[l, CONV_HIST + tm - (CONV_WIDTH - 1):CONV_HIST + tm, :])

    u_scr[:, 0:CONV_HIST, :] = u_scr[:, tm:tm + CONV_HIST, :]

    g = jnp.dot(xb, wg_ref[...], preferred_element_type=F32)
    z = _conv_gate(c_scr[...], g, cg_ref, cb_ref)
    y = jnp.dot(z, wo_ref[...], preferred_element_type=F32)
    y_ref[...] = _layer_norm(DEEPNORM_ALPHA * x + y, lng_ref[...], lnb_ref[...])


def _conv_sample_kernel(x_ref, st_ref, w_ref, wo_ref, wdw_ref, bdw_ref, cg_ref, cb_ref,
                        lng_ref, lnb_ref,
                        y_ref, state_ref,
                        u_scr, c_slab, c_scr, *, n_batch, n_new):
    hist = CONV_WIDTH - 1
    pad = CONV_HIST - hist
    rows = n_batch * n_new
    wa_ref, wb_ref, wg_ref = _split_conv_w_in(w_ref)
    x = x_ref[...]
    xb = x.astype(BF16)
    a = jnp.dot(xb, wa_ref[...], preferred_element_type=F32)
    b = jnp.dot(xb, wb_ref[...], preferred_element_type=F32)
    u = a * jax.nn.sigmoid(b)
    first_new = (pad + hist) * n_batch
    for l in range(D_MODEL // LANES):
        sl = slice(l * LANES, (l + 1) * LANES)
        u_scr[l, 0:pad * n_batch, :] = jnp.zeros((pad * n_batch, LANES), F32)
        u_scr[l, pad * n_batch:first_new, :] = st_ref[:, sl]
        for bi in range(n_batch):
            u_scr[l, pl.ds(first_new + bi, n_new, stride=n_batch), :] = u[bi * n_new:(bi + 1) * n_new, sl]
        state_ref[:, sl] = u_scr[l, (pad + n_new) * n_batch:(pad + n_new + hist) * n_batch, :]

    def row_body(rt, carry):
        base = pl.multiple_of(rt * CONV_ROW_TILE, CONV_ROW_TILE)
        for l in range(D_MODEL // LANES):
            sl = slice(l * LANES, (l + 1) * LANES)
            acc = jnp.zeros((CONV_ROW_TILE, LANES), F32)
            for j in range(CONV_WIDTH):
                acc = acc + wdw_ref[j:j + 1, sl] * u_scr[l, pl.ds(base + (j + pad) * n_batch, CONV_ROW_TILE), :]
            c_slab[l, pl.ds(base, CONV_ROW_TILE), :] = acc + bdw_ref[:, sl]
        return carry

    lax.fori_loop(0, rows // CONV_ROW_TILE, row_body, 0)

    for l in range(D_MODEL // LANES):
        for bi in range(n_batch):
            c_scr[bi * n_new:(bi + 1) * n_new, l * LANES:(l + 1) * LANES] = (
                c_slab[l, pl.ds(bi, n_new, stride=n_batch), :])

    g = jnp.dot(xb, wg_ref[...], preferred_element_type=F32)
    z = _conv_gate(c_scr[...], g, cg_ref, cb_ref)
    y = jnp.dot(z, wo_ref[...], preferred_element_type=F32)
    y_ref[...] = _layer_norm(DEEPNORM_ALPHA * x + y, lng_ref[...], lnb_ref[...])


def _const_spec(shape):
    return pl.BlockSpec(shape, lambda i: (0,) * len(shape), pipeline_mode=pl.Buffered(1))


def _const_out_spec(shape):
    return pl.BlockSpec(shape, lambda i: (0,) * len(shape), pipeline_mode=pl.Buffered(1))


def _vmem_spec():
    return pl.BlockSpec(memory_space=pltpu.VMEM)


def _smem_spec():
    return pl.BlockSpec(memory_space=pltpu.SMEM)


def _rope_lane_consts():
    half = ROT_DIM // 2
    inv = ROPE_THETA ** (-jnp.arange(half, dtype=F32) * (2.0 / ROT_DIM))
    zeros = jnp.zeros((HEAD_DIM - ROT_DIM,), F32)
    inv_head = jnp.concatenate([inv, inv, zeros])
    sgn_head = jnp.concatenate([-jnp.ones((half,), F32), jnp.ones((half,), F32), zeros])
    reps = LANES // HEAD_DIM
    return jnp.tile(inv_head, reps)[None, :], jnp.tile(sgn_head, reps)[None, :]


def _attn_layer(xp, xs, ck, cv, w_in, sink, w_out, ln_g, ln_b, next_w_in, next_w_out):
    seq = xp.shape[0]
    n_batch, n_cache = ck.shape[0], ck.shape[2]
    n_new = xs.shape[0] // n_batch
    assert n_cache == LANES and LANES % n_new == 0 and xs.shape[0] % LANES == 0
    inv, sgn = _rope_lane_consts()
    lng, lnb = ln_g[None, :], ln_b[None, :]
    params = pltpu.CompilerParams(dimension_semantics=("arbitrary",), vmem_limit_bytes=VMEM_LIMIT_BYTES)

    rows = xs.shape[0]
    k_rows = 4 * LANES
    h, wi, wo = pl.pallas_call(
        _attn_sample_proj_kernel,
        grid=(D_MODEL // k_rows,),
        in_specs=[pl.BlockSpec((rows, k_rows), lambda k: (0, k)),
                  pl.BlockSpec((k_rows, w_in.shape[1]), lambda k: (k, 0)),
                  pl.BlockSpec((k_rows, w_out.shape[1]), lambda k: (k, 0))],
        out_specs=[_const_out_spec((rows, w_in.shape[1])),
                   pl.BlockSpec((k_rows, w_in.shape[1]), lambda k: (k, 0)),
                   pl.BlockSpec((k_rows, w_out.shape[1]), lambda k: (k, 0))],
        out_shape=[jax.ShapeDtypeStruct((rows, w_in.shape[1]), F32),
                   jax.ShapeDtypeStruct(w_in.shape, BF16),
                   jax.ShapeDtypeStruct(w_out.shape, BF16)],
        compiler_params=params,
        name="attn_sample_proj",
    )(xs, w_in, w_out)

    new_t = pltpu.VMEM((rows // LANES, KV_WIDTH, LANES), F32)
    ys, ksn, vsn = pl.pallas_call(
        functools.partial(_attn_sample_kernel, n_batch=n_batch, n_new=n_new, n_cache=n_cache),
        in_specs=[_vmem_spec()] * 7 + [_smem_spec(), _vmem_spec(), _vmem_spec()],
        out_specs=[_vmem_spec()] * 3,
        out_shape=[jax.ShapeDtypeStruct((rows, D_MODEL), F32),
                   jax.ShapeDtypeStruct(ck.shape, F32),
                   jax.ShapeDtypeStruct(cv.shape, F32)],
        scratch_shapes=[pltpu.VMEM((rows, ATTN_WIDTH), BF16), new_t, new_t,
                        pltpu.VMEM((rows, ATTN_WIDTH), F32)],
        compiler_params=pltpu.CompilerParams(vmem_limit_bytes=VMEM_LIMIT_BYTES),
        name="attn_sample",
    )(xs, ck, cv, h, wo, inv, sgn, sink, lng, lnb)

    tm = ROW_BLOCK
    n_steps = seq // tm
    w_rows = next_w_in.shape[0] // n_steps
    assert next_w_in.shape[0] == next_w_out.shape[0] == w_rows * n_steps and w_rows % (2 * SUBLANES) == 0
    kv_scr = pltpu.VMEM((WIN_ROWS + tm, N_KV_HEADS * LANES), BF16)
    yp, kp, vp, next_wi, next_wo = pl.pallas_call(
        _attn_prompt_kernel,
        grid=(n_steps,),
        in_specs=[pl.BlockSpec((tm, D_MODEL), lambda i: (i, 0)),
                  _const_spec(wi.shape), _const_spec(wo.shape),
                  _const_spec(inv.shape), _const_spec(sgn.shape), _smem_spec(),
                  _const_spec(lng.shape), _const_spec(lnb.shape),
                  pl.BlockSpec((w_rows, next_w_in.shape[1]), lambda i: (i, 0)),
                  pl.BlockSpec((w_rows, next_w_out.shape[1]), lambda i: (i, 0))],
        out_specs=[pl.BlockSpec((tm, D_MODEL), lambda i: (i, 0)),
                   pl.BlockSpec((KV_WIDTH, WIN_ROWS), lambda i: (0, 0)),
                   pl.BlockSpec((KV_WIDTH, WIN_ROWS), lambda i: (0, 0)),
                   pl.BlockSpec((w_rows, next_w_in.shape[1]), lambda i: (i, 0)),
                   pl.BlockSpec((w_rows, next_w_out.shape[1]), lambda i: (i, 0))],
        out_shape=[jax.ShapeDtypeStruct((seq, D_MODEL), F32),
                   jax.ShapeDtypeStruct((KV_WIDTH, WIN_ROWS), F32),
                   jax.ShapeDtypeStruct((KV_WIDTH, WIN_ROWS), F32),
                   jax.ShapeDtypeStruct(next_w_in.shape, BF16),
                   jax.ShapeDtypeStruct(next_w_out.shape, BF16)],
        scratch_shapes=[pltpu.VMEM((tm, ATTN_WIDTH), BF16), kv_scr, kv_scr, kv_scr, kv_scr,
                        pltpu.VMEM((tm, ATTN_WIDTH), F32), pltpu.VMEM((4, tm, LANES), F32)],
        compiler_params=params,
        name="attn_prompt",
    )(xp, wi, wo, inv, sgn, sink, lng, lnb, next_w_in, next_w_out)

    return yp, ys, kp, vp, ksn, vsn, next_wi, next_wo


def _conv_layer(xp, xs, state, n_batch, wi, w_dw, b_dw, cg, cb, wo, ln_g, ln_b):
    seq = xp.shape[0]
    n_new = xs.shape[0] // n_batch
    hist = CONV_WIDTH - 1
    assert state.shape[0] == hist * n_batch and xs.shape[0] % CONV_ROW_TILE == 0
    vecs = [b_dw[None, :], cg[None, :], cb[None, :], ln_g[None, :], ln_b[None, :]]

    tm = ROW_BLOCK
    yp, cp = pl.pallas_call(
        _conv_prompt_kernel,
        grid=(seq // tm,),
        in_specs=[pl.BlockSpec((tm, D_MODEL), lambda i: (i, 0)),
                  _const_spec(wi.shape), _const_spec(wo.shape), _const_spec(w_dw.shape)]
                 + [_const_spec(t.shape) for t in vecs],
        out_specs=[pl.BlockSpec((tm, D_MODEL), lambda i: (i, 0)),
                   pl.BlockSpec((hist, D_MODEL), lambda i: (0, 0))],
        out_shape=[jax.ShapeDtypeStruct((seq, D_MODEL), F32),
                   jax.ShapeDtypeStruct((hist, D_MODEL), F32)],
        scratch_shapes=[pltpu.VMEM((D_MODEL // LANES, CONV_HIST + tm, LANES), F32),
                        pltpu.VMEM((tm, D_MODEL), F32)],
        compiler_params=pltpu.CompilerParams(dimension_semantics=("arbitrary",),
                                             vmem_limit_bytes=VMEM_LIMIT_BYTES),
        name="conv_prompt",
    )(xp, wi, wo, w_dw, *vecs)

    rows = xs.shape[0]
    ys, cs = pl.pallas_call(
        functools.partial(_conv_sample_kernel, n_batch=n_batch, n_new=n_new),
        in_specs=[_vmem_spec()] * 10,
        out_specs=[_vmem_spec()] * 2,
        out_shape=[jax.ShapeDtypeStruct((rows, D_MODEL), F32),
                   jax.ShapeDtypeStruct(state.shape, F32)],
        scratch_shapes=[pltpu.VMEM((D_MODEL // LANES, n_batch * (CONV_HIST + n_new), LANES), F32),
                        pltpu.VMEM((D_MODEL // LANES, rows, LANES), F32),
                        pltpu.VMEM((rows, D_MODEL), F32)],
        compiler_params=pltpu.CompilerParams(vmem_limit_bytes=VMEM_LIMIT_BYTES),
        name="conv_sample",
    )(xs, state, wi, wo, w_dw, *vecs)
    return yp, ys, cp, cs


def kernel(x_prompt, x_sample, cache_k, cache_v, state_conv, attn_w_in, attn_sink, attn_w_out,
           conv_w_in, conv_w_dw, conv_b_dw, conv_ln_g, conv_ln_b, conv_w_out, post_ln_g, post_ln_b):
    batch, seq, _ = x_prompt.shape
    dec_batch, dec_seq, _ = x_sample.shape
    wc = cache_k.shape[2]
    assert batch == 1 and wc == WIN_ROWS and seq % ROW_BLOCK == 0
    xp = x_prompt.reshape(seq, D_MODEL)
    xs = x_sample.reshape(dec_batch * dec_seq, D_MODEL)

    def feature_major(c):
        return jnp.transpose(c, (0, 2, 3, 1)).reshape(c.shape[0], KV_WIDTH, wc)

    def frame_major(c_t, n):
        return jnp.transpose(c_t.reshape(n, N_KV_HEADS, HEAD_DIM, wc), (0, 3, 1, 2))[None]

    xp, xs, kp, vp, ksn, vsn, conv_wi, conv_wo = _attn_layer(
        xp, xs, feature_major(cache_k[0]), feature_major(cache_v[0]),
        attn_w_in[0], attn_sink[0], attn_w_out[0], post_ln_g[0], post_ln_b[0],
        conv_w_in[0], conv_w_out[0])
    hist = state_conv.shape[2]
    state = jnp.transpose(state_conv[0], (1, 0, 2)).reshape(hist * dec_batch, D_MODEL)
    xp, xs, cp, cs = _conv_layer(
        xp, xs, state, dec_batch, conv_wi, conv_w_dw[0], conv_b_dw[0], conv_ln_g[0], conv_ln_b[0],
        conv_wo, post_ln_g[1], post_ln_b[1])
    cs = jnp.transpose(cs.reshape(hist, dec_batch, D_MODEL), (1, 0, 2))

    return (xp.reshape(batch, seq, D_MODEL), xs.reshape(dec_batch, dec_seq, D_MODEL),
            frame_major(kp, 1), frame_major(vp, 1),
            frame_major(ksn, dec_batch), frame_major(vsn, dec_batch),
            cp[None, None], cs[None])
```

```python
import functools

import jax
import jax.numpy as jnp
from jax import lax
from jax.experimental import pallas as pl
from jax.experimental.pallas import tpu as pltpu

F32 = jnp.float32
BF16 = jnp.bfloat16

D_MODEL = 2048
DEPTH = 2
PAST_LEN = 1024
CHUNK = 64
HEAD_DIM = 64
N_HEADS = D_MODEL // HEAD_DIM
N_KV_HEADS = 8
GROUP = N_HEADS // N_KV_HEADS
ATTN_WIDTH = N_HEADS * HEAD_DIM
KV_WIDTH = N_KV_HEADS * HEAD_DIM
WIN_CHUNKS = 2
WIN_ROWS = WIN_CHUNKS * CHUNK
ROT_DIM = HEAD_DIM // 4
ROPE_THETA = 500000.0
CONV_WIDTH = 31
LN_EPS = 1e-5
DEEPNORM_ALPHA = (2.0 * DEPTH) ** 0.25
ATTN_SCALE = HEAD_DIM ** -0.5
LOG2E = 1.4426950408889634
Q_SCALE = ATTN_SCALE * LOG2E

LANES = 128
SUBLANES = 8
VMEM_LIMIT_BYTES = 60 * 1024 * 1024
ROW_BLOCK = 256
CONV_HIST = 32
CONV_ROW_TILE = 32
PAIR_ROWS = 2 * CHUNK
PAIR_KEYS = WIN_ROWS + PAIR_ROWS


def _layer_norm(r, g, b):
    mu = jnp.mean(r, axis=-1, keepdims=True)
    d = r - mu
    var = jnp.mean(d * d, axis=-1, keepdims=True)
    return d * lax.rsqrt(var + LN_EPS) * g + b


def _silu(t):
    return t * jax.nn.sigmoid(t)


def _rope_tables(pos, inv_ref, sgn_ref):
    ang = pos * inv_ref[...]
    sin = jnp.sin(ang)
    return jnp.cos(ang), sin if sgn_ref is None else sin * sgn_ref[...]


def _rope(t, cosv, sinv):
    lane = lax.broadcasted_iota(jnp.int32, (1, LANES), 1) % HEAD_DIM
    partner = jnp.where(lane < ROT_DIM // 2,
                        pltpu.roll(t, LANES - ROT_DIM // 2, 1),
                        pltpu.roll(t, ROT_DIM // 2, 1))
    return t * cosv + partner * sinv


def _lo_hi(col):
    low = lax.broadcasted_iota(jnp.int32, (1, LANES), 1) < HEAD_DIM
    swapped = pltpu.roll(col, HEAD_DIM, 1)
    zero = jnp.zeros_like(col)
    even_lo = jnp.where(low, col, zero).astype(BF16)
    even_hi = jnp.where(low, zero, swapped).astype(BF16)
    odd_lo = jnp.where(low, swapped, zero).astype(BF16)
    odd_hi = jnp.where(low, zero, col).astype(BF16)
    return even_lo, even_hi, odd_lo, odd_hi


def _sink_softmax(s, sk, bias):
    if bias is not None:
        s = s + bias
    m = jnp.maximum(jnp.max(s, axis=-1, keepdims=True), sk)
    p = jnp.exp2(s - m)
    denom = jnp.sum(p, axis=-1, keepdims=True) + jnp.exp2(sk - m)
    return p.astype(BF16), 1.0 / denom


def _sink_cols(sink_ref, kv, q_rows):
    row = lax.broadcasted_iota(jnp.int32, (2 * q_rows, 1), 0)
    first = row < q_rows
    h = kv * GROUP
    return (jnp.where(first, sink_ref[h + 0] * LOG2E, sink_ref[h + 2] * LOG2E),
            jnp.where(first, sink_ref[h + 1] * LOG2E, sink_ref[h + 3] * LOG2E))


def _split_attn_w_in(w_ref):
    k0, v0, g0 = ATTN_WIDTH, ATTN_WIDTH + KV_WIDTH, ATTN_WIDTH + 2 * KV_WIDTH
    return w_ref.at[:, 0:k0], w_ref.at[:, k0:v0], w_ref.at[:, v0:g0], w_ref.at[:, g0:g0 + ATTN_WIDTH]


def _attn_prompt_kernel(x_ref, w_ref, wo_ref, inv_ref, sgn_ref,
                        sink_ref, lng_ref, lnb_ref, next_wi_ref, next_wo_ref,
                        y_ref, knew_ref, vnew_ref, next_wi_bf_ref, next_wo_bf_ref,
                        qs, klo, khi, vwin, os_, rope_tab):
    i = pl.program_id(0)
    last = pl.num_programs(0) - 1
    tm = x_ref.shape[0]
    next_wi_bf_ref[...] = next_wi_ref[...].astype(BF16)
    next_wo_bf_ref[...] = next_wo_ref[...].astype(BF16)
    wq_ref, wk_ref, wv_ref, wg_ref = _split_attn_w_in(w_ref)

    @pl.when(i == 0)
    def _():
        for ref in (klo, khi, vwin):
            ref[0:WIN_ROWS, :] = jnp.zeros((WIN_ROWS, ref.shape[1]), BF16)
        off = lax.broadcasted_iota(jnp.int32, (tm, 1), 0).astype(F32)
        cos_r, sin_r = _rope_tables(off, inv_ref, None)
        rope_tab[0] = cos_r
        rope_tab[1] = sin_r
        rope_tab[2] = cos_r * sgn_ref[...]
        rope_tab[3] = sin_r * sgn_ref[...]

    x = x_ref[...]
    xb = x.astype(BF16)
    start = jnp.full((1, 1), i * tm, jnp.int32).astype(F32)
    cos_s, sin_s = _rope_tables(start, inv_ref, None)
    cosv = cos_s * rope_tab[0] - sin_s * rope_tab[1]
    sinv = sin_s * rope_tab[2] + cos_s * rope_tab[3]

    q = jnp.dot(xb, wq_ref[...], preferred_element_type=F32)
    for j in range(ATTN_WIDTH // LANES):
        sl = slice(j * LANES, (j + 1) * LANES)
        qs[:, sl] = (_rope(q[:, sl], cosv, sinv) * Q_SCALE).astype(BF16)

    k = jnp.dot(xb, wk_ref[...], preferred_element_type=F32)
    v = jnp.dot(xb, wv_ref[...], preferred_element_type=F32)
    new_rows = slice(WIN_ROWS, WIN_ROWS + tm)
    for c in range(KV_WIDTH // LANES):
        sl = slice(c * LANES, (c + 1) * LANES)
        kc = _rope(k[:, sl], cosv, sinv)
        vc = v[:, sl]

        @pl.when(i == last)
        def _():
            knew_ref[sl, :] = kc[tm - WIN_ROWS:, :].T
            vnew_ref[sl, :] = vc[tm - WIN_ROWS:, :].T

        ev = slice(2 * c * LANES, (2 * c + 1) * LANES)
        od = slice((2 * c + 1) * LANES, (2 * c + 2) * LANES)
        klo[new_rows, ev], khi[new_rows, ev], klo[new_rows, od], khi[new_rows, od] = _lo_hi(kc)
        vwin[new_rows, sl] = vc.astype(BF16)

    key = lax.broadcasted_iota(jnp.int32, (PAIR_KEYS, 1), 0)
    qcol = lax.broadcasted_iota(jnp.int32, (1, 2 * PAIR_ROWS), 1)
    first_key = jnp.where(qcol % PAIR_ROWS >= CHUNK, CHUNK, 0)
    in_window = (key >= first_key) & (key < first_key + (PAIR_KEYS - CHUNK))
    window_bias = jnp.where(in_window, 0.0, -jnp.inf)
    dims_nt = (((1,), (1,)), ((), ()))
    dims_tn = (((0,), (0,)), ((), ()))

    def softmax_t(s, sk, bias):
        s = s + bias
        m = jnp.maximum(jnp.max(s, axis=0, keepdims=True), sk)
        p = jnp.exp2(s - m)
        denom = jnp.sum(p, axis=0, keepdims=True) + jnp.exp2(sk - m)
        return p.astype(BF16), 1.0 / denom

    def attend_pair(p):
        r0 = p * PAIR_ROWS
        first_key_chunk = i * (tm // CHUNK) + 2 * p - WIN_CHUNKS
        started = key >= jnp.maximum(0, -first_key_chunk) * CHUNK
        bias = window_bias + jnp.where(started, 0.0, -jnp.inf)
        keys = pl.ds(r0, PAIR_KEYS)
        scores = []
        for kv in range(N_KV_HEADS):
            hl = slice(kv * LANES, (kv + 1) * LANES)
            q2 = qs[pl.ds(r0, PAIR_ROWS), kv * 2 * LANES:(kv + 1) * 2 * LANES]
            qst = jnp.concatenate([q2[:, :LANES], q2[:, LANES:]], axis=0)
            scores.append((lax.dot_general(klo[keys, hl], qst, dims_nt, preferred_element_type=F32),
                           lax.dot_general(khi[keys, hl], qst, dims_nt, preferred_element_type=F32)))
        probs = []
        for kv, (s_a, s_b) in enumerate(scores):
            h = kv * GROUP
            first = qcol < PAIR_ROWS
            sk_a = jnp.where(first, sink_ref[h + 0] * LOG2E, sink_ref[h + 2] * LOG2E)
            sk_b = jnp.where(first, sink_ref[h + 1] * LOG2E, sink_ref[h + 3] * LOG2E)
            probs.append(softmax_t(s_a, sk_a, bias) + softmax_t(s_b, sk_b, bias))
        for kv, (p_a, inv_a, p_b, inv_b) in enumerate(probs):
            vt = slice((kv // 2) * LANES, (kv // 2 + 1) * LANES)
            d0 = (kv % 2) * HEAD_DIM
            o_a = lax.dot_general(vwin[keys, vt], p_a, dims_tn, preferred_element_type=F32)
            o_b = lax.dot_general(vwin[keys, vt], p_b, dims_tn, preferred_element_type=F32)
            o = jnp.concatenate([o_a[d0:d0 + HEAD_DIM] * inv_a,
                                 o_b[d0:d0 + HEAD_DIM] * inv_b], axis=0).T
            os_[pl.ds(r0, PAIR_ROWS), kv * 2 * LANES:kv * 2 * LANES + LANES] = o[:PAIR_ROWS]
            os_[pl.ds(r0, PAIR_ROWS), kv * 2 * LANES + LANES:(kv + 1) * 2 * LANES] = o[PAIR_ROWS:]

    for p in range(tm // PAIR_ROWS):
        attend_pair(p)

    for ref in (klo, khi, vwin):
        ref[0:WIN_ROWS, :] = ref[tm:tm + WIN_ROWS, :]

    g = jnp.dot(xb, wg_ref[...], preferred_element_type=F32)
    z = (os_[...] * _silu(g)).astype(BF16)
    y = jnp.dot(z, wo_ref[...], preferred_element_type=F32)
    y_ref[...] = _layer_norm(DEEPNORM_ALPHA * x + y, lng_ref[...], lnb_ref[...])


def _attn_sample_proj_kernel(xk_ref, w_ref, wo_ref, h_ref, w_bf_ref, wo_bf_ref):
    kstep = pl.program_id(0)
    w_bf = w_ref[...].astype(BF16)
    w_bf_ref[...] = w_bf
    wo_bf_ref[...] = wo_ref[...].astype(BF16)
    part = jnp.dot(xk_ref[...].astype(BF16), w_bf, preferred_element_type=F32)

    @pl.when(kstep == 0)
    def _():
        h_ref[...] = part

    @pl.when(kstep > 0)
    def _():
        h_ref[...] += part


def _attn_sample_kernel(x_ref, ck_ref, cv_ref, h_ref, wo_ref, inv_ref, sgn_ref, sink_ref, lng_ref, lnb_ref,
                        y_ref, knew_ref, vnew_ref, qs, ks, vs, os_, *, n_batch, n_new, n_cache):
    rows = n_batch * n_new
    hq_ref, hk_ref, hv_ref, hg_ref = _split_attn_w_in(h_ref)
    x = x_ref[...]
    pos = (PAST_LEN + lax.broadcasted_iota(jnp.int32, (rows, 1), 0) % n_new).astype(F32)
    cosv, sinv = _rope_tables(pos, inv_ref, sgn_ref)

    for j in range(ATTN_WIDTH // LANES):
        sl = slice(j * LANES, (j + 1) * LANES)
        qs[:, sl] = (_rope(hq_ref[:, sl], cosv, sinv) * Q_SCALE).astype(BF16)
    k = hk_ref[...]
    v = hv_ref[...]
    k_t = jnp.concatenate([_rope(k[:, c * LANES:(c + 1) * LANES], cosv, sinv)
                           for c in range(KV_WIDTH // LANES)], axis=1).T
    v_t = v.T
    for j in range(rows // LANES):
        ks[j] = k_t[:, j * LANES:(j + 1) * LANES]
        vs[j] = v_t[:, j * LANES:(j + 1) * LANES]

    per_tile = LANES // n_new
    lane = lax.broadcasted_iota(jnp.int32, (1, LANES), 1)
    zeros = jnp.zeros((HEAD_DIM, 2 * LANES), BF16)
    dims_nt = (((1,), (1,)), ((), ()))

    def batch_body(b, carry):
        r0 = pl.multiple_of(b * n_new, n_new)
        off = (b % per_tile) * n_new
        k_old, v_old = ck_ref[b], cv_ref[b]
        k_new, v_new = ks[b // per_tile], vs[b // per_tile]
        mine = (lane >= off) & (lane < off + n_new)

        to_end = n_cache - n_new - off
        to_end = jnp.where(to_end < 0, to_end + LANES, to_end)
        for old, new, out_ref in ((k_old, k_new, knew_ref), (v_old, v_new, vnew_ref)):
            out_ref[b] = jnp.where(lane < n_cache - n_new,
                                   pltpu.roll(old, n_cache - n_new, 1), pltpu.roll(new, to_end, 1))

        bias = jnp.concatenate([jnp.zeros((1, n_cache), F32), jnp.where(mine, 0.0, -jnp.inf)], axis=1)
        k_all = jnp.concatenate([k_old, k_new], axis=1).astype(BF16)
        v_all = jnp.concatenate([v_old, jnp.where(mine, v_new, 0.0)], axis=1).astype(BF16)

        scores = []
        for kv in range(N_KV_HEADS):
            q2 = qs[pl.ds(r0, n_new), kv * 2 * LANES:(kv + 1) * 2 * LANES]
            qst = jnp.concatenate([q2[:, :LANES], q2[:, LANES:]], axis=0)
            k_kv = k_all[kv * HEAD_DIM:(kv + 1) * HEAD_DIM, :]
            scores.append((jnp.dot(qst, jnp.concatenate([k_kv, zeros], axis=0), preferred_element_type=F32),
                           jnp.dot(qst, jnp.concatenate([zeros, k_kv], axis=0), preferred_element_type=F32)))
        probs = []
        for kv, (s_a, s_b) in enumerate(scores):
            sink_a, sink_b = _sink_cols(sink_ref, kv, n_new)
            probs.append(_sink_softmax(s_a, sink_a, bias) + _sink_softmax(s_b, sink_b, bias))
        for kv, (p_a, inv_a, p_b, inv_b) in enumerate(probs):
            v_kv = v_all[kv * HEAD_DIM:(kv + 1) * HEAD_DIM, :]
            o_a = lax.dot_general(p_a, jnp.concatenate([v_kv, zeros], axis=0), dims_nt,
                                  preferred_element_type=F32)
            o_b = lax.dot_general(p_b, jnp.concatenate([zeros, v_kv], axis=0), dims_nt,
                                  preferred_element_type=F32)
            o = o_a * inv_a + o_b * inv_b
            os_[pl.ds(r0, n_new), kv * 2 * LANES:kv * 2 * LANES + LANES] = o[:n_new]
            os_[pl.ds(r0, n_new), kv * 2 * LANES + LANES:(kv + 1) * 2 * LANES] = o[n_new:]
        return carry

    lax.fori_loop(0, n_batch, batch_body, 0)

    z = (os_[...] * _silu(hg_ref[...])).astype(BF16)
    y = jnp.dot(z, wo_ref[...], preferred_element_type=F32)
    y_ref[...] = _layer_norm(DEEPNORM_ALPHA * x + y, lng_ref[...], lnb_ref[...])


def _depthwise(u_scr, c_scr, wdw_ref, bdw_ref, base, out_base, n_rows):
    off = CONV_HIST - (CONV_WIDTH - 1)
    for l in range(D_MODEL // LANES):
        sl = slice(l * LANES, (l + 1) * LANES)
        acc = jnp.zeros((n_rows, LANES), F32)
        for j in range(CONV_WIDTH):
            acc = acc + wdw_ref[j:j + 1, sl] * u_scr[l, pl.ds(base + j + off, n_rows, stride=1), :]
        c_scr[pl.ds(out_base, n_rows), sl] = acc + bdw_ref[:, sl]


def _conv_gate(c, g, cg_ref, cb_ref):
    cn = _layer_norm(c, cg_ref[...], cb_ref[...])
    return (_silu(cn) * _silu(g)).astype(BF16)


def _split_conv_w_in(w_ref):
    return tuple(w_ref.at[:, n * D_MODEL:(n + 1) * D_MODEL] for n in range(3))


def _conv_prompt_kernel(x_ref, w_ref, wo_ref, wdw_ref, bdw_ref, cg_ref, cb_ref,
                        lng_ref, lnb_ref,
                        y_ref, state_ref,
                        u_scr, c_scr):
    i = pl.program_id(0)
    last = pl.num_programs(0) - 1
    tm = x_ref.shape[0]
    wa_ref, wb_ref, wg_ref = _split_conv_w_in(w_ref)
    n_tiles = D_MODEL // LANES

    @pl.when(i == 0)
    def _():
        u_scr[:, 0:CONV_HIST, :] = jnp.zeros((n_tiles, CONV_HIST, LANES), F32)

    x = x_ref[...]
    xb = x.astype(BF16)
    a = jnp.dot(xb, wa_ref[...], preferred_element_type=F32)
    b = jnp.dot(xb, wb_ref[...], preferred_element_type=F32)
    u = a * jax.nn.sigmoid(b)
    for l in range(n_tiles):
        u_scr[l, CONV_HIST:CONV_HIST + tm, :] = u[:, l * LANES:(l + 1) * LANES]

    def row_body(rt, carry):
        base = pl.multiple_of(rt * CONV_ROW_TILE, CONV_ROW_TILE)
        _depthwise(u_scr, c_scr, wdw_ref, bdw_ref, base, base, CONV_ROW_TILE)
        return carry

    lax.fori_loop(0, tm // CONV_ROW_TILE, row_body, 0)

    @pl.when(i == last)
    def _():
        for l in range(n_tiles):
            state_ref[:, l * LANES:(l + 1) * LANES] = (
                u_scr[l, CONV_HIST + tm - (CONV_WIDTH - 1):CONV_HIST + tm, :])

    u_scr[:, 0:CONV_HIST, :] = u_scr[:, tm:tm + CONV_HIST, :]

    g = jnp.dot(xb, wg_ref[...], preferred_element_type=F32)
    z = _conv_gate(c_scr[...], g, cg_ref, cb_ref)
    y = jnp.dot(z, wo_ref[...], preferred_element_type=F32)
    y_ref[...] = _layer_norm(DEEPNORM_ALPHA * x + y, lng_ref[...], lnb_ref[...])


def _conv_sample_kernel(x_ref, st_ref, w_ref, wo_ref, wdw_ref, bdw_ref, cg_ref, cb_ref,
                        lng_ref, lnb_ref,
                        y_ref, state_ref,
                        u_scr, c_slab, c_scr, *, n_batch, n_new):
    hist = CONV_WIDTH - 1
    pad = CONV_HIST - hist
    rows = n_batch * n_new
    wa_ref, wb_ref, wg_ref = _split_conv_w_in(w_ref)
    x = x_ref[...]
    xb = x.astype(BF16)
    a = jnp.dot(xb, wa_ref[...], preferred_element_type=F32)
    b = jnp.dot(xb, wb_ref[...], preferred_element_type=F32)
    u = a * jax.nn.sigmoid(b)
    first_new = (pad + hist) * n_batch
    for l in range(D_MODEL // LANES):
        sl = slice(l * LANES, (l + 1) * LANES)
        u_scr[l, 0:pad * n_batch, :] = jnp.zeros((pad * n_batch, LANES), F32)
        u_scr[l, pad * n_batch:first_new, :] = st_ref[:, sl]
        for bi in range(n_batch):
            u_scr[l, pl.ds(first_new + bi, n_new, stride=n_batch), :] = u[bi * n_new:(bi + 1) * n_new, sl]
        state_ref[:, sl] = u_scr[l, (pad + n_new) * n_batch:(pad + n_new + hist) * n_batch, :]

    def row_body(rt, carry):
        base = pl.multiple_of(rt * CONV_ROW_TILE, CONV_ROW_TILE)
        for l in range(D_MODEL // LANES):
            sl = slice(l * LANES, (l + 1) * LANES)
            acc = jnp.zeros((CONV_ROW_TILE, LANES), F32)
            for j in range(CONV_WIDTH):
                acc = acc + wdw_ref[j:j + 1, sl] * u_scr[l, pl.ds(base + (j + pad) * n_batch, CONV_ROW_TILE), :]
            c_slab[l, pl.ds(base, CONV_ROW_TILE), :] = acc + bdw_ref[:, sl]
        return carry

    lax.fori_loop(0, rows // CONV_ROW_TILE, row_body, 0)

    for l in range(D_MODEL // LANES):
        for bi in range(n_batch):
            c_scr[bi * n_new:(bi + 1) * n_new, l * LANES:(l + 1) * LANES] = (
                c_slab[l, pl.ds(bi, n_new, stride=n_batch), :])

    g = jnp.dot(xb, wg_ref[...], preferred_element_type=F32)
    z = _conv_gate(c_scr[...], g, cg_ref, cb_ref)
    y = jnp.dot(z, wo_ref[...], preferred_element_type=F32)
    y_ref[...] = _layer_norm(DEEPNORM_ALPHA * x + y, lng_ref[...], lnb_ref[...])


def _const_spec(shape):
    return pl.BlockSpec(shape, lambda i: (0,) * len(shape), pipeline_mode=pl.Buffered(1))


def _const_out_spec(shape):
    return pl.BlockSpec(shape, lambda i: (0,) * len(shape), pipeline_mode=pl.Buffered(1))


def _vmem_spec():
    return pl.BlockSpec(memory_space=pltpu.VMEM)


def _smem_spec():
    return pl.BlockSpec(memory_space=pltpu.SMEM)


def _rope_lane_consts():
    half = ROT_DIM // 2
    inv = ROPE_THETA ** (-jnp.arange(half, dtype=F32) * (2.0 / ROT_DIM))
    zeros = jnp.zeros((HEAD_DIM - ROT_DIM,), F32)
    inv_head = jnp.concatenate([inv, inv, zeros])
    sgn_head = jnp.concatenate([-jnp.ones((half,), F32), jnp.ones((half,), F32), zeros])
    reps = LANES // HEAD_DIM
    return jnp.tile(inv_head, reps)[None, :], jnp.tile(sgn_head, reps)[None, :]


def _attn_layer(xp, xs, ck, cv, w_in, sink, w_out, ln_g, ln_b, next_w_in, next_w_out):
    seq = xp.shape[0]
    n_batch, n_cache = ck.shape[0], ck.shape[2]
    n_new = xs.shape[0] // n_batch
    assert n_cache == LANES and LANES % n_new == 0 and xs.shape[0] % LANES == 0
    inv, sgn = _rope_lane_consts()
    lng, lnb = ln_g[None, :], ln_b[None, :]
    params = pltpu.CompilerParams(dimension_semantics=("arbitrary",), vmem_limit_bytes=VMEM_LIMIT_BYTES)

    rows = xs.shape[0]
    k_rows = 2 * LANES
    h, wi, wo = pl.pallas_call(
        _attn_sample_proj_kernel,
        grid=(D_MODEL // k_rows,),
        in_specs=[pl.BlockSpec((rows, k_rows), lambda k: (0, k)),
                  pl.BlockSpec((k_rows, w_in.shape[1]), lambda k: (k, 0)),
                  pl.BlockSpec((k_rows, w_out.shape[1]), lambda k: (k, 0))],
        out_specs=[_const_out_spec((rows, w_in.shape[1])),
                   pl.BlockSpec((k_rows, w_in.shape[1]), lambda k: (k, 0)),
                   pl.BlockSpec((k_rows, w_out.shape[1]), lambda k: (k, 0))],
        out_shape=[jax.ShapeDtypeStruct((rows, w_in.shape[1]), F32),
                   jax.ShapeDtypeStruct(w_in.shape, BF16),
                   jax.ShapeDtypeStruct(w_out.shape, BF16)],
        compiler_params=params,
        name="attn_sample_proj",
    )(xs, w_in, w_out)

    new_t = pltpu.VMEM((rows // LANES, KV_WIDTH, LANES), F32)
    ys, ksn, vsn = pl.pallas_call(
        functools.partial(_attn_sample_kernel, n_batch=n_batch, n_new=n_new, n_cache=n_cache),
        in_specs=[_vmem_spec()] * 7 + [_smem_spec(), _vmem_spec(), _vmem_spec()],
        out_specs=[_vmem_spec()] * 3,
        out_shape=[jax.ShapeDtypeStruct((rows, D_MODEL), F32),
                   jax.ShapeDtypeStruct(ck.shape, F32),
                   jax.ShapeDtypeStruct(cv.shape, F32)],
        scratch_shapes=[pltpu.VMEM((rows, ATTN_WIDTH), BF16), new_t, new_t,
                        pltpu.VMEM((rows, ATTN_WIDTH), F32)],
        compiler_params=pltpu.CompilerParams(vmem_limit_bytes=VMEM_LIMIT_BYTES),
        name="attn_sample",
    )(xs, ck, cv, h, wo, inv, sgn, sink, lng, lnb)

    tm = ROW_BLOCK
    n_steps = seq // tm
    w_rows = next_w_in.shape[0] // n_steps
    assert next_w_in.shape[0] == next_w_out.shape[0] == w_rows * n_steps and w_rows % (2 * SUBLANES) == 0
    kv_scr = pltpu.VMEM((WIN_ROWS + tm, N_KV_HEADS * LANES), BF16)
    yp, kp, vp, next_wi, next_wo = pl.pallas_call(
        _attn_prompt_kernel,
        grid=(n_steps,),
        in_specs=[pl.BlockSpec((tm, D_MODEL), lambda i: (i, 0)),
                  _const_spec(wi.shape), _const_spec(wo.shape),
                  _const_spec(inv.shape), _const_spec(sgn.shape), _smem_spec(),
                  _const_spec(lng.shape), _const_spec(lnb.shape),
                  pl.BlockSpec((w_rows, next_w_in.shape[1]), lambda i: (i, 0)),
                  pl.BlockSpec((w_rows, next_w_out.shape[1]), lambda i: (i, 0))],
        out_specs=[pl.BlockSpec((tm, D_MODEL), lambda i: (i, 0)),
                   pl.BlockSpec((KV_WIDTH, WIN_ROWS), lambda i: (0, 0)),
                   pl.BlockSpec((KV_WIDTH, WIN_ROWS), lambda i: (0, 0)),
                   pl.BlockSpec((w_rows, next_w_in.shape[1]), lambda i: (i, 0)),
                   pl.BlockSpec((w_rows, next_w_out.shape[1]), lambda i: (i, 0))],
        out_shape=[jax.ShapeDtypeStruct((seq, D_MODEL), F32),
                   jax.ShapeDtypeStruct((KV_WIDTH, WIN_ROWS), F32),
                   jax.ShapeDtypeStruct((KV_WIDTH, WIN_ROWS), F32),
                   jax.ShapeDtypeStruct(next_w_in.shape, BF16),
                   jax.ShapeDtypeStruct(next_w_out.shape, BF16)],
        scratch_shapes=[pltpu.VMEM((tm, ATTN_WIDTH), BF16), kv_scr, kv_scr,
                        pltpu.VMEM((WIN_ROWS + tm, KV_WIDTH), BF16),
                        pltpu.VMEM((tm, ATTN_WIDTH), F32), pltpu.VMEM((4, tm, LANES), F32)],
        compiler_params=params,
        name="attn_prompt",
    )(xp, wi, wo, inv, sgn, sink, lng, lnb, next_w_in, next_w_out)

    return yp, ys, kp, vp, ksn, vsn, next_wi, next_wo


def _conv_layer(xp, xs, state, n_batch, wi, w_dw, b_dw, cg, cb, wo, ln_g, ln_b):
    seq = xp.shape[0]
    n_new = xs.shape[0] // n_batch
    hist = CONV_WIDTH - 1
    assert state.shape[0] == hist * n_batch and xs.shape[0] % CONV_ROW_TILE == 0
    vecs = [b_dw[None, :], cg[None, :], cb[None, :], ln_g[None, :], ln_b[None, :]]

    tm = ROW_BLOCK
    yp, cp = pl.pallas_call(
        _conv_prompt_kernel,
        grid=(seq // tm,),
        in_specs=[pl.BlockSpec((tm, D_MODEL), lambda i: (i, 0)),
                  _const_spec(wi.shape), _const_spec(wo.shape), _const_spec(w_dw.shape)]
                 + [_const_spec(t.shape) for t in vecs],
        out_specs=[pl.BlockSpec((tm, D_MODEL), lambda i: (i, 0)),
                   pl.BlockSpec((hist, D_MODEL), lambda i: (0, 0))],
        out_shape=[jax.ShapeDtypeStruct((seq, D_MODEL), F32),
                   jax.ShapeDtypeStruct((hist, D_MODEL), F32)],
        scratch_shapes=[pltpu.VMEM((D_MODEL // LANES, CONV_HIST + tm, LANES), F32),
                        pltpu.VMEM((tm, D_MODEL), F32)],
        compiler_params=pltpu.CompilerParams(dimension_semantics=("arbitrary",),
                                             vmem_limit_bytes=VMEM_LIMIT_BYTES),
        name="conv_prompt",
    )(xp, wi, wo, w_dw, *vecs)

    rows = xs.shape[0]
    ys, cs = pl.pallas_call(
        functools.partial(_conv_sample_kernel, n_batch=n_batch, n_new=n_new),
        in_specs=[_vmem_spec()] * 10,
        out_specs=[_vmem_spec()] * 2,
        out_shape=[jax.ShapeDtypeStruct((rows, D_MODEL), F32),
                   jax.ShapeDtypeStruct(state.shape, F32)],
        scratch_shapes=[pltpu.VMEM((D_MODEL // LANES, n_batch * (CONV_HIST + n_new), LANES), F32),
                        pltpu.VMEM((D_MODEL // LANES, rows, LANES), F32),
                        pltpu.VMEM((rows, D_MODEL), F32)],
        compiler_params=pltpu.CompilerParams(vmem_limit_bytes=VMEM_LIMIT_BYTES),
        name="conv_sample",
    )(xs, state, wi, wo, w_dw, *vecs)
    return yp, ys, cp, cs


def kernel(x_prompt, x_sample, cache_k, cache_v, state_conv, attn_w_in, attn_sink, attn_w_out,
           conv_w_in, conv_w_dw, conv_b_dw, conv_ln_g, conv_ln_b, conv_w_out, post_ln_g, post_ln_b):
    batch, seq, _ = x_prompt.shape
    dec_batch, dec_seq, _ = x_sample.shape
    wc = cache_k.shape[2]
    assert batch == 1 and wc == WIN_ROWS and seq % ROW_BLOCK == 0
    xp = x_prompt.reshape(seq, D_MODEL)
    xs = x_sample.reshape(dec_batch * dec_seq, D_MODEL)

    def feature_major(c):
        return jnp.transpose(c, (0, 2, 3, 1)).reshape(c.shape[0], KV_WIDTH, wc)

    def frame_major(c_t, n):
        return jnp.transpose(c_t.reshape(n, N_KV_HEADS, HEAD_DIM, wc), (0, 3, 1, 2))[None]

    xp, xs, kp, vp, ksn, vsn, conv_wi, conv_wo = _attn_layer(
        xp, xs, feature_major(cache_k[0]), feature_major(cache_v[0]),
        attn_w_in[0], attn_sink[0], attn_w_out[0], post_ln_g[0], post_ln_b[0],
        conv_w_in[0], conv_w_out[0])
    hist = state_conv.shape[2]
    state = jnp.transpose(state_conv[0], (1, 0, 2)).reshape(hist * dec_batch, D_MODEL)
    xp, xs, cp, cs = _conv_layer(
        xp, xs, state, dec_batch, conv_wi, conv_w_dw[0], conv_b_dw[0], conv_ln_g[0], conv_ln_b[0],
        conv_wo, post_ln_g[1], post_ln_b[1])
    cs = jnp.transpose(cs.reshape(hist, dec_batch, D_MODEL), (1, 0, 2))

    return (xp.reshape(batch, seq, D_MODEL), xs.reshape(dec_batch, dec_seq, D_MODEL),
            frame_major(kp, 1), frame_major(vp, 1),
            frame_major(ksn, dec_batch), frame_major(vsn, dec_batch),
            cp[None, None], cs[None])
```
